```python
import jax
import jax.numpy as jnp
from jax import lax
import numpy as np


D_MODEL = 1024
BATCH = 2
SEQ = 16384
DEPTH = 2
DEC_BATCH = 16
DEC_SEQ = 64
PAST_LEN = 1024

CHUNK = 64
HEAD_DIM = 64
D_MIX = D_MODEL
D_FOX = D_MIX // 2
C_CONV = D_MIX // 4
D_SB = D_MIX - D_FOX - C_CONV
H_FOX = D_FOX // HEAD_DIM
H_SB = D_SB // HEAD_DIM
CONV_W = 31
D_FF = ((8 * D_MODEL // 3 + 127) // 128) * 128
QBLK = 128
EPS = 1e-6
FORGET_BIAS = 2.0

OFF_QA = 0
OFF_KA = OFF_QA + D_FOX
OFF_VA = OFF_KA + D_FOX
OFF_F = OFF_VA + D_FOX
OFF_GLU = OFF_F + H_FOX
OFF_QC = OFF_GLU + 2 * C_CONV
OFF_KC = OFF_QC + D_SB
OFF_VC = OFF_KC + D_SB
D_IN = OFF_VC + D_SB

kernel_name = 'hybrid_fox_conformerconv_stickbreak_stream_step'


def _rmsnorm(x, g):
    xf = x.astype(jnp.float32)
    y = xf * lax.rsqrt(jnp.mean(xf * xf, axis=-1, keepdims=True) + EPS) * g.astype(jnp.float32)
    return y.astype(x.dtype)


def _layernorm(x, g, b):
    xf = x.astype(jnp.float32)
    mu = jnp.mean(xf, axis=-1, keepdims=True)
    xc = xf - mu
    var = jnp.mean(xc * xc, axis=-1, keepdims=True)
    y = xc * lax.rsqrt(var + EPS) * g.astype(jnp.float32) + b.astype(jnp.float32)
    return y.astype(x.dtype)


def _swiglu(x, w_gate, w_up, w_down):
    return (jax.nn.silu(x @ w_gate) * (x @ w_up)) @ w_down


def _sweep(fn, T, *args):
    if T <= QBLK:
        return fn(args)
    nb = T // QBLK
    split = lambda a: jnp.moveaxis(a.reshape(a.shape[0], nb, QBLK, *a.shape[2:]), 1, 0)
    out = lax.map(fn, tuple(split(a) for a in args))
    out = jnp.moveaxis(out, 0, 1)
    return out.reshape(out.shape[0], T, *out.shape[3:])


def _fox_attention(q, k, v, cum_q, cum_k, pos_q, pos_k):
    scale = HEAD_DIM ** -0.5
    cum_kt = jnp.swapaxes(cum_k, 1, 2)

    def block(args):
        qb, cqb, pb = args
        s = jnp.einsum('bqhd,bkhd->bhqk', qb, k).astype(jnp.float32) * scale
        s = s + jnp.swapaxes(cqb, 1, 2)[..., None] - cum_kt[:, :, None, :]
        mask = (pos_k[None, None, :] <= pb[:, :, None])[:, None]
        p = jax.nn.softmax(jnp.where(mask, s, -jnp.inf), axis=-1)
        return jnp.einsum('bhqk,bkhd->bqhd', p.astype(v.dtype), v)

    return _sweep(block, q.shape[1], q, cum_q, pos_q)


def _stick_breaking_attention(q, k, v, pos_q, pos_k):
    scale = HEAD_DIM ** -0.5

    def block(args):
        qb, pb = args
        z = jnp.einsum('bqhd,bkhd->bhqk', qb, k).astype(jnp.float32) * scale
        mask = (pos_k[None, None, :] < pb[:, :, None])[:, None]
        log1m = jnp.where(mask, jax.nn.log_sigmoid(-z), 0.0)
        suffix = lax.cumsum(log1m, axis=3, reverse=True) - log1m
        a = jnp.where(mask, jnp.exp(jax.nn.log_sigmoid(z) + suffix), 0.0)
        return jnp.einsum('bhqk,bkhd->bqhd', a.astype(v.dtype), v)

    return _sweep(block, q.shape[1], q, pos_q)


def _conformer_conv(a, g, buf, w, b, ln_g, ln_b):
    u = a * jax.nn.sigmoid(g)
    xp = jnp.concatenate([buf.astype(u.dtype), u], axis=1)
    y = lax.conv_general_dilated(xp, w[:, None, :].astype(u.dtype), (1,), 'VALID',
                                 dimension_numbers=('NWC', 'WIO', 'NWC'),
                                 feature_group_count=C_CONV)
    y = jax.nn.silu(_layernorm(y + b, ln_g, ln_b))
    return y, xp[:, xp.shape[1] - (CONV_W - 1):]


def _trunk_layer(x, past_k, past_v, past_lf, past_sk, past_sv, conv_buf,
                 ffn_norm, ffn_gate, ffn_up, ffn_down, mix_norm, w_in, b_forget,
                 conv_w, conv_b, conv_ln_g, conv_ln_b, w_out):
    B, T, _ = x.shape
    P = past_k.shape[1]
    x = x + 0.5 * _swiglu(_rmsnorm(x, ffn_norm[0]), ffn_gate[0], ffn_up[0], ffn_down[0])
    h = _rmsnorm(x, mix_norm)
    z = h @ w_in
    heads = lambda a, n: a.reshape(B, T, n, HEAD_DIM)
    q_f = heads(z[..., OFF_QA:OFF_KA], H_FOX)
    k_f = heads(z[..., OFF_KA:OFF_VA], H_FOX)
    v_f = heads(z[..., OFF_VA:OFF_F], H_FOX)
    logf = jax.nn.log_sigmoid((z[..., OFF_F:OFF_GLU] + b_forget).astype(jnp.float32))
    q_s = heads(z[..., OFF_QC:OFF_KC], H_SB)
    k_s = heads(z[..., OFF_KC:OFF_VC], H_SB)
    v_s = heads(z[..., OFF_VC:D_IN], H_SB)

    pos_k = jnp.arange(P + T)
    pos_q = (P + jnp.arange(T))[None]

    kf_all = jnp.concatenate([past_k.astype(x.dtype), k_f], axis=1)
    vf_all = jnp.concatenate([past_v.astype(x.dtype), v_f], axis=1)
    cum = jnp.cumsum(jnp.concatenate([past_lf.astype(jnp.float32), logf], axis=1), axis=1)
    y_fox = _fox_attention(q_f, kf_all, vf_all, cum[:, P:], cum, pos_q, pos_k)

    y_conv, new_buf = _conformer_conv(z[..., OFF_GLU:OFF_GLU + C_CONV], z[..., OFF_GLU + C_CONV:OFF_QC],
                                      conv_buf, conv_w, conv_b, conv_ln_g, conv_ln_b)

    ks_all = jnp.concatenate([past_sk.astype(x.dtype), k_s], axis=1)
    vs_all = jnp.concatenate([past_sv.astype(x.dtype), v_s], axis=1)
    y_sb = _stick_breaking_attention(q_s, ks_all, vs_all, pos_q, pos_k)

    mixed = jnp.concatenate([y_fox.reshape(B, T, D_FOX), y_conv, y_sb.reshape(B, T, D_SB)], axis=-1)
    x = x + mixed @ w_out
    x = x + 0.5 * _swiglu(_rmsnorm(x, ffn_norm[1]), ffn_gate[1], ffn_up[1], ffn_down[1])
    return x, k_f, v_f, logf.astype(x.dtype), k_s, v_s, new_buf


def setup_inputs(seed: int = 0) -> dict:
    key = jax.random.key(seed)
    ks = jax.random.split(key, 24)
    nrm = lambda k, shape, s=1.0: s * jax.random.normal(k, shape, jnp.float32)
    return {
        'x_prompt': nrm(ks[0], (BATCH, SEQ, D_MODEL)),
        'x_sample': nrm(ks[1], (DEC_BATCH, DEC_SEQ, D_MODEL)),
        'cache_fox_k': nrm(ks[2], (DEPTH, DEC_BATCH, PAST_LEN, H_FOX, HEAD_DIM)),
        'cache_fox_v': nrm(ks[3], (DEPTH, DEC_BATCH, PAST_LEN, H_FOX, HEAD_DIM)),
        'cache_fox_logf': jax.nn.log_sigmoid(nrm(ks[4], (DEPTH, DEC_BATCH, PAST_LEN, H_FOX)) + FORGET_BIAS),
        'cache_sb_k': nrm(ks[5], (DEPTH, DEC_BATCH, PAST_LEN, H_SB, HEAD_DIM)),
        'cache_sb_v': nrm(ks[6], (DEPTH, DEC_BATCH, PAST_LEN, H_SB, HEAD_DIM)),
        'state_conv': nrm(ks[7], (DEPTH, DEC_BATCH, CONV_W - 1, C_CONV), 0.5),
        'ffn_norm': 1.0 + nrm(ks[8], (DEPTH, 2, D_MODEL), 0.02),
        'ffn_gate': nrm(ks[9], (DEPTH, 2, D_MODEL, D_FF), D_MODEL ** -0.5),
        'ffn_up': nrm(ks[10], (DEPTH, 2, D_MODEL, D_FF), D_MODEL ** -0.5),
        'ffn_down': nrm(ks[11], (DEPTH, 2, D_FF, D_MODEL), D_FF ** -0.5),
        'mix_norm': 1.0 + nrm(ks[12], (DEPTH, D_MODEL), 0.02),
        'w_in': nrm(ks[13], (DEPTH, D_MODEL, D_IN), D_MODEL ** -0.5),
        'b_forget': FORGET_BIAS + nrm(ks[14], (DEPTH, H_FOX), 0.1),
        'conv_w': nrm(ks[15], (DEPTH, CONV_W, C_CONV), CONV_W ** -0.5),
        'conv_b': nrm(ks[16], (DEPTH, C_CONV), 0.02),
        'conv_ln_g': 1.0 + nrm(ks[17], (DEPTH, C_CONV), 0.02),
        'conv_ln_b': nrm(ks[18], (DEPTH, C_CONV), 0.02),
        'w_out': nrm(ks[19], (DEPTH, D_MIX, D_MODEL), D_MIX ** -0.5),
        'final_norm': 1.0 + nrm(ks[20], (D_MODEL,), 0.02),
    }


def reference(x_prompt, x_sample, cache_fox_k, cache_fox_v, cache_fox_logf, cache_sb_k, cache_sb_v,
              state_conv, ffn_norm, ffn_gate, ffn_up, ffn_down, mix_norm, w_in, b_forget,
              conv_w, conv_b, conv_ln_g, conv_ln_b, w_out, final_norm):
    bp = x_prompt.shape[0]
    dt = x_prompt.dtype
    empty_kf = jnp.zeros((bp, 0, H_FOX, HEAD_DIM), dt)
    empty_lf = jnp.zeros((bp, 0, H_FOX), dt)
    empty_ks = jnp.zeros((bp, 0, H_SB, HEAD_DIM), dt)
    zero_buf = jnp.zeros((bp, CONV_W - 1, C_CONV), dt)

    xp, xs = x_prompt, x_sample
    p_kf, p_vf, p_lf, p_ks, p_vs, p_cv = [], [], [], [], [], []
    s_kf, s_vf, s_lf, s_ks, s_vs, s_cv = [], [], [], [], [], []
    for l in range(DEPTH):
        w = (ffn_norm[l], ffn_gate[l], ffn_up[l], ffn_down[l], mix_norm[l], w_in[l], b_forget[l],
             conv_w[l], conv_b[l], conv_ln_g[l], conv_ln_b[l], w_out[l])
        xp, kf, vf, lf, ksn, vsn, cv = _trunk_layer(xp, empty_kf, empty_kf, empty_lf, empty_ks, empty_ks,
                                                    zero_buf, *w)
        p_kf.append(kf); p_vf.append(vf); p_lf.append(lf); p_ks.append(ksn); p_vs.append(vsn); p_cv.append(cv)
        xs, kf, vf, lf, ksn, vsn, cv = _trunk_layer(xs, cache_fox_k[l], cache_fox_v[l], cache_fox_logf[l],
                                                    cache_sb_k[l], cache_sb_v[l], state_conv[l], *w)
        s_kf.append(kf); s_vf.append(vf); s_lf.append(lf); s_ks.append(ksn); s_vs.append(vsn); s_cv.append(cv)

    y_prompt = _rmsnorm(xp, final_norm)
    y_sample = _rmsnorm(xs, final_norm)
    new_fox_k_p = jnp.stack(p_kf, 0)
    new_fox_v_p = jnp.stack(p_vf, 0)
    new_fox_logf_p = jnp.stack(p_lf, 0)
    new_sb_k_p = jnp.stack(p_ks, 0)
    new_sb_v_p = jnp.stack(p_vs, 0)
    new_conv_p = jnp.stack(p_cv, 0)
    new_fox_k_s = jnp.stack(s_kf, 0)
    new_fox_v_s = jnp.stack(s_vf, 0)
    new_fox_logf_s = jnp.stack(s_lf, 0)
    new_sb_k_s = jnp.stack(s_ks, 0)
    new_sb_v_s = jnp.stack(s_vs, 0)
    new_conv_s = jnp.stack(s_cv, 0)
    return (y_prompt, y_sample, new_fox_k_p, new_fox_v_p, new_fox_logf_p, new_sb_k_p, new_sb_v_p, new_conv_p,
            new_fox_k_s, new_fox_v_s, new_fox_logf_s, new_sb_k_s, new_sb_v_s, new_conv_s)
```

```python
import functools

import jax
import jax.numpy as jnp
from jax import lax
from jax.experimental import pallas as pl
from jax.experimental.pallas import tpu as pltpu

D_MODEL = 1024
HEAD_DIM = 64
D_FOX = 512
C_CONV = 256
D_SB = 256
H_FOX = 8
H_SB = 4
CONV_W = 31
D_FF = 2816
EPS = 1e-6

LANES = 128
HALO = 32
FF_CHUNK = D_FF // 2
N_SPLIT = 3
VMEM_LIMIT = 56 * 1024 * 1024

ZQ, ZK, ZV, ZF, ZA, ZG, ZQS, ZKS, ZVS, ZEND = 0, 512, 1024, 1536, 1664, 1920, 2176, 2432, 2688, 2944

F32 = jnp.float32
BF16 = jnp.bfloat16


def _dot(a, b):
    return jnp.dot(a, b, preferred_element_type=F32)


def _dot_nt(a, b):
    return lax.dot_general(a, b, (((1,), (1,)), ((), ())), preferred_element_type=F32)


def _sigmoid(x):
    return 1.0 / (1.0 + jnp.exp(-x))


def _softplus(x):
    return jnp.maximum(x, 0.0) + jnp.log1p(jnp.exp(-jnp.abs(x)))


def _rms(x, g):
    return x * lax.rsqrt(jnp.mean(x * x, axis=-1, keepdims=True) + EPS) * g


def _ffn_half(x, g_ref, wg_ref, wu_ref, wd_ref):
    hn = _rms(x, g_ref[...]).astype(BF16)
    y = jnp.zeros_like(x)
    for c in range(D_FF // FF_CHUNK):
        sl = slice(c * FF_CHUNK, (c + 1) * FF_CHUNK)
        g = _dot(hn, wg_ref[:, sl])
        u = _dot(hn, wu_ref[:, sl])
        act = (g * _sigmoid(g) * u).astype(BF16)
        y = y + _dot(act, wd_ref[sl, :])
    return x + 0.5 * y


def _split3(x):
    a = x.astype(BF16)
    r = x - a.astype(F32)
    b = r.astype(BF16)
    c = (r - b.astype(F32)).astype(BF16)
    return a, b, c


def _dense_in_kernel(x_ref, n1_ref, wg_ref, wu_ref, wd_ref, n2_ref, wp_ref, bf_ref,
                     x1_ref, qf_ref, kf_ref, vf_ref, lfp_ref, u_ref, qs_ref, ks_ref, vs_ref,
                     ksb_ref, vsb_ref):
    x1 = _ffn_half(x_ref[...], n1_ref, wg_ref, wu_ref, wd_ref)
    x1_ref[...] = x1
    h = _rms(x1, n2_ref[...]).astype(BF16)
    z = _dot(h, wp_ref[...])
    scale = HEAD_DIM ** -0.5
    qf_ref[...] = (z[:, ZQ:ZK] * scale).astype(BF16)
    kf_ref[...] = z[:, ZK:ZV]
    vf_ref[...] = z[:, ZV:ZF]
    zf = z[:, ZF:ZA] + bf_ref[...]
    lane = lax.broadcasted_iota(jnp.int32, zf.shape, 1)
    lfp_ref[...] = jnp.where(lane < H_FOX, -_softplus(-zf), 0.0)
    u_ref[...] = z[:, ZA:ZG] * _sigmoid(z[:, ZG:ZQS])
    qs_ref[...] = (z[:, ZQS:ZKS] * scale).astype(BF16)
    ks = z[:, ZKS:ZVS]
    vs = z[:, ZVS:ZEND]
    ks_ref[...] = ks
    vs_ref[...] = vs
    ksb_ref[...] = ks.astype(BF16)
    vsb_ref[...] = vs.astype(BF16)


def _const_spec(shape):
    return pl.BlockSpec(shape, lambda *_: (0,) * len(shape), pipeline_mode=pl.Buffered(1))


def _dense_in(x, n1, wg, wu, wd, n2, wp, bfp, tm):
    n = x.shape[0]
    row = lambda w: pl.BlockSpec((tm, w), lambda i: (i, 0))
    widths = (D_MODEL, D_FOX, D_FOX, D_FOX, LANES, C_CONV, D_SB, D_SB, D_SB, D_SB, D_SB)
    dtypes = (F32, BF16, F32, F32, F32, F32, BF16, F32, F32, BF16, BF16)
    return pl.pallas_call(
        _dense_in_kernel,
        grid=(n // tm,),
        in_specs=[row(D_MODEL), _const_spec((1, D_MODEL)), _const_spec((D_MODEL, D_FF)),
                  _const_spec((D_MODEL, D_FF)), _const_spec((D_FF, D_MODEL)), _const_spec((1, D_MODEL)),
                  _const_spec((D_MODEL, ZEND)), _const_spec((1, LANES))],
        out_specs=[row(w) for w in widths],
        out_shape=[jax.ShapeDtypeStruct((n, w), d) for w, d in zip(widths, dtypes)],
        compiler_params=pltpu.CompilerParams(dimension_semantics=("arbitrary",),
                                             vmem_limit_bytes=VMEM_LIMIT),
        name="dense_in",
    )(x, n1, wg, wu, wd, n2, wp, bfp)


def _dense_out_kernel(x1_ref, yf_ref, yc_ref, ys_ref, wo_ref, n_ref, wg_ref, wu_ref, wd_ref, fn_ref,
                      o_ref, *, final):
    x2 = (x1_ref[...] + _dot(yf_ref[...], wo_ref[0:D_FOX, :])
          + _dot(yc_ref[...], wo_ref[D_FOX:D_FOX + C_CONV, :])
          + _dot(ys_ref[...], wo_ref[D_FOX + C_CONV:, :]))
    x3 = _ffn_half(x2, n_ref, wg_ref, wu_ref, wd_ref)
    o_ref[...] = _rms(x3, fn_ref[...]) if final else x3


def _dense_out(x1, yf, yc, ys, wo, n, wg, wu, wd, fn, tm, final):
    nrow = x1.shape[0]
    row = lambda w: pl.BlockSpec((tm, w), lambda i: (i, 0))
    return pl.pallas_call(
        functools.partial(_dense_out_kernel, final=final),
        grid=(nrow // tm,),
        in_specs=[row(D_MODEL), row(D_FOX), row(C_CONV), row(D_SB), _const_spec((D_MODEL, D_MODEL)),
                  _const_spec((1, D_MODEL)), _const_spec((D_MODEL, D_FF)), _const_spec((D_MODEL, D_FF)),
                  _const_spec((D_FF, D_MODEL)), _const_spec((1, D_MODEL))],
        out_specs=row(D_MODEL),
        out_shape=jax.ShapeDtypeStruct((nrow, D_MODEL), F32),
        compiler_params=pltpu.CompilerParams(dimension_semantics=("arbitrary",),
                                             vmem_limit_bytes=VMEM_LIMIT),
        name="dense_out",
    )(x1, yf, yc, ys, wo, n, wg, wu, wd, fn)


def _prep_kernel(kf_ref, vf_ref, lfp_ref, tri_ref, sel_ref, kt_ref, vb_ref, carry_ref):
    @pl.when(pl.program_id(1) == 0)
    def _():
        carry_ref[...] = jnp.zeros_like(carry_ref)

    tri = tri_ref[...]
    cum = carry_ref[...]
    for part in _split3(lfp_ref[0]):
        cum = cum + _dot(tri, part)
    carry_ref[...] = cum[cum.shape[0] - 1:, :]
    aug = None
    for s, part in enumerate(_split3(-cum)):
        term = _dot(part, sel_ref[s])
        aug = term if aug is None else aug + term
    for hp in range(H_FOX // 2):
        base = 2 * LANES * hp
        kt_ref[0, :, base:base + LANES] = kf_ref[0, :, hp * LANES:(hp + 1) * LANES].astype(BF16)
        kt_ref[0, :, base + LANES:base + 2 * LANES] = aug[:, hp * LANES:(hp + 1) * LANES].astype(BF16)
    vb_ref[0] = vf_ref[0].astype(BF16)


def _prep(kf, vf, lfp, tp):
    b, tk, _ = kf.shape
    r = jnp.arange(tp)
    tri = (r[None, :] <= r[:, None]).astype(BF16)
    h = jnp.arange(LANES)
    col = jnp.arange(D_FOX)
    sel = jnp.stack([(col[None, :] == ((h // 2) * LANES + (h % 2) * N_SPLIT + s)[:, None])
                     & (h[:, None] < H_FOX) for s in range(N_SPLIT)]).astype(BF16)
    blk = lambda w: pl.BlockSpec((1, tp, w), lambda bi, i: (bi, i, 0))
    return pl.pallas_call(
        _prep_kernel,
        grid=(b, tk // tp),
        in_specs=[blk(D_FOX), blk(D_FOX), blk(LANES), _const_spec((tp, tp)),
                  _const_spec((N_SPLIT, LANES, D_FOX))],
        out_specs=[blk(2 * D_FOX), blk(D_FOX)],
        out_shape=[jax.ShapeDtypeStruct((b, tk, 2 * D_FOX), BF16),
                   jax.ShapeDtypeStruct((b, tk, D_FOX), BF16)],
        scratch_shapes=[pltpu.VMEM((1, LANES), F32)],
        compiler_params=pltpu.CompilerParams(dimension_semantics=("arbitrary", "arbitrary"),
                                             vmem_limit_bytes=VMEM_LIMIT),
        name="prep",
    )(kf, vf, lfp, tri, sel)


def _conv_kernel(u_ref, prev_ref, buf_ref, w_ref, b_ref, g_ref, beta_ref, y_ref, xw_ref, *, tt, rows):
    first = pl.program_id(1) == 0
    xw_ref[0:HALO, :] = jnp.where(first, buf_ref[0], prev_ref[0])
    xw_ref[HALO:HALO + tt, :] = u_ref[0]
    off = HALO - (CONV_W - 1)
    for r0 in range(0, tt, rows):
        acc = jnp.zeros((rows, C_CONV), F32)
        for j in range(CONV_W):
            acc = acc + w_ref[j:j + 1, :] * xw_ref[r0 + j + off:r0 + j + off + rows, :]
        y = acc + b_ref[...]
        mu = jnp.mean(y, axis=-1, keepdims=True)
        yc = y - mu
        var = jnp.mean(yc * yc, axis=-1, keepdims=True)
        y = yc * lax.rsqrt(var + EPS) * g_ref[...] + beta_ref[...]
        y_ref[0, r0:r0 + rows, :] = (y * _sigmoid(y)).astype(BF16)


def _conv(u, buf_pad, w, b, g, beta, tt):
    bsz, t, _ = u.shape
    rows = min(tt, 64)
    per = tt // HALO
    return pl.pallas_call(
        functools.partial(_conv_kernel, tt=tt, rows=rows),
        grid=(bsz, t // tt),
        in_specs=[pl.BlockSpec((1, tt, C_CONV), lambda bi, i: (bi, i, 0)),
                  pl.BlockSpec((1, HALO, C_CONV), lambda bi, i: (bi, jnp.maximum(i * per - 1, 0), 0)),
                  pl.BlockSpec((1, HALO, C_CONV), lambda bi, i: (bi, 0, 0)),
                  _const_spec((CONV_W, C_CONV)), _const_spec((1, C_CONV)), _const_spec((1, C_CONV)),
                  _const_spec((1, C_CONV))],
        out_specs=pl.BlockSpec((1, tt, C_CONV), lambda bi, i: (bi, i, 0)),
        out_shape=jax.ShapeDtypeStruct((bsz, t, C_CONV), BF16),
        scratch_shapes=[pltpu.VMEM((HALO + tt, C_CONV), F32)],
        compiler_params=pltpu.CompilerParams(dimension_semantics=("arbitrary", "arbitrary")),
        name="conv",
    )(u, u, buf_pad, w, b, g, beta)


def _head_masks(q2):
    lane = lax.broadcasted_iota(jnp.int32, q2.shape, 1)
    q2 = q2.astype(F32)
    return (lane, jnp.where(lane < HEAD_DIM, q2, 0.0).astype(BF16),
            jnp.where(lane >= HEAD_DIM, q2, 0.0).astype(BF16))


def _diag_block(i, tq, tk, past):
    return (past + i * tq + tq - 1) // tk


def _fox_kernel(q_ref, k_ref, v_ref, o_ref, m_ref, l_ref, acc_ref, *, tq, tk, past):
    i = pl.program_id(2)
    jd = _diag_block(i, tq, tk, past)
    lane, qa, qb = _head_masks(q_ref[0])
    ones_a = jnp.where(lane < N_SPLIT, 1.0, 0.0).astype(BF16)
    ones_b = jnp.where(lane < 2 * N_SPLIT, 1.0, 0.0).astype(BF16) - ones_a
    q_heads = (jnp.concatenate([qa, ones_a], axis=1), jnp.concatenate([qb, ones_b], axis=1))
    row_pos = past + i * tq + lax.broadcasted_iota(jnp.int32, (tq, tk), 0)
    col = lax.broadcasted_iota(jnp.int32, (tq, tk), 1)

    m_ref[...] = jnp.full_like(m_ref, -jnp.inf)
    l_ref[...] = jnp.zeros_like(l_ref)
    acc_ref[...] = jnp.zeros_like(acc_ref)

    def step(j, masked):
        start = pl.multiple_of(j * tk, tk)
        kblk = k_ref[0, pl.ds(start, tk), :]
        vblk = v_ref[0, pl.ds(start, tk), :]
        for hh, qh in enumerate(q_heads):
            s = _dot_nt(qh, kblk)
            if masked:
                s = jnp.where(start + col <= row_pos, s, -jnp.inf)
            m_prev = m_ref[hh]
            m_new = jnp.maximum(m_prev, jnp.max(s, axis=1, keepdims=True))
            alpha = jnp.exp(m_prev - m_new)
            p = jnp.exp(s - m_new)
            l_ref[hh] = alpha * l_ref[hh] + jnp.sum(p, axis=1, keepdims=True)
            acc_ref[hh] = alpha * acc_ref[hh] + _dot(p.astype(BF16), vblk)
            m_ref[hh] = m_new

    step(jd, True)

    def body(t, carry):
        step(jd - 1 - t, False)
        return carry

    lax.fori_loop(0, jd, body, 0)
    out = jnp.where(lane < HEAD_DIM, acc_ref[0] / l_ref[0], acc_ref[1] / l_ref[1])
    o_ref[0] = out.astype(BF16)


def _fox(q, kt, vb, tq, tk):
    b, t, _ = q.shape
    tkeys = kt.shape[1]
    past = tkeys - t
    assert (tq == tk and past % tk == 0) or tk == tkeys
    return pl.pallas_call(
        functools.partial(_fox_kernel, tq=tq, tk=tk, past=past),
        grid=(b, H_FOX // 2, t // tq),
        in_specs=[pl.BlockSpec((1, tq, LANES), lambda bi, hp, i: (bi, i, hp)),
                  pl.BlockSpec((1, tkeys, 2 * LANES), lambda bi, hp, i: (bi, 0, hp)),
                  pl.BlockSpec((1, tkeys, LANES), lambda bi, hp, i: (bi, 0, hp))],
        out_specs=pl.BlockSpec((1, tq, LANES), lambda bi, hp, i: (bi, i, hp)),
        out_shape=jax.ShapeDtypeStruct((b, t, D_FOX), BF16),
        scratch_shapes=[pltpu.VMEM((2, tq, 1), F32), pltpu.VMEM((2, tq, 1), F32),
                        pltpu.VMEM((2, tq, LANES), F32)],
        compiler_params=pltpu.CompilerParams(dimension_semantics=("arbitrary",) * 3,
                                             vmem_limit_bytes=VMEM_LIMIT),
        name="fox",
    )(q, kt, vb)


def _sb_kernel(q_ref, k_ref, v_ref, tri_ref, o_ref, carry_ref, acc_ref, *, tq, tk, past):
    i = pl.program_id(2)
    jd = _diag_block(i, tq, tk, past)
    lane, qa, qb = _head_masks(q_ref[0])
    row_pos = past + i * tq + lax.broadcasted_iota(jnp.int32, (tq, tk), 0)
    col = lax.broadcasted_iota(jnp.int32, (tq, tk), 1)

    carry_ref[...] = jnp.zeros_like(carry_ref)
    acc_ref[...] = jnp.zeros_like(acc_ref)

    def step(j, masked):
        start = pl.multiple_of(j * tk, tk)
        kblk = k_ref[0, pl.ds(start, tk), :]
        vblk = v_ref[0, pl.ds(start, tk), :]
        tri = tri_ref[...]
        for hh, qh in enumerate((qa, qb)):
            z = _dot_nt(qh, kblk)
            sp = _softplus(z)
            log1m = -sp
            if masked:
                valid = start + col < row_pos
                log1m = jnp.where(valid, log1m, 0.0)
            hi = log1m.astype(BF16)
            lo = (log1m - hi.astype(F32)).astype(BF16)
            suffix = _dot(hi, tri) + _dot(lo, tri)
            later = carry_ref[hh]
            a = jnp.exp(z - sp + suffix + later)
            if masked:
                a = jnp.where(valid, a, 0.0)
            acc_ref[hh] = acc_ref[hh] + _dot(a.astype(BF16), vblk)
            carry_ref[hh] = later + jnp.sum(log1m, axis=1, keepdims=True)

    step(jd, True)

    def body(t, carry):
        step(jd - 1 - t, False)
        return carry

    lax.fori_loop(0, jd, body, 0)
    o_ref[0] = jnp.where(lane < HEAD_DIM, acc_ref[0], acc_ref[1]).astype(BF16)


def _sb(q, k, v, tq, tk):
    b, t, _ = q.shape
    tkeys = k.shape[1]
    past = tkeys - t
    assert (tq == tk and past % tk == 0) or tk == tkeys
    r = jnp.arange(tk)
    tri = (r[:, None] > r[None, :]).astype(BF16)
    return pl.pallas_call(
        functools.partial(_sb_kernel, tq=tq, tk=tk, past=past),
        grid=(b, H_SB // 2, t // tq),
        in_specs=[pl.BlockSpec((1, tq, LANES), lambda bi, hp, i: (bi, i, hp)),
                  pl.BlockSpec((1, tkeys, LANES), lambda bi, hp, i: (bi, 0, hp)),
                  pl.BlockSpec((1, tkeys, LANES), lambda bi, hp, i: (bi, 0, hp)),
                  _const_spec((tk, tk))],
        out_specs=pl.BlockSpec((1, tq, LANES), lambda bi, hp, i: (bi, i, hp)),
        out_shape=jax.ShapeDtypeStruct((b, t, D_SB), BF16),
        scratch_shapes=[pltpu.VMEM((2, tq, 1), F32), pltpu.VMEM((2, tq, LANES), F32)],
        compiler_params=pltpu.CompilerParams(dimension_semantics=("arbitrary",) * 3,
                                             vmem_limit_bytes=VMEM_LIMIT),
        name="sb",
    )(q, k, v, tri)


def _pick_tile(n, pref):
    t = min(n, pref)
    while n % t:
        t //= 2
    return t


def _layer(x, past, w, final_norm, final):
    b, t, _ = x.shape
    past_k, past_v, past_lf, past_sk, past_sv, conv_buf = past
    p = past_k.shape[1]
    assert t >= CONV_W - 1 and t % HALO == 0
    n = b * t
    tm = _pick_tile(n, 256)

    (x1, qf, kf, vf, lfp, u, qs, ks, vs, ksb, vsb) = _dense_in(
        x.reshape(n, D_MODEL), w["n1"], w["wg1"], w["wu1"], w["wd1"], w["n2"], w["wp"], w["bfp"], tm)

    r3 = lambda a: a.reshape(b, t, a.shape[-1])
    kf3, vf3, lfp3, u3 = r3(kf), r3(vf), r3(lfp), r3(u)
    if p:
        kf_all = jnp.concatenate([past_k.reshape(b, p, D_FOX), kf3], axis=1)
        vf_all = jnp.concatenate([past_v.reshape(b, p, D_FOX), vf3], axis=1)
        lf_all = jnp.concatenate([jnp.pad(past_lf, ((0, 0), (0, 0), (0, LANES - H_FOX))), lfp3], axis=1)
        ks_all = jnp.concatenate([past_sk.reshape(b, p, D_SB).astype(BF16), r3(ksb)], axis=1)
        vs_all = jnp.concatenate([past_sv.reshape(b, p, D_SB).astype(BF16), r3(vsb)], axis=1)
        tq = t
        tk = p + t
        tp = p + t
    else:
        kf_all, vf_all, lf_all, ks_all, vs_all = kf3, vf3, lfp3, r3(ksb), r3(vsb)
        tq = tk = _pick_tile(t, 256)
        tp = _pick_tile(t, 512)

    kt, vb = _prep(kf_all, vf_all, lf_all, tp)
    yf = _fox(r3(qf), kt, vb, tq, tk)
    ys = _sb(r3(qs), ks_all, vs_all, tq, tk)
    buf_pad = jnp.pad(conv_buf, ((0, 0), (HALO - (CONV_W - 1), 0), (0, 0)))
    yc = _conv(u3, buf_pad, w["conv_w"], w["conv_b"], w["ln_g"], w["ln_b"], _pick_tile(t, 256))

    flat = lambda a: a.reshape(n, a.shape[-1])
    xo = _dense_out(x1, flat(yf), flat(yc), flat(ys), w["wo"], w["n3"], w["wg2"], w["wu2"], w["wd2"],
                    final_norm, tm, final)
    new = (kf3.reshape(b, t, H_FOX, HEAD_DIM), vf3.reshape(b, t, H_FOX, HEAD_DIM), lfp3[..., :H_FOX],
           r3(ks).reshape(b, t, H_SB, HEAD_DIM), r3(vs).reshape(b, t, H_SB, HEAD_DIM),
           u3[:, t - (CONV_W - 1):, :])
    return xo.reshape(b, t, D_MODEL), new


def _layer_weights(l, ffn_norm, ffn_gate, ffn_up, ffn_down, mix_norm, w_in, b_forget,
                   conv_w, conv_b, conv_ln_g, conv_ln_b, w_out):
    wi = w_in[l]
    off_f = 3 * D_FOX
    off_glu = off_f + H_FOX
    off_qc = off_glu + 2 * C_CONV
    wp = jnp.concatenate([wi[:, :off_f], jnp.pad(wi[:, off_f:off_glu], ((0, 0), (0, LANES - H_FOX))),
                          wi[:, off_glu:off_qc], wi[:, off_qc:]], axis=1).astype(BF16)
    row = lambda a: a.reshape(1, -1).astype(F32)
    return dict(
        n1=row(ffn_norm[l, 0]), wg1=ffn_gate[l, 0].astype(BF16), wu1=ffn_up[l, 0].astype(BF16),
        wd1=ffn_down[l, 0].astype(BF16), n2=row(mix_norm[l]), wp=wp,
        bfp=jnp.pad(row(b_forget[l]), ((0, 0), (0, LANES - H_FOX))),
        conv_w=conv_w[l], conv_b=row(conv_b[l]), ln_g=row(conv_ln_g[l]), ln_b=row(conv_ln_b[l]),
        wo=w_out[l].astype(BF16), n3=row(ffn_norm[l, 1]), wg2=ffn_gate[l, 1].astype(BF16),
        wu2=ffn_up[l, 1].astype(BF16), wd2=ffn_down[l, 1].astype(BF16))


def kernel(x_prompt, x_sample, cache_fox_k, cache_fox_v, cache_fox_logf, cache_sb_k, cache_sb_v, state_conv,
           ffn_norm, ffn_gate, ffn_up, ffn_down, mix_norm, w_in, b_forget, conv_w, conv_b, conv_ln_g,
           conv_ln_b, w_out, final_norm):
    depth = w_in.shape[0]
    bp = x_prompt.shape[0]
    dt = x_prompt.dtype
    empty = (jnp.zeros((bp, 0, H_FOX, HEAD_DIM), dt), jnp.zeros((bp, 0, H_FOX, HEAD_DIM), dt),
             jnp.zeros((bp, 0, H_FOX), dt), jnp.zeros((bp, 0, H_SB, HEAD_DIM), dt),
             jnp.zeros((bp, 0, H_SB, HEAD_DIM), dt), jnp.zeros((bp, CONV_W - 1, C_CONV), dt))
    fn = final_norm.reshape(1, -1).astype(F32)
    xp, xs = x_prompt, x_sample
    new_p, new_s = [], []
    for l in range(depth):
        w = _layer_weights(l, ffn_norm, ffn_gate, ffn_up, ffn_down, mix_norm, w_in, b_forget,
                           conv_w, conv_b, conv_ln_g, conv_ln_b, w_out)
        final = l == depth - 1
        xp, np_ = _layer(xp, empty, w, fn, final)
        cache = (cache_fox_k[l], cache_fox_v[l], cache_fox_logf[l], cache_sb_k[l], cache_sb_v[l],
                 state_conv[l])
        xs, ns_ = _layer(xs, cache, w, fn, final)
        new_p.append(np_)
        new_s.append(ns_)
    stack = lambda items, k: jnp.stack([it[k] for it in items], 0)
    return (xp, xs) + tuple(stack(new_p, k) for k in range(6)) + tuple(stack(new_s, k) for k in range(6))
```

```python
import functools

import jax
import jax.numpy as jnp
from jax import lax
from jax.experimental import pallas as pl
from jax.experimental.pallas import tpu as pltpu

D_MODEL = 1024
HEAD_DIM = 64
D_FOX = 512
C_CONV = 256
D_SB = 256
H_FOX = 8
H_SB = 4
CONV_W = 31
D_FF = 2816
EPS = 1e-6

LANES = 128
HALO = 32
FF_CHUNK = D_FF // 2
N_SPLIT = 3
VMEM_LIMIT = 56 * 1024 * 1024
SKIP_LOG = 110.0
NORM_SLACK = 1.01

ZQ, ZK, ZV, ZF, ZA, ZG, ZQS, ZKS, ZVS, ZEND = 0, 512, 1024, 1536, 1664, 1920, 2176, 2432, 2688, 2944

F32 = jnp.float32
BF16 = jnp.bfloat16


def _dot(a, b):
    return jnp.dot(a, b, preferred_element_type=F32)


def _dot_nt(a, b):
    return lax.dot_general(a, b, (((1,), (1,)), ((), ())), preferred_element_type=F32)


def _sigmoid(x):
    return 1.0 / (1.0 + jnp.exp(-x))


def _softplus(x):
    return jnp.maximum(x, 0.0) + jnp.log1p(jnp.exp(-jnp.abs(x)))


def _rms(x, g):
    return x * lax.rsqrt(jnp.mean(x * x, axis=-1, keepdims=True) + EPS) * g


def _ffn_half(x, g_ref, wg_ref, wu_ref, wd_ref):
    hn = _rms(x, g_ref[...]).astype(BF16)
    y = jnp.zeros_like(x)
    for c in range(D_FF // FF_CHUNK):
        sl = slice(c * FF_CHUNK, (c + 1) * FF_CHUNK)
        g = _dot(hn, wg_ref[:, sl])
        u = _dot(hn, wu_ref[:, sl])
        act = (g * _sigmoid(g) * u).astype(BF16)
        y = y + _dot(act, wd_ref[sl, :])
    return x + 0.5 * y


def _split3(x):
    a = x.astype(BF16)
    r = x - a.astype(F32)
    b = r.astype(BF16)
    c = (r - b.astype(F32)).astype(BF16)
    return a, b, c


def _dense_in_kernel(x_ref, n1_ref, wg_ref, wu_ref, wd_ref, n2_ref, wp_ref, bf_ref,
                     x1_ref, qf_ref, kf_ref, vf_ref, lfp_ref, u_ref, qs_ref, ks_ref, vs_ref,
                     ksb_ref, vsb_ref):
    x1 = _ffn_half(x_ref[...], n1_ref, wg_ref, wu_ref, wd_ref)
    x1_ref[...] = x1
    h = _rms(x1, n2_ref[...]).astype(BF16)
    z = _dot(h, wp_ref[...])
    scale = HEAD_DIM ** -0.5
    qf_ref[...] = (z[:, ZQ:ZK] * scale).astype(BF16)
    kf_ref[...] = z[:, ZK:ZV]
    vf_ref[...] = z[:, ZV:ZF]
    zf = z[:, ZF:ZA] + bf_ref[...]
    lane = lax.broadcasted_iota(jnp.int32, zf.shape, 1)
    lfp_ref[...] = jnp.where(lane < H_FOX, -_softplus(-zf), 0.0)
    u_ref[...] = z[:, ZA:ZG] * _sigmoid(z[:, ZG:ZQS])
    qs_ref[...] = (z[:, ZQS:ZKS] * scale).astype(BF16)
    ks = z[:, ZKS:ZVS]
    vs = z[:, ZVS:ZEND]
    ks_ref[...] = ks
    vs_ref[...] = vs
    ksb_ref[...] = ks.astype(BF16)
    vsb_ref[...] = vs.astype(BF16)


def _const_spec(shape):
    return pl.BlockSpec(shape, lambda *_: (0,) * len(shape), pipeline_mode=pl.Buffered(1))


def _dense_in(x, n1, wg, wu, wd, n2, wp, bfp, tm):
    n = x.shape[0]
    row = lambda w: pl.BlockSpec((tm, w), lambda i: (i, 0))
    widths = (D_MODEL, D_FOX, D_FOX, D_FOX, LANES, C_CONV, D_SB, D_SB, D_SB, D_SB, D_SB)
    dtypes = (F32, BF16, F32, F32, F32, F32, BF16, F32, F32, BF16, BF16)
    return pl.pallas_call(
        _dense_in_kernel,
        grid=(n // tm,),
        in_specs=[row(D_MODEL), _const_spec((1, D_MODEL)), _const_spec((D_MODEL, D_FF)),
                  _const_spec((D_MODEL, D_FF)), _const_spec((D_FF, D_MODEL)), _const_spec((1, D_MODEL)),
                  _const_spec((D_MODEL, ZEND)), _const_spec((1, LANES))],
        out_specs=[row(w) for w in widths],
        out_shape=[jax.ShapeDtypeStruct((n, w), d) for w, d in zip(widths, dtypes)],
        compiler_params=pltpu.CompilerParams(dimension_semantics=("arbitrary",),
                                             vmem_limit_bytes=VMEM_LIMIT),
        name="dense_in",
    )(x, n1, wg, wu, wd, n2, wp, bfp)


def _dense_out_kernel(x1_ref, yf_ref, yc_ref, ys_ref, wo_ref, n_ref, wg_ref, wu_ref, wd_ref, fn_ref,
                      o_ref, *, final):
    x2 = (x1_ref[...] + _dot(yf_ref[...], wo_ref[0:D_FOX, :])
          + _dot(yc_ref[...], wo_ref[D_FOX:D_FOX + C_CONV, :])
          + _dot(ys_ref[...], wo_ref[D_FOX + C_CONV:, :]))
    x3 = _ffn_half(x2, n_ref, wg_ref, wu_ref, wd_ref)
    o_ref[...] = _rms(x3, fn_ref[...]) if final else x3


def _dense_out(x1, yf, yc, ys, wo, n, wg, wu, wd, fn, tm, final):
    nrow = x1.shape[0]
    row = lambda w: pl.BlockSpec((tm, w), lambda i: (i, 0))
    return pl.pallas_call(
        functools.partial(_dense_out_kernel, final=final),
        grid=(nrow // tm,),
        in_specs=[row(D_MODEL), row(D_FOX), row(C_CONV), row(D_SB), _const_spec((D_MODEL, D_MODEL)),
                  _const_spec((1, D_MODEL)), _const_spec((D_MODEL, D_FF)), _const_spec((D_MODEL, D_FF)),
                  _const_spec((D_FF, D_MODEL)), _const_spec((1, D_MODEL))],
        out_specs=row(D_MODEL),
        out_shape=jax.ShapeDtypeStruct((nrow, D_MODEL), F32),
        compiler_params=pltpu.CompilerParams(dimension_semantics=("arbitrary",),
                                             vmem_limit_bytes=VMEM_LIMIT),
        name="dense_out",
    )(x1, yf, yc, ys, wo, n, wg, wu, wd, fn)


def _prep_kernel(kf_ref, vf_ref, lfp_ref, tri_ref, sel_ref, hsel_ref,
                 kt_ref, vb_ref, bias_max_ref, knorm_max_ref, carry_ref, bmax_ref, kmax_ref):
    i = pl.program_id(1)

    @pl.when(i == 0)
    def _():
        carry_ref[...] = jnp.zeros_like(carry_ref)
        bmax_ref[...] = jnp.full_like(bmax_ref, -jnp.inf)
        kmax_ref[...] = jnp.zeros_like(kmax_ref)

    tri = tri_ref[...]
    cum = carry_ref[...]
    for part in _split3(lfp_ref[0]):
        cum = cum + _dot(tri, part)
    carry_ref[...] = cum[cum.shape[0] - 1:, :]
    bias = -cum
    aug = None
    for s, part in enumerate(_split3(bias)):
        term = _dot(part, sel_ref[s])
        aug = term if aug is None else aug + term
    kb = kf_ref[0].astype(BF16)
    for hp in range(H_FOX // 2):
        base = 2 * LANES * hp
        kt_ref[0, :, base:base + LANES] = kb[:, hp * LANES:(hp + 1) * LANES]
        kt_ref[0, :, base + LANES:base + 2 * LANES] = aug[:, hp * LANES:(hp + 1) * LANES].astype(BF16)
    vb_ref[0] = vf_ref[0].astype(BF16)

    ksq = kb.astype(F32)
    ksq = ksq * ksq
    hi = ksq.astype(BF16)
    lo = (ksq - hi.astype(F32)).astype(BF16)
    norm2 = _dot(hi, hsel_ref[...]) + _dot(lo, hsel_ref[...])
    kmax = jnp.maximum(kmax_ref[...], jnp.max(norm2, axis=0, keepdims=True))
    bmax = jnp.maximum(bmax_ref[...], jnp.max(bias, axis=0, keepdims=True))
    kmax_ref[...] = kmax
    bmax_ref[...] = bmax
    knorm_max_ref[0, pl.ds(i, 1), :] = jnp.sqrt(kmax) * NORM_SLACK
    bias_max_ref[0, pl.ds(i, 1), :] = bmax


def _prep(kf, vf, lfp, tp):
    b, tk, _ = kf.shape
    nblk = tk // tp
    r = jnp.arange(tp)
    tri = (r[None, :] <= r[:, None]).astype(BF16)
    h = jnp.arange(LANES)
    col = jnp.arange(D_FOX)
    sel = jnp.stack([(col[None, :] == ((h // 2) * LANES + (h % 2) * N_SPLIT + s)[:, None])
                     & (h[:, None] < H_FOX) for s in range(N_SPLIT)]).astype(BF16)
    hsel = (col[:, None] // HEAD_DIM == h[None, :]).astype(BF16)
    blk = lambda w: pl.BlockSpec((1, tp, w), lambda bi, i: (bi, i, 0))
    per_batch = pl.BlockSpec((1, nblk, LANES), lambda bi, i: (bi, 0, 0))
    return pl.pallas_call(
        _prep_kernel,
        grid=(b, nblk),
        in_specs=[blk(D_FOX), blk(D_FOX), blk(LANES), _const_spec((tp, tp)),
                  _const_spec((N_SPLIT, LANES, D_FOX)), _const_spec((D_FOX, LANES))],
        out_specs=[blk(2 * D_FOX), blk(D_FOX), per_batch, per_batch],
        out_shape=[jax.ShapeDtypeStruct((b, tk, 2 * D_FOX), BF16),
                   jax.ShapeDtypeStruct((b, tk, D_FOX), BF16),
                   jax.ShapeDtypeStruct((b, nblk, LANES), F32),
                   jax.ShapeDtypeStruct((b, nblk, LANES), F32)],
        scratch_shapes=[pltpu.VMEM((1, LANES), F32)] * 3,
        compiler_params=pltpu.CompilerParams(dimension_semantics=("arbitrary", "arbitrary"),
                                             vmem_limit_bytes=VMEM_LIMIT),
        name="prep",
    )(kf, vf, lfp, tri, sel, hsel)


def _conv_kernel(u_ref, prev_ref, buf_ref, w_ref, b_ref, g_ref, beta_ref, y_ref, xw_ref, *, tt, rows):
    first = pl.program_id(1) == 0
    xw_ref[0:HALO, :] = jnp.where(first, buf_ref[0], prev_ref[0])
    xw_ref[HALO:HALO + tt, :] = u_ref[0]
    off = HALO - (CONV_W - 1)
    for r0 in range(0, tt, rows):
        acc = jnp.zeros((rows, C_CONV), F32)
        for j in range(CONV_W):
            acc = acc + w_ref[j:j + 1, :] * xw_ref[r0 + j + off:r0 + j + off + rows, :]
        y = acc + b_ref[...]
        mu = jnp.mean(y, axis=-1, keepdims=True)
        yc = y - mu
        var = jnp.mean(yc * yc, axis=-1, keepdims=True)
        y = yc * lax.rsqrt(var + EPS) * g_ref[...] + beta_ref[...]
        y_ref[0, r0:r0 + rows, :] = (y * _sigmoid(y)).astype(BF16)


def _conv(u, buf_pad, w, b, g, beta, tt):
    bsz, t, _ = u.shape
    rows = min(tt, 64)
    per = tt // HALO
    return pl.pallas_call(
        functools.partial(_conv_kernel, tt=tt, rows=rows),
        grid=(bsz, t // tt),
        in_specs=[pl.BlockSpec((1, tt, C_CONV), lambda bi, i: (bi, i, 0)),
                  pl.BlockSpec((1, HALO, C_CONV), lambda bi, i: (bi, jnp.maximum(i * per - 1, 0), 0)),
                  pl.BlockSpec((1, HALO, C_CONV), lambda bi, i: (bi, 0, 0)),
                  _const_spec((CONV_W, C_CONV)), _const_spec((1, C_CONV)), _const_spec((1, C_CONV)),
                  _const_spec((1, C_CONV))],
        out_specs=pl.BlockSpec((1, tt, C_CONV), lambda bi, i: (bi, i, 0)),
        out_shape=jax.ShapeDtypeStruct((bsz, t, C_CONV), BF16),
        scratch_shapes=[pltpu.VMEM((HALO + tt, C_CONV), F32)],
        compiler_params=pltpu.CompilerParams(dimension_semantics=("arbitrary", "arbitrary")),
        name="conv",
    )(u, u, buf_pad, w, b, g, beta)


def _head_masks(q2):
    lane = lax.broadcasted_iota(jnp.int32, q2.shape, 1)
    q2 = q2.astype(F32)
    return (lane, jnp.where(lane < HEAD_DIM, q2, 0.0).astype(BF16),
            jnp.where(lane >= HEAD_DIM, q2, 0.0).astype(BF16))


def _diag_block(i, tq, tk, past):
    return (past + i * tq + tq - 1) // tk


def _fox_kernel(bias_max_ref, knorm_max_ref, q_ref, k_ref, v_ref, o_ref, m_ref, l_ref, acc_ref,
                *, tq, tk, past, nblk):
    bi = pl.program_id(0)
    hp = pl.program_id(1)
    i = pl.program_id(2)
    jd = _diag_block(i, tq, tk, past)
    lane, qa, qb = _head_masks(q_ref[0])
    q_norms = []
    for qh in (qa, qb):
        qf = qh.astype(F32)
        q_norms.append(jnp.sqrt(jnp.sum(qf * qf, axis=1, keepdims=True)) * NORM_SLACK)
    ones_a = jnp.where(lane < N_SPLIT, 1.0, 0.0).astype(BF16)
    ones_b = jnp.where(lane < 2 * N_SPLIT, 1.0, 0.0).astype(BF16) - ones_a
    q_heads = (jnp.concatenate([qa, ones_a], axis=1), jnp.concatenate([qb, ones_b], axis=1))
    row_pos = past + i * tq + lax.broadcasted_iota(jnp.int32, (tq, tk), 0)
    col = lax.broadcasted_iota(jnp.int32, (tq, tk), 1)

    m_ref[...] = jnp.full_like(m_ref, -jnp.inf)
    l_ref[...] = jnp.zeros_like(l_ref)
    acc_ref[...] = jnp.zeros_like(acc_ref)

    def step(j, masked):
        start = pl.multiple_of(j * tk, tk)
        kblk = k_ref[0, pl.ds(start, tk), :]
        vblk = v_ref[0, pl.ds(start, tk), :]
        for hh, qh in enumerate(q_heads):
            s = _dot_nt(qh, kblk)
            if masked:
                s = jnp.where(start + col <= row_pos, s, -jnp.inf)
            m_prev = m_ref[hh]
            m_new = jnp.maximum(m_prev, jnp.max(s, axis=1, keepdims=True))
            alpha = jnp.exp(m_prev - m_new)
            p = jnp.exp(s - m_new)
            l_ref[hh] = alpha * l_ref[hh] + jnp.sum(p, axis=1, keepdims=True)
            acc_ref[hh] = alpha * acc_ref[hh] + _dot(p.astype(BF16), vblk)
            m_ref[hh] = m_new

    def reaches(j):
        base = (bi * nblk + jnp.maximum(j, 0)) * H_FOX + 2 * hp
        worst = None
        for hh in range(2):
            top = q_norms[hh] * knorm_max_ref[base + hh] + bias_max_ref[base + hh] - m_ref[hh]
            worst = top if worst is None else jnp.maximum(worst, top)
        return jnp.max(worst) >= -SKIP_LOG

    step(jd, True)

    def cond(c):
        j, go = c
        return jnp.logical_and(j >= 0, go)

    def body(c):
        j, _ = c
        step(j, False)
        return j - 1, reaches(j - 1)

    lax.while_loop(cond, body, (jd - 1, reaches(jd - 1)))
    out = jnp.where(lane < HEAD_DIM, acc_ref[0] / l_ref[0], acc_ref[1] / l_ref[1])
    o_ref[0] = out.astype(BF16)


def _fox(q, kt, vb, bias_max, knorm_max, tq, tk):
    b, t, _ = q.shape
    tkeys = kt.shape[1]
    past = tkeys - t
    nblk = tkeys // tk
    assert (tq == tk and past % tk == 0) or tk == tkeys
    smem = pl.BlockSpec(memory_space=pltpu.SMEM)
    per_head = lambda a: a[:, :, :H_FOX].reshape(-1)
    return pl.pallas_call(
        functools.partial(_fox_kernel, tq=tq, tk=tk, past=past, nblk=nblk),
        grid=(b, H_FOX // 2, t // tq),
        in_specs=[smem, smem,
                  pl.BlockSpec((1, tq, LANES), lambda bi, hp, i: (bi, i, hp)),
                  pl.BlockSpec((1, tkeys, 2 * LANES), lambda bi, hp, i: (bi, 0, hp)),
                  pl.BlockSpec((1, tkeys, LANES), lambda bi, hp, i: (bi, 0, hp))],
        out_specs=pl.BlockSpec((1, tq, LANES), lambda bi, hp, i: (bi, i, hp)),
        out_shape=jax.ShapeDtypeStruct((b, t, D_FOX), BF16),
        scratch_shapes=[pltpu.VMEM((2, tq, 1), F32), pltpu.VMEM((2, tq, 1), F32),
                        pltpu.VMEM((2, tq, LANES), F32)],
        compiler_params=pltpu.CompilerParams(dimension_semantics=("arbitrary",) * 3,
                                             vmem_limit_bytes=VMEM_LIMIT),
        name="fox",
    )(per_head(bias_max), per_head(knorm_max), q, kt, vb)


def _sb_kernel(q_ref, k_ref, v_ref, tri_ref, o_ref, carry_ref, acc_ref, *, tq, tk, past):
    i = pl.program_id(2)
    jd = _diag_block(i, tq, tk, past)
    lane, qa, qb = _head_masks(q_ref[0])
    row_pos = past + i * tq + lax.broadcasted_iota(jnp.int32, (tq, tk), 0)
    col = lax.broadcasted_iota(jnp.int32, (tq, tk), 1)

    carry_ref[...] = jnp.zeros_like(carry_ref)
    acc_ref[...] = jnp.zeros_like(acc_ref)

    def step(j, masked):
        start = pl.multiple_of(j * tk, tk)
        kblk = k_ref[0, pl.ds(start, tk), :]
        vblk = v_ref[0, pl.ds(start, tk), :]
        tri = tri_ref[...]
        for hh, qh in enumerate((qa, qb)):
            z = _dot_nt(qh, kblk)
            sp = _softplus(z)
            log1m = -sp
            if masked:
                valid = start + col < row_pos
                log1m = jnp.where(valid, log1m, 0.0)
            hi = log1m.astype(BF16)
            lo = (log1m - hi.astype(F32)).astype(BF16)
            suffix = _dot(hi, tri) + _dot(lo, tri)
            later = carry_ref[hh]
            a = jnp.exp(z - sp + suffix + later)
            if masked:
                a = jnp.where(valid, a, 0.0)
            acc_ref[hh] = acc_ref[hh] + _dot(a.astype(BF16), vblk)
            carry_ref[hh] = later + jnp.sum(log1m, axis=1, keepdims=True)

    def reaches():
        return jnp.max(jnp.maximum(carry_ref[0], carry_ref[1])) >= -SKIP_LOG

    step(jd, True)

    def cond(c):
        j, go = c
        return jnp.logical_and(j >= 0, go)

    def body(c):
        j, _ = c
        step(j, False)
        return j - 1, reaches()

    lax.while_loop(cond, body, (jd - 1, reaches()))
    o_ref[0] = jnp.where(lane < HEAD_DIM, acc_ref[0], acc_ref[1]).astype(BF16)


def _sb(q, k, v, tq, tk):
    b, t, _ = q.shape
    tkeys = k.shape[1]
    past = tkeys - t
    assert (tq == tk and past % tk == 0) or tk == tkeys
    r = jnp.arange(tk)
    tri = (r[:, None] > r[None, :]).astype(BF16)
    return pl.pallas_call(
        functools.partial(_sb_kernel, tq=tq, tk=tk, past=past),
        grid=(b, H_SB // 2, t // tq),
        in_specs=[pl.BlockSpec((1, tq, LANES), lambda bi, hp, i: (bi, i, hp)),
                  pl.BlockSpec((1, tkeys, LANES), lambda bi, hp, i: (bi, 0, hp)),
                  pl.BlockSpec((1, tkeys, LANES), lambda bi, hp, i: (bi, 0, hp)),
                  _const_spec((tk, tk))],
        out_specs=pl.BlockSpec((1, tq, LANES), lambda bi, hp, i: (bi, i, hp)),
        out_shape=jax.ShapeDtypeStruct((b, t, D_SB), BF16),
        scratch_shapes=[pltpu.VMEM((2, tq, 1), F32), pltpu.VMEM((2, tq, LANES), F32)],
        compiler_params=pltpu.CompilerParams(dimension_semantics=("arbitrary",) * 3,
                                             vmem_limit_bytes=VMEM_LIMIT),
        name="sb",
    )(q, k, v, tri)


def _pick_tile(n, pref):
    t = min(n, pref)
    while n % t:
        t //= 2
    return t


def _layer(x, past, w, final_norm, final):
    b, t, _ = x.shape
    past_k, past_v, past_lf, past_sk, past_sv, conv_buf = past
    p = past_k.shape[1]
    assert t >= CONV_W - 1 and t % HALO == 0
    n = b * t
    tm = _pick_tile(n, 256)

    (x1, qf, kf, vf, lfp, u, qs, ks, vs, ksb, vsb) = _dense_in(
        x.reshape(n, D_MODEL), w["n1"], w["wg1"], w["wu1"], w["wd1"], w["n2"], w["wp"], w["bfp"], tm)

    r3 = lambda a: a.reshape(b, t, a.shape[-1])
    kf3, vf3, lfp3, u3 = r3(kf), r3(vf), r3(lfp), r3(u)
    if p:
        kf_all = jnp.concatenate([past_k.reshape(b, p, D_FOX), kf3], axis=1)
        vf_all = jnp.concatenate([past_v.reshape(b, p, D_FOX), vf3], axis=1)
        lf_all = jnp.concatenate([jnp.pad(past_lf, ((0, 0), (0, 0), (0, LANES - H_FOX))), lfp3], axis=1)
        ks_all = jnp.concatenate([past_sk.reshape(b, p, D_SB).astype(BF16), r3(ksb)], axis=1)
        vs_all = jnp.concatenate([past_sv.reshape(b, p, D_SB).astype(BF16), r3(vsb)], axis=1)
        tq = t
        tk = p + t
    else:
        kf_all, vf_all, lf_all, ks_all, vs_all = kf3, vf3, lfp3, r3(ksb), r3(vsb)
        tq = tk = _pick_tile(t, 256)

    kt, vb, bias_max, knorm_max = _prep(kf_all, vf_all, lf_all, tk)
    yf = _fox(r3(qf), kt, vb, bias_max, knorm_max, tq, tk)
    ys = _sb(r3(qs), ks_all, vs_all, tq, tk)
    buf_pad = jnp.pad(conv_buf, ((0, 0), (HALO - (CONV_W - 1), 0), (0, 0)))
    yc = _conv(u3, buf_pad, w["conv_w"], w["conv_b"], w["ln_g"], w["ln_b"], _pick_tile(t, 256))

    flat = lambda a: a.reshape(n, a.shape[-1])
    xo = _dense_out(x1, flat(yf), flat(yc), flat(ys), w["wo"], w["n3"], w["wg2"], w["wu2"], w["wd2"],
                    final_norm, tm, final)
    new = (kf3.reshape(b, t, H_FOX, HEAD_DIM), vf3.reshape(b, t, H_FOX, HEAD_DIM), lfp3[..., :H_FOX],
           r3(ks).reshape(b, t, H_SB, HEAD_DIM), r3(vs).reshape(b, t, H_SB, HEAD_DIM),
           u3[:, t - (CONV_W - 1):, :])
    return xo.reshape(b, t, D_MODEL), new


def _layer_weights(l, ffn_norm, ffn_gate, ffn_up, ffn_down, mix_norm, w_in, b_forget,
                   conv_w, conv_b, conv_ln_g, conv_ln_b, w_out):
    wi = w_in[l]
    off_f = 3 * D_FOX
    off_glu = off_f + H_FOX
    off_qc = off_glu + 2 * C_CONV
    wp = jnp.concatenate([wi[:, :off_f], jnp.pad(wi[:, off_f:off_glu], ((0, 0), (0, LANES - H_FOX))),
                          wi[:, off_glu:off_qc], wi[:, off_qc:]], axis=1).astype(BF16)
    row = lambda a: a.reshape(1, -1).astype(F32)
    return dict(
        n1=row(ffn_norm[l, 0]), wg1=ffn_gate[l, 0].astype(BF16), wu1=ffn_up[l, 0].astype(BF16),
        wd1=ffn_down[l, 0].astype(BF16), n2=row(mix_norm[l]), wp=wp,
        bfp=jnp.pad(row(b_forget[l]), ((0, 0), (0, LANES - H_FOX))),
        conv_w=conv_w[l], conv_b=row(conv_b[l]), ln_g=row(conv_ln_g[l]), ln_b=row(conv_ln_b[l]),
        wo=w_out[l].astype(BF16), n3=row(ffn_norm[l, 1]), wg2=ffn_gate[l, 1].astype(BF16),
        wu2=ffn_up[l, 1].astype(BF16), wd2=ffn_down[l, 1].astype(BF16))


def kernel(x_prompt, x_sample, cache_fox_k, cache_fox_v, cache_fox_logf, cache_sb_k, cache_sb_v, state_conv,
           ffn_norm, ffn_gate, ffn_up, ffn_down, mix_norm, w_in, b_forget, conv_w, conv_b, conv_ln_g,
           conv_ln_b, w_out, final_norm):
    depth = w_in.shape[0]
    bp = x_prompt.shape[0]
    dt = x_prompt.dtype
    empty = (jnp.zeros((bp, 0, H_FOX, HEAD_DIM), dt), jnp.zeros((bp, 0, H_FOX, HEAD_DIM), dt),
             jnp.zeros((bp, 0, H_FOX), dt), jnp.zeros((bp, 0, H_SB, HEAD_DIM), dt),
             jnp.zeros((bp, 0, H_SB, HEAD_DIM), dt), jnp.zeros((bp, CONV_W - 1, C_CONV), dt))
    fn = final_norm.reshape(1, -1).astype(F32)
    xp, xs = x_prompt, x_sample
    new_p, new_s = [], []
    for l in range(depth):
        w = _layer_weights(l, ffn_norm, ffn_gate, ffn_up, ffn_down, mix_norm, w_in, b_forget,
                           conv_w, conv_b, conv_ln_g, conv_ln_b, w_out)
        final = l == depth - 1
        xp, np_ = _layer(xp, empty, w, fn, final)
        cache = (cache_fox_k[l], cache_fox_v[l], cache_fox_logf[l], cache_sb_k[l], cache_sb_v[l],
                 state_conv[l])
        xs, ns_ = _layer(xs, cache, w, fn, final)
        new_p.append(np_)
        new_s.append(ns_)
    stack = lambda items, k: jnp.stack([it[k] for it in items], 0)
    return (xp, xs) + tuple(stack(new_p, k) for k in range(6)) + tuple(stack(new_s, k) for k in range(6))
```

```python
import functools

import jax
import jax.numpy as jnp
from jax import lax
from jax.experimental import pallas as pl
from jax.experimental.pallas import tpu as pltpu

D_MODEL = 1024
HEAD_DIM = 64
D_FOX = 512
C_CONV = 256
D_SB = 256
H_FOX = 8
H_SB = 4
CONV_W = 31
D_FF = 2816
EPS = 1e-6

LANES = 128
HALO = 32
FF_CHUNK = D_FF // 2
N_SPLIT = 3
VMEM_LIMIT = 56 * 1024 * 1024
SKIP_LOG = 110.0
NORM_SLACK = 1.01

ZQ, ZK, ZV, ZF, ZA, ZG, ZQS, ZKS, ZVS, ZEND = 0, 512, 1024, 1536, 1664, 1920, 2176, 2432, 2688, 2944

F32 = jnp.float32
BF16 = jnp.bfloat16


def _dot(a, b):
    return jnp.dot(a, b, preferred_element_type=F32)


def _sigmoid(x):
    return 1.0 / (1.0 + jnp.exp(-x))


def _softplus(x):
    return jnp.maximum(x, 0.0) + jnp.log1p(jnp.exp(-jnp.abs(x)))


def _rms(x, g):
    return x * lax.rsqrt(jnp.mean(x * x, axis=-1, keepdims=True) + EPS) * g


def _ffn_half(x, g_ref, wg_ref, wu_ref, wd_ref):
    hn = _rms(x, g_ref[...]).astype(BF16)
    y = jnp.zeros_like(x)
    for c in range(D_FF // FF_CHUNK):
        sl = slice(c * FF_CHUNK, (c + 1) * FF_CHUNK)
        g = _dot(hn, wg_ref[:, sl])
        u = _dot(hn, wu_ref[:, sl])
        act = (g * _sigmoid(g) * u).astype(BF16)
        y = y + _dot(act, wd_ref[sl, :])
    return x + 0.5 * y


def _split3(x):
    a = x.astype(BF16)
    r = x - a.astype(F32)
    b = r.astype(BF16)
    c = (r - b.astype(F32)).astype(BF16)
    return a, b, c


def _dense_in_kernel(x_ref, n1_ref, wg_ref, wu_ref, wd_ref, n2_ref, wp_ref, bf_ref,
                     x1_ref, qf_ref, kf_ref, vf_ref, lfp_ref, u_ref, qs_ref, ks_ref, vs_ref, ksb_ref):
    x1 = _ffn_half(x_ref[...], n1_ref, wg_ref, wu_ref, wd_ref)
    x1_ref[...] = x1
    h = _rms(x1, n2_ref[...]).astype(BF16)
    z = _dot(h, wp_ref[...])
    scale = HEAD_DIM ** -0.5
    qf_ref[...] = (z[:, ZQ:ZK] * scale).astype(BF16)
    kf_ref[...] = z[:, ZK:ZV]
    vf_ref[...] = z[:, ZV:ZF]
    zf = z[:, ZF:ZA] + bf_ref[...]
    lane = lax.broadcasted_iota(jnp.int32, zf.shape, 1)
    lfp_ref[...] = jnp.where(lane < H_FOX, -_softplus(-zf), 0.0)
    u_ref[...] = z[:, ZA:ZG] * _sigmoid(z[:, ZG:ZQS])
    qs_ref[...] = (z[:, ZQS:ZKS] * scale).astype(BF16)
    ks = z[:, ZKS:ZVS]
    ks_ref[...] = ks
    ksb_ref[...] = ks.astype(BF16)
    vs_ref[...] = z[:, ZVS:ZEND]


def _const_spec(shape):
    return pl.BlockSpec(shape, lambda *_: (0,) * len(shape), pipeline_mode=pl.Buffered(1))


def _dense_in(x, n1, wg, wu, wd, n2, wp, bfp, tm):
    n = x.shape[0]
    row = lambda w: pl.BlockSpec((tm, w), lambda i: (i, 0))
    widths = (D_MODEL, D_FOX, D_FOX, D_FOX, LANES, C_CONV, D_SB, D_SB, D_SB, D_SB)
    dtypes = (F32, BF16, F32, F32, F32, F32, BF16, F32, F32, BF16)
    return pl.pallas_call(
        _dense_in_kernel,
        grid=(n // tm,),
        in_specs=[row(D_MODEL), _const_spec((1, D_MODEL)), _const_spec((D_MODEL, D_FF)),
                  _const_spec((D_MODEL, D_FF)), _const_spec((D_FF, D_MODEL)), _const_spec((1, D_MODEL)),
                  _const_spec((D_MODEL, ZEND)), _const_spec((1, LANES))],
        out_specs=[row(w) for w in widths],
        out_shape=[jax.ShapeDtypeStruct((n, w), d) for w, d in zip(widths, dtypes)],
        compiler_params=pltpu.CompilerParams(dimension_semantics=("arbitrary",),
                                             vmem_limit_bytes=VMEM_LIMIT),
        name="dense_in",
    )(x, n1, wg, wu, wd, n2, wp, bfp)


def _dense_out_kernel(x1_ref, yf_ref, yc_ref, ys_ref, wo_ref, n_ref, wg_ref, wu_ref, wd_ref, fn_ref,
                      o_ref, *, final):
    x2 = (x1_ref[...] + _dot(yf_ref[...], wo_ref[0:D_FOX, :])
          + _dot(yc_ref[...], wo_ref[D_FOX:D_FOX + C_CONV, :])
          + _dot(ys_ref[...], wo_ref[D_FOX + C_CONV:, :]))
    x3 = _ffn_half(x2, n_ref, wg_ref, wu_ref, wd_ref)
    o_ref[...] = _rms(x3, fn_ref[...]) if final else x3


def _dense_out(x1, yf, yc, ys, wo, n, wg, wu, wd, fn, tm, final):
    nrow = x1.shape[0]
    row = lambda w: pl.BlockSpec((tm, w), lambda i: (i, 0))
    return pl.pallas_call(
        functools.partial(_dense_out_kernel, final=final),
        grid=(nrow // tm,),
        in_specs=[row(D_MODEL), row(D_FOX), row(C_CONV), row(D_SB), _const_spec((D_MODEL, D_MODEL)),
                  _const_spec((1, D_MODEL)), _const_spec((D_MODEL, D_FF)), _const_spec((D_MODEL, D_FF)),
                  _const_spec((D_FF, D_MODEL)), _const_spec((1, D_MODEL))],
        out_specs=row(D_MODEL),
        out_shape=jax.ShapeDtypeStruct((nrow, D_MODEL), F32),
        compiler_params=pltpu.CompilerParams(dimension_semantics=("arbitrary",),
                                             vmem_limit_bytes=VMEM_LIMIT),
        name="dense_out",
    )(x1, yf, yc, ys, wo, n, wg, wu, wd, fn)


def _prep_kernel(kf_ref, vf_ref, vs_ref, lfp_ref, tri_ref, sel_ref, hsel_ref,
                 kt_ref, vft_ref, vst_ref, bias_max_ref, knorm_max_ref, carry_ref, bmax_ref, kmax_ref):
    i = pl.program_id(1)

    @pl.when(i == 0)
    def _():
        carry_ref[...] = jnp.zeros_like(carry_ref)
        bmax_ref[...] = jnp.full_like(bmax_ref, -jnp.inf)
        kmax_ref[...] = jnp.zeros_like(kmax_ref)

    tri = tri_ref[...]
    cum = carry_ref[...]
    for part in _split3(lfp_ref[0]):
        cum = cum + _dot(tri, part)
    carry_ref[...] = cum[cum.shape[0] - 1:, :]
    bias = -cum
    aug = None
    for s, part in enumerate(_split3(bias)):
        term = _dot(part, sel_ref[s])
        aug = term if aug is None else aug + term
    kb = kf_ref[0].astype(BF16)
    for hp in range(H_FOX // 2):
        base = 2 * LANES * hp
        kt_ref[0, :, base:base + LANES] = kb[:, hp * LANES:(hp + 1) * LANES]
        kt_ref[0, :, base + LANES:base + 2 * LANES] = aug[:, hp * LANES:(hp + 1) * LANES].astype(BF16)
    vft_ref[0] = jnp.transpose(vf_ref[0]).astype(BF16)
    vst_ref[0] = jnp.transpose(vs_ref[0]).astype(BF16)

    ksq = kb.astype(F32)
    ksq = ksq * ksq
    hi = ksq.astype(BF16)
    lo = (ksq - hi.astype(F32)).astype(BF16)
    norm2 = _dot(hi, hsel_ref[...]) + _dot(lo, hsel_ref[...])
    kmax = jnp.maximum(kmax_ref[...], jnp.max(norm2, axis=0, keepdims=True))
    bmax = jnp.maximum(bmax_ref[...], jnp.max(bias, axis=0, keepdims=True))
    kmax_ref[...] = kmax
    bmax_ref[...] = bmax
    knorm_max_ref[0, pl.ds(i, 1), :] = jnp.sqrt(kmax) * NORM_SLACK
    bias_max_ref[0, pl.ds(i, 1), :] = bmax


def _prep(kf, vf, vs, lfp, tp):
    b, tk, _ = kf.shape
    nblk = tk // tp
    r = jnp.arange(tp)
    tri = (r[None, :] <= r[:, None]).astype(BF16)
    h = jnp.arange(LANES)
    col = jnp.arange(D_FOX)
    sel = jnp.stack([(col[None, :] == ((h // 2) * LANES + (h % 2) * N_SPLIT + s)[:, None])
                     & (h[:, None] < H_FOX) for s in range(N_SPLIT)]).astype(BF16)
    hsel = (col[:, None] // HEAD_DIM == h[None, :]).astype(BF16)
    blk = lambda w: pl.BlockSpec((1, tp, w), lambda bi, i: (bi, i, 0))
    blk_t = lambda w: pl.BlockSpec((1, w, tp), lambda bi, i: (bi, 0, i))
    per_batch = pl.BlockSpec((1, nblk, LANES), lambda bi, i: (bi, 0, 0))
    return pl.pallas_call(
        _prep_kernel,
        grid=(b, nblk),
        in_specs=[blk(D_FOX), blk(D_FOX), blk(D_SB), blk(LANES), _const_spec((tp, tp)),
                  _const_spec((N_SPLIT, LANES, D_FOX)), _const_spec((D_FOX, LANES))],
        out_specs=[blk(2 * D_FOX), blk_t(D_FOX), blk_t(D_SB), per_batch, per_batch],
        out_shape=[jax.ShapeDtypeStruct((b, tk, 2 * D_FOX), BF16),
                   jax.ShapeDtypeStruct((b, D_FOX, tk), BF16),
                   jax.ShapeDtypeStruct((b, D_SB, tk), BF16),
                   jax.ShapeDtypeStruct((b, nblk, LANES), F32),
                   jax.ShapeDtypeStruct((b, nblk, LANES), F32)],
        scratch_shapes=[pltpu.VMEM((1, LANES), F32)] * 3,
        compiler_params=pltpu.CompilerParams(dimension_semantics=("arbitrary", "arbitrary"),
                                             vmem_limit_bytes=VMEM_LIMIT),
        name="prep",
    )(kf, vf, vs, lfp, tri, sel, hsel)


def _conv_kernel(u_ref, prev_ref, buf_ref, w_ref, b_ref, g_ref, beta_ref, y_ref, xw_ref, *, tt, rows):
    first = pl.program_id(1) == 0
    xw_ref[0:HALO, :] = jnp.where(first, buf_ref[0], prev_ref[0])
    xw_ref[HALO:HALO + tt, :] = u_ref[0]
    off = HALO - (CONV_W - 1)
    for r0 in range(0, tt, rows):
        acc = jnp.zeros((rows, C_CONV), F32)
        for j in range(CONV_W):
            acc = acc + w_ref[j:j + 1, :] * xw_ref[r0 + j + off:r0 + j + off + rows, :]
        y = acc + b_ref[...]
        mu = jnp.mean(y, axis=-1, keepdims=True)
        yc = y - mu
        var = jnp.mean(yc * yc, axis=-1, keepdims=True)
        y = yc * lax.rsqrt(var + EPS) * g_ref[...] + beta_ref[...]
        y_ref[0, r0:r0 + rows, :] = (y * _sigmoid(y)).astype(BF16)


def _conv(u, buf_pad, w, b, g, beta, tt):
    bsz, t, _ = u.shape
    rows = min(tt, 64)
    per = tt // HALO
    return pl.pallas_call(
        functools.partial(_conv_kernel, tt=tt, rows=rows),
        grid=(bsz, t // tt),
        in_specs=[pl.BlockSpec((1, tt, C_CONV), lambda bi, i: (bi, i, 0)),
                  pl.BlockSpec((1, HALO, C_CONV), lambda bi, i: (bi, jnp.maximum(i * per - 1, 0), 0)),
                  pl.BlockSpec((1, HALO, C_CONV), lambda bi, i: (bi, 0, 0)),
                  _const_spec((CONV_W, C_CONV)), _const_spec((1, C_CONV)), _const_spec((1, C_CONV)),
                  _const_spec((1, C_CONV))],
        out_specs=pl.BlockSpec((1, tt, C_CONV), lambda bi, i: (bi, i, 0)),
        out_shape=jax.ShapeDtypeStruct((bsz, t, C_CONV), BF16),
        scratch_shapes=[pltpu.VMEM((HALO + tt, C_CONV), F32)],
        compiler_params=pltpu.CompilerParams(dimension_semantics=("arbitrary", "arbitrary")),
        name="conv",
    )(u, u, buf_pad, w, b, g, beta)


def _query_heads(q_ref):
    qt = jnp.transpose(q_ref[0].astype(F32))
    row = lax.broadcasted_iota(jnp.int32, qt.shape, 0)
    return row, jnp.where(row < HEAD_DIM, qt, 0.0), jnp.where(row >= HEAD_DIM, qt, 0.0)


def _diag_block(i, tq, tk, past):
    return (past + i * tq + tq - 1) // tk


def _positions(i, tq, tk, past):
    key_pos = lax.broadcasted_iota(jnp.int32, (tk, tq), 0)
    q_pos = past + i * tq + lax.broadcasted_iota(jnp.int32, (tk, tq), 1)
    return key_pos, q_pos


def _merge_heads(out_a, out_b):
    return jnp.transpose(jnp.concatenate([out_a, out_b], axis=0))


def _descend(jd, step, reaches):
    step(jd, True)

    def cond(c):
        j, go = c
        return jnp.logical_and(j >= 0, go)

    def body(c):
        j, _ = c
        step(j, False)
        return j - 1, reaches(j - 1)

    lax.while_loop(cond, body, (jd - 1, reaches(jd - 1)))


def _fox_kernel(bias_max_ref, knorm_max_ref, q_ref, k_ref, vt_ref, o_ref, m_ref, l_ref, acc_ref,
                *, tq, tk, past, nblk):
    bi = pl.program_id(0)
    hp = pl.program_id(1)
    i = pl.program_id(2)
    row, qa, qb = _query_heads(q_ref)
    ones_a = jnp.where(row < N_SPLIT, 1.0, 0.0)
    ones_b = jnp.where(row < 2 * N_SPLIT, 1.0, 0.0) - ones_a
    q_heads = (jnp.concatenate([qa, ones_a], axis=0).astype(BF16),
               jnp.concatenate([qb, ones_b], axis=0).astype(BF16))
    q_norms = [jnp.sqrt(jnp.sum(qh * qh, axis=0, keepdims=True)) * NORM_SLACK for qh in (qa, qb)]
    key_pos, q_pos = _positions(i, tq, tk, past)

    m_ref[...] = jnp.full_like(m_ref, -jnp.inf)
    l_ref[...] = jnp.zeros_like(l_ref)
    acc_ref[...] = jnp.zeros_like(acc_ref)

    def step(j, masked):
        start = pl.multiple_of(j * tk, tk)
        kblk = k_ref[0, pl.ds(start, tk), :]
        vt = vt_ref[0, :, pl.ds(start, tk)]
        scores = [_dot(kblk, q_heads[hh]) for hh in range(2)]
        for hh in range(2):
            s = scores[hh]
            if masked:
                s = jnp.where(start + key_pos <= q_pos, s, -jnp.inf)
            m_prev = m_ref[hh]
            m_new = jnp.maximum(m_prev, jnp.max(s, axis=0, keepdims=True))
            alpha = jnp.exp(m_prev - m_new)
            p = jnp.exp(s - m_new)
            l_ref[hh] = alpha * l_ref[hh] + jnp.sum(p, axis=0, keepdims=True)
            pv = _dot(vt[hh * HEAD_DIM:(hh + 1) * HEAD_DIM, :], p.astype(BF16))
            acc_ref[hh] = alpha * acc_ref[hh] + pv
            m_ref[hh] = m_new

    def reaches(j):
        base = (bi * nblk + jnp.maximum(j, 0)) * H_FOX + 2 * hp
        worst = None
        for hh in range(2):
            top = q_norms[hh] * knorm_max_ref[base + hh] + bias_max_ref[base + hh] - m_ref[hh]
            worst = top if worst is None else jnp.maximum(worst, top)
        return jnp.max(worst) >= -SKIP_LOG

    _descend(_diag_block(i, tq, tk, past), step, reaches)
    o_ref[0] = _merge_heads(acc_ref[0] / l_ref[0], acc_ref[1] / l_ref[1]).astype(BF16)


def _fox(q, kt, vt, bias_max, knorm_max, past, tq, tk):
    b, t, _ = q.shape
    tkeys = kt.shape[1]
    nblk = tkeys // tk
    assert (tk % tq == 0 and past % tk == 0) or tk == tkeys
    smem = pl.BlockSpec(memory_space=pltpu.SMEM)
    per_head = lambda a: a[:, :, :H_FOX].reshape(-1)
    return pl.pallas_call(
        functools.partial(_fox_kernel, tq=tq, tk=tk, past=past, nblk=nblk),
        grid=(b, H_FOX // 2, t // tq),
        in_specs=[smem, smem,
                  pl.BlockSpec((1, tq, LANES), lambda bi, hp, i: (bi, i, hp)),
                  pl.BlockSpec((1, tkeys, 2 * LANES), lambda bi, hp, i: (bi, 0, hp)),
                  pl.BlockSpec((1, LANES, tkeys), lambda bi, hp, i: (bi, hp, 0))],
        out_specs=pl.BlockSpec((1, tq, LANES), lambda bi, hp, i: (bi, i, hp)),
        out_shape=jax.ShapeDtypeStruct((b, t, D_FOX), BF16),
        scratch_shapes=[pltpu.VMEM((2, 1, tq), F32), pltpu.VMEM((2, 1, tq), F32),
                        pltpu.VMEM((2, HEAD_DIM, tq), F32)],
        compiler_params=pltpu.CompilerParams(dimension_semantics=("arbitrary",) * 3,
                                             vmem_limit_bytes=VMEM_LIMIT),
        name="fox",
    )(per_head(bias_max), per_head(knorm_max), q, kt, vt)


def _sb_kernel(q_ref, k_ref, vt_ref, tri_ref, o_ref, carry_ref, acc_ref, *, tq, tk, past):
    i = pl.program_id(2)
    _, qa, qb = _query_heads(q_ref)
    q_heads = (qa.astype(BF16), qb.astype(BF16))
    key_pos, q_pos = _positions(i, tq, tk, past)

    carry_ref[...] = jnp.zeros_like(carry_ref)
    acc_ref[...] = jnp.zeros_like(acc_ref)

    def step(j, masked):
        start = pl.multiple_of(j * tk, tk)
        kblk = k_ref[0, pl.ds(start, tk), :]
        vt = vt_ref[0, :, pl.ds(start, tk)]
        tri = tri_ref[...]
        scores = [_dot(kblk, q_heads[hh]) for hh in range(2)]
        for hh in range(2):
            z = scores[hh]
            sp = _softplus(z)
            log1m = -sp
            if masked:
                valid = start + key_pos < q_pos
                log1m = jnp.where(valid, log1m, 0.0)
            hi = log1m.astype(BF16)
            lo = (log1m - hi.astype(F32)).astype(BF16)
            suffix = _dot(tri, hi) + _dot(tri, lo)
            later = carry_ref[hh]
            a = jnp.exp(z - sp + suffix + later)
            if masked:
                a = jnp.where(valid, a, 0.0)
            acc_ref[hh] = acc_ref[hh] + _dot(vt[hh * HEAD_DIM:(hh + 1) * HEAD_DIM, :], a.astype(BF16))
            carry_ref[hh] = later + jnp.sum(log1m, axis=0, keepdims=True)

    def reaches(j):
        return jnp.max(jnp.maximum(carry_ref[0], carry_ref[1])) >= -SKIP_LOG

    _descend(_diag_block(i, tq, tk, past), step, reaches)
    o_ref[0] = _merge_heads(acc_ref[0], acc_ref[1]).astype(BF16)


def _sb(q, k, vt, past, tq, tk):
    b, t, _ = q.shape
    tkeys = k.shape[1]
    assert (tk % tq == 0 and past % tk == 0) or tk == tkeys
    r = jnp.arange(tk)
    tri = (r[None, :] > r[:, None]).astype(BF16)
    return pl.pallas_call(
        functools.partial(_sb_kernel, tq=tq, tk=tk, past=past),
        grid=(b, H_SB // 2, t // tq),
        in_specs=[pl.BlockSpec((1, tq, LANES), lambda bi, hp, i: (bi, i, hp)),
                  pl.BlockSpec((1, tkeys, LANES), lambda bi, hp, i: (bi, 0, hp)),
                  pl.BlockSpec((1, LANES, tkeys), lambda bi, hp, i: (bi, hp, 0)),
                  _const_spec((tk, tk))],
        out_specs=pl.BlockSpec((1, tq, LANES), lambda bi, hp, i: (bi, i, hp)),
        out_shape=jax.ShapeDtypeStruct((b, t, D_SB), BF16),
        scratch_shapes=[pltpu.VMEM((2, 1, tq), F32), pltpu.VMEM((2, HEAD_DIM, tq), F32)],
        compiler_params=pltpu.CompilerParams(dimension_semantics=("arbitrary",) * 3,
                                             vmem_limit_bytes=VMEM_LIMIT),
        name="sb",
    )(q, k, vt, tri)


def _pick_tile(n, pref):
    t = min(n, pref)
    while n % t:
        t //= 2
    return t


def _layer(x, past, w, final_norm, final):
    b, t, _ = x.shape
    past_k, past_v, past_lf, past_sk, past_sv, conv_buf = past
    p = past_k.shape[1]
    assert t >= CONV_W - 1 and t % HALO == 0
    n = b * t
    tm = _pick_tile(n, 256)

    (x1, qf, kf, vf, lfp, u, qs, ks, vs, ksb) = _dense_in(
        x.reshape(n, D_MODEL), w["n1"], w["wg1"], w["wu1"], w["wd1"], w["n2"], w["wp"], w["bfp"], tm)

    r3 = lambda a: a.reshape(b, t, a.shape[-1])
    kf3, vf3, lfp3, u3, vs3 = r3(kf), r3(vf), r3(lfp), r3(u), r3(vs)
    if p:
        fill = -(p + t) % LANES
        cat = lambda old, new: jnp.concatenate(
            [old.reshape(b, p, -1).astype(new.dtype), new,
             jnp.zeros((b, fill, new.shape[-1]), new.dtype)], axis=1)
        kf_all, vf_all, vs_all = cat(past_k, kf3), cat(past_v, vf3), cat(past_sv, vs3)
        lf_all = cat(jnp.pad(past_lf, ((0, 0), (0, 0), (0, LANES - H_FOX))), lfp3)
        ks_all = cat(past_sk, r3(ksb))
        tq = t
        tk_fox = tk_sb = p + t + fill
    else:
        kf_all, vf_all, vs_all, lf_all, ks_all = kf3, vf3, vs3, lfp3, r3(ksb)
        tq = tk_sb = _pick_tile(t, 256)
        tk_fox = _pick_tile(t, 512)

    kt, vft, vst, bias_max, knorm_max = _prep(kf_all, vf_all, vs_all, lf_all, tk_fox)
    yf = _fox(r3(qf), kt, vft, bias_max, knorm_max, p, tq, tk_fox)
    ys = _sb(r3(qs), ks_all, vst, p, tq, tk_sb)
    buf_pad = jnp.pad(conv_buf, ((0, 0), (HALO - (CONV_W - 1), 0), (0, 0)))
    yc = _conv(u3, buf_pad, w["conv_w"], w["conv_b"], w["ln_g"], w["ln_b"], _pick_tile(t, 256))

    flat = lambda a: a.reshape(n, a.shape[-1])
    xo = _dense_out(x1, flat(yf), flat(yc), flat(ys), w["wo"], w["n3"], w["wg2"], w["wu2"], w["wd2"],
                    final_norm, tm, final)
    new = (kf3.reshape(b, t, H_FOX, HEAD_DIM), vf3.reshape(b, t, H_FOX, HEAD_DIM), lfp3[..., :H_FOX],
           r3(ks).reshape(b, t, H_SB, HEAD_DIM), vs3.reshape(b, t, H_SB, HEAD_DIM),
           u3[:, t - (CONV_W - 1):, :])
    return xo.reshape(b, t, D_MODEL), new


def _layer_weights(l, ffn_norm, ffn_gate, ffn_up, ffn_down, mix_norm, w_in, b_forget,
                   conv_w, conv_b, conv_ln_g, conv_ln_b, w_out):
    wi = w_in[l]
    off_f = 3 * D_FOX
    off_glu = off_f + H_FOX
    off_qc = off_glu + 2 * C_CONV
    wp = jnp.concatenate([wi[:, :off_f], jnp.pad(wi[:, off_f:off_glu], ((0, 0), (0, LANES - H_FOX))),
                          wi[:, off_glu:off_qc], wi[:, off_qc:]], axis=1).astype(BF16)
    row = lambda a: a.reshape(1, -1).astype(F32)
    return dict(
        n1=row(ffn_norm[l, 0]), wg1=ffn_gate[l, 0].astype(BF16), wu1=ffn_up[l, 0].astype(BF16),
        wd1=ffn_down[l, 0].astype(BF16), n2=row(mix_norm[l]), wp=wp,
        bfp=jnp.pad(row(b_forget[l]), ((0, 0), (0, LANES - H_FOX))),
        conv_w=conv_w[l], conv_b=row(conv_b[l]), ln_g=row(conv_ln_g[l]), ln_b=row(conv_ln_b[l]),
        wo=w_out[l].astype(BF16), n3=row(ffn_norm[l, 1]), wg2=ffn_gate[l, 1].astype(BF16),
        wu2=ffn_up[l, 1].astype(BF16), wd2=ffn_down[l, 1].astype(BF16))


def kernel(x_prompt, x_sample, cache_fox_k, cache_fox_v, cache_fox_logf, cache_sb_k, cache_sb_v, state_conv,
           ffn_norm, ffn_gate, ffn_up, ffn_down, mix_norm, w_in, b_forget, conv_w, conv_b, conv_ln_g,
           conv_ln_b, w_out, final_norm):
    depth = w_in.shape[0]
    bp = x_prompt.shape[0]
    dt = x_prompt.dtype
    empty = (jnp.zeros((bp, 0, H_FOX, HEAD_DIM), dt), jnp.zeros((bp, 0, H_FOX, HEAD_DIM), dt),
             jnp.zeros((bp, 0, H_FOX), dt), jnp.zeros((bp, 0, H_SB, HEAD_DIM), dt),
             jnp.zeros((bp, 0, H_SB, HEAD_DIM), dt), jnp.zeros((bp, CONV_W - 1, C_CONV), dt))
    fn = final_norm.reshape(1, -1).astype(F32)
    xp, xs = x_prompt, x_sample
    new_p, new_s = [], []
    for l in range(depth):
        w = _layer_weights(l, ffn_norm, ffn_gate, ffn_up, ffn_down, mix_norm, w_in, b_forget,
                           conv_w, conv_b, conv_ln_g, conv_ln_b, w_out)
        final = l == depth - 1
        xp, np_ = _layer(xp, empty, w, fn, final)
        cache = (cache_fox_k[l], cache_fox_v[l], cache_fox_logf[l], cache_sb_k[l], cache_sb_v[l],
                 state_conv[l])
        xs, ns_ = _layer(xs, cache, w, fn, final)
        new_p.append(np_)
        new_s.append(ns_)
    stack = lambda items, k: jnp.stack([it[k] for it in items], 0)
    return (xp, xs) + tuple(stack(new_p, k) for k in range(6)) + tuple(stack(new_s, k) for k in range(6))
```

```python
import functools

import jax
import jax.numpy as jnp
from jax import lax
from jax.experimental import pallas as pl
from jax.experimental.pallas import tpu as pltpu

D_MODEL = 1024
HEAD_DIM = 64
D_FOX = 512
C_CONV = 256
D_SB = 256
H_FOX = 8
H_SB = 4
CONV_W = 31
D_FF = 2816
EPS = 1e-6

LANES = 128
HALO = 32
N_SPLIT = 3
VMEM_LIMIT = 56 * 1024 * 1024
SKIP_LOG = 110.0
LOG2E = 1.4426950408889634
SKIP_LOG2 = SKIP_LOG * LOG2E
NORM_SLACK = 1.01

ZQ, ZK, ZV, ZF, ZA, ZG, ZQS, ZKS, ZVS, ZEND = 0, 512, 1024, 1536, 1664, 1920, 2176, 2432, 2688, 2944

F32 = jnp.float32
BF16 = jnp.bfloat16


def _dot(a, b):
    return jnp.dot(a, b, preferred_element_type=F32)


def _sigmoid(x):
    return 1.0 / (1.0 + jnp.exp(-x))


def _softplus(x):
    return jnp.maximum(x, 0.0) + jnp.log1p(jnp.exp(-jnp.abs(x)))


def _rms(x, g):
    return x * lax.rsqrt(jnp.mean(x * x, axis=-1, keepdims=True) + EPS) * g


def _ffn_half(x, g_ref, wg_ref, wu_ref, wd_ref):
    hn = _rms(x, g_ref[...]).astype(BF16)
    g = _dot(hn, wg_ref[...])
    u = _dot(hn, wu_ref[...])
    act = (g * _sigmoid(g) * u).astype(BF16)
    return x + 0.5 * _dot(act, wd_ref[...])


def _split3(x):
    a = x.astype(BF16)
    r = x - a.astype(F32)
    b = r.astype(BF16)
    c = (r - b.astype(F32)).astype(BF16)
    return a, b, c


def _dense_in_kernel(x_ref, n1_ref, wg_ref, wu_ref, wd_ref, n2_ref, wp_ref, bf_ref, *refs):
    x1_ref, qf_ref, kf_ref, vf_ref, lfp_ref, u_ref, qs_ref, ks_ref, vs_ref, ksb_ref = refs[-10:]
    x1 = _ffn_half(x_ref[...], n1_ref, wg_ref, wu_ref, wd_ref)
    x1_ref[...] = x1
    h = _rms(x1, n2_ref[...]).astype(BF16)
    z = _dot(h, wp_ref[...])
    scale = HEAD_DIM ** -0.5 * LOG2E
    qf_ref[...] = (z[:, ZQ:ZK] * scale).astype(BF16)
    kf_ref[...] = z[:, ZK:ZV]
    vf_ref[...] = z[:, ZV:ZF]
    zf = z[:, ZF:ZA] + bf_ref[...]
    lane = lax.broadcasted_iota(jnp.int32, zf.shape, 1)
    lfp_ref[...] = jnp.where(lane < H_FOX, -_softplus(-zf), 0.0)
    u_ref[...] = z[:, ZA:ZG] * _sigmoid(z[:, ZG:ZQS])
    qs_ref[...] = (z[:, ZQS:ZKS] * scale).astype(BF16)
    ks = z[:, ZKS:ZVS]
    ks_ref[...] = ks
    ksb_ref[...] = ks.astype(BF16)
    vs_ref[...] = z[:, ZVS:ZEND]


def _const_spec(shape):
    return pl.BlockSpec(shape, lambda *_: (0,) * len(shape), pipeline_mode=pl.Buffered(1))


STACKED = (2, 3, 7, 8)


def _dense_in(x, n1, wg, wu, wd, n2, wp, bfp, tm, layer, depth, stacked):
    n = x.shape[0]
    nt = n // tm
    row = lambda w: pl.BlockSpec((tm, w), lambda i: (i, 0))
    layer_row = lambda w: pl.BlockSpec((tm, w), lambda i: (i + layer * nt, 0))
    widths = (D_MODEL, D_FOX, D_FOX, D_FOX, LANES, C_CONV, D_SB, D_SB, D_SB, D_SB)
    dtypes = (F32, BF16, F32, F32, F32, F32, BF16, F32, F32, BF16)
    rows = [depth * n if k in STACKED else n for k in range(len(widths))]
    prev = () if stacked is None else tuple(stacked)
    n_fixed = 8
    return pl.pallas_call(
        _dense_in_kernel,
        grid=(nt,),
        in_specs=[row(D_MODEL), _const_spec((1, D_MODEL)), _const_spec((D_MODEL, D_FF)),
                  _const_spec((D_MODEL, D_FF)), _const_spec((D_FF, D_MODEL)), _const_spec((1, D_MODEL)),
                  _const_spec((D_MODEL, ZEND)), _const_spec((1, LANES))]
                 + [pl.BlockSpec(memory_space=pl.ANY)] * len(prev),
        out_specs=[layer_row(w) if k in STACKED else row(w) for k, w in enumerate(widths)],
        out_shape=[jax.ShapeDtypeStruct((r, w), d) for r, w, d in zip(rows, widths, dtypes)],
        input_output_aliases={n_fixed + a: k for a, k in enumerate(STACKED)} if prev else {},
        compiler_params=pltpu.CompilerParams(dimension_semantics=("arbitrary",),
                                             vmem_limit_bytes=VMEM_LIMIT),
        name="dense_in",
    )(x, n1, wg, wu, wd, n2, wp, bfp, *prev)


def _dense_out_kernel(x1_ref, yf_ref, yc_ref, ys_ref, wo_ref, n_ref, wg_ref, wu_ref, wd_ref, fn_ref,
                      o_ref, *, final):
    x2 = (x1_ref[...] + _dot(yf_ref[...], wo_ref[0:D_FOX, :])
          + _dot(yc_ref[...], wo_ref[D_FOX:D_FOX + C_CONV, :])
          + _dot(ys_ref[...], wo_ref[D_FOX + C_CONV:, :]))
    x3 = _ffn_half(x2, n_ref, wg_ref, wu_ref, wd_ref)
    o_ref[...] = _rms(x3, fn_ref[...]) if final else x3


def _dense_out(x1, yf, yc, ys, wo, n, wg, wu, wd, fn, tm, final):
    nrow = x1.shape[0]
    row = lambda w: pl.BlockSpec((tm, w), lambda i: (i, 0))
    return pl.pallas_call(
        functools.partial(_dense_out_kernel, final=final),
        grid=(nrow // tm,),
        in_specs=[row(D_MODEL), row(D_FOX), row(C_CONV), row(D_SB), _const_spec((D_MODEL, D_MODEL)),
                  _const_spec((1, D_MODEL)), _const_spec((D_MODEL, D_FF)), _const_spec((D_MODEL, D_FF)),
                  _const_spec((D_FF, D_MODEL)), _const_spec((1, D_MODEL))],
        out_specs=row(D_MODEL),
        out_shape=jax.ShapeDtypeStruct((nrow, D_MODEL), F32),
        compiler_params=pltpu.CompilerParams(dimension_semantics=("arbitrary",),
                                             vmem_limit_bytes=VMEM_LIMIT),
        name="dense_out",
    )(x1, yf, yc, ys, wo, n, wg, wu, wd, fn)


def _prep_kernel(kf_ref, vf_ref, vs_ref, lfp_ref, tri_ref, sel_ref, hsel_ref,
                 kt_ref, vft_ref, vst_ref, bias_max_ref, knorm_max_ref, carry_ref, bmax_ref, kmax_ref):
    i = pl.program_id(1)

    @pl.when(i == 0)
    def _():
        carry_ref[...] = jnp.zeros_like(carry_ref)
        bmax_ref[...] = jnp.full_like(bmax_ref, -jnp.inf)
        kmax_ref[...] = jnp.zeros_like(kmax_ref)

    tri = tri_ref[...]
    cum = carry_ref[...]
    for part in _split3(lfp_ref[0]):
        cum = cum + _dot(tri, part)
    carry_ref[...] = cum[cum.shape[0] - 1:, :]
    bias = -cum * LOG2E
    aug = None
    for s, part in enumerate(_split3(bias)):
        term = _dot(part, sel_ref[s])
        aug = term if aug is None else aug + term
    kb = kf_ref[0].astype(BF16)
    for hp in range(H_FOX // 2):
        base = 2 * LANES * hp
        kt_ref[0, :, base:base + LANES] = kb[:, hp * LANES:(hp + 1) * LANES]
        kt_ref[0, :, base + LANES:base + 2 * LANES] = aug[:, hp * LANES:(hp + 1) * LANES].astype(BF16)
    vft_ref[0] = jnp.transpose(vf_ref[0]).astype(BF16)
    vst_ref[0] = jnp.transpose(vs_ref[0]).astype(BF16)

    ksq = kb.astype(F32)
    ksq = ksq * ksq
    hi = ksq.astype(BF16)
    lo = (ksq - hi.astype(F32)).astype(BF16)
    norm2 = _dot(hi, hsel_ref[...]) + _dot(lo, hsel_ref[...])
    kmax = jnp.maximum(kmax_ref[...], jnp.max(norm2, axis=0, keepdims=True))
    bmax = jnp.maximum(bmax_ref[...], jnp.max(bias, axis=0, keepdims=True))
    kmax_ref[...] = kmax
    bmax_ref[...] = bmax
    knorm_max_ref[0, pl.ds(i, 1), :] = jnp.sqrt(kmax) * NORM_SLACK
    bias_max_ref[0, pl.ds(i, 1), :] = bmax


def _prep(kf, vf, vs, lfp, tp, boff):
    b, tk, _ = lfp.shape
    nblk = tk // tp
    r = jnp.arange(tp)
    tri = (r[None, :] <= r[:, None]).astype(BF16)
    h = jnp.arange(LANES)
    col = jnp.arange(D_FOX)
    sel = jnp.stack([(col[None, :] == ((h // 2) * LANES + (h % 2) * N_SPLIT + s)[:, None])
                     & (h[:, None] < H_FOX) for s in range(N_SPLIT)]).astype(BF16)
    hsel = (col[:, None] // HEAD_DIM == h[None, :]).astype(BF16)
    blk = lambda w: pl.BlockSpec((1, tp, w), lambda bi, i: (bi, i, 0))
    blk_l = lambda w: pl.BlockSpec((1, tp, w), lambda bi, i: (bi + boff, i, 0))
    blk_t = lambda w: pl.BlockSpec((1, w, tp), lambda bi, i: (bi, 0, i))
    per_batch = pl.BlockSpec((1, nblk, LANES), lambda bi, i: (bi, 0, 0))
    return pl.pallas_call(
        _prep_kernel,
        grid=(b, nblk),
        in_specs=[blk_l(D_FOX), blk_l(D_FOX), blk_l(D_SB), blk(LANES), _const_spec((tp, tp)),
                  _const_spec((N_SPLIT, LANES, D_FOX)), _const_spec((D_FOX, LANES))],
        out_specs=[blk(2 * D_FOX), blk_t(D_FOX), blk_t(D_SB), per_batch, per_batch],
        out_shape=[jax.ShapeDtypeStruct((b, tk, 2 * D_FOX), BF16),
                   jax.ShapeDtypeStruct((b, D_FOX, tk), BF16),
                   jax.ShapeDtypeStruct((b, D_SB, tk), BF16),
                   jax.ShapeDtypeStruct((b, nblk, LANES), F32),
                   jax.ShapeDtypeStruct((b, nblk, LANES), F32)],
        scratch_shapes=[pltpu.VMEM((1, LANES), F32)] * 3,
        compiler_params=pltpu.CompilerParams(dimension_semantics=("arbitrary", "arbitrary"),
                                             vmem_limit_bytes=VMEM_LIMIT),
        name="prep",
    )(kf, vf, vs, lfp, tri, sel, hsel)


def _conv_kernel(u_ref, prev_ref, buf_ref, w_ref, b_ref, g_ref, beta_ref, y_ref, xw_ref, *, tt, rows):
    first = pl.program_id(1) == 0
    xw_ref[0:HALO, :] = jnp.where(first, buf_ref[0], prev_ref[0])
    xw_ref[HALO:HALO + tt, :] = u_ref[0]
    off = HALO - (CONV_W - 1)
    for r0 in range(0, tt, rows):
        acc = jnp.zeros((rows, C_CONV), F32)
        for j in range(CONV_W):
            acc = acc + w_ref[j:j + 1, :] * xw_ref[r0 + j + off:r0 + j + off + rows, :]
        y = acc + b_ref[...]
        mu = jnp.mean(y, axis=-1, keepdims=True)
        yc = y - mu
        var = jnp.mean(yc * yc, axis=-1, keepdims=True)
        y = yc * lax.rsqrt(var + EPS) * g_ref[...] + beta_ref[...]
        y_ref[0, r0:r0 + rows, :] = (y * _sigmoid(y)).astype(BF16)


def _conv(u, buf_pad, w, b, g, beta, tt):
    bsz, t, _ = u.shape
    rows = min(tt, 64)
    per = tt // HALO
    return pl.pallas_call(
        functools.partial(_conv_kernel, tt=tt, rows=rows),
        grid=(bsz, t // tt),
        in_specs=[pl.BlockSpec((1, tt, C_CONV), lambda bi, i: (bi, i, 0)),
                  pl.BlockSpec((1, HALO, C_CONV), lambda bi, i: (bi, jnp.maximum(i * per - 1, 0), 0)),
                  pl.BlockSpec((1, HALO, C_CONV), lambda bi, i: (bi, 0, 0)),
                  _const_spec((CONV_W, C_CONV)), _const_spec((1, C_CONV)), _const_spec((1, C_CONV)),
                  _const_spec((1, C_CONV))],
        out_specs=pl.BlockSpec((1, tt, C_CONV), lambda bi, i: (bi, i, 0)),
        out_shape=jax.ShapeDtypeStruct((bsz, t, C_CONV), BF16),
        scratch_shapes=[pltpu.VMEM((HALO + tt, C_CONV), F32)],
        compiler_params=pltpu.CompilerParams(dimension_semantics=("arbitrary", "arbitrary")),
        name="conv",
    )(u, u, buf_pad, w, b, g, beta)


def _query_heads(q_ref):
    qt = jnp.transpose(q_ref[0].astype(F32))
    row = lax.broadcasted_iota(jnp.int32, qt.shape, 0)
    return row, jnp.where(row < HEAD_DIM, qt, 0.0), jnp.where(row >= HEAD_DIM, qt, 0.0)


def _diag_block(i, tq, tk, past):
    return (past + i * tq + tq - 1) // tk


def _positions(i, tq, tk, past):
    key_pos = lax.broadcasted_iota(jnp.int32, (tk, tq), 0)
    q_pos = past + i * tq + lax.broadcasted_iota(jnp.int32, (tk, tq), 1)
    return key_pos, q_pos


def _merge_heads(out_a, out_b):
    return jnp.transpose(jnp.concatenate([out_a, out_b], axis=0))


def _descend(jd, step, reaches):
    step(jd, True)

    def cond(c):
        j, go = c
        return jnp.logical_and(j >= 0, go)

    def body(c):
        j, _ = c
        step(j, False)
        return j - 1, reaches(j - 1)

    lax.while_loop(cond, body, (jd - 1, reaches(jd - 1)))


def _fox_kernel(bias_max_ref, knorm_max_ref, q_ref, k_ref, vt_ref, o_ref, m_ref, l_ref, acc_ref,
                *, tq, tk, past, nblk):
    bi = pl.program_id(0)
    hp = pl.program_id(1)
    i = pl.program_id(2)
    row, qa, qb = _query_heads(q_ref)
    ones_a = jnp.where(row < N_SPLIT, 1.0, 0.0)
    ones_b = jnp.where(row < 2 * N_SPLIT, 1.0, 0.0) - ones_a
    q_heads = (jnp.concatenate([qa, ones_a], axis=0).astype(BF16),
               jnp.concatenate([qb, ones_b], axis=0).astype(BF16))
    q_norms = [jnp.sqrt(jnp.sum(qh * qh, axis=0, keepdims=True)) * NORM_SLACK for qh in (qa, qb)]
    key_pos, q_pos = _positions(i, tq, tk, past)

    m_ref[...] = jnp.full_like(m_ref, -jnp.inf)
    l_ref[...] = jnp.zeros_like(l_ref)
    acc_ref[...] = jnp.zeros_like(acc_ref)

    def step(j, masked):
        start = pl.multiple_of(j * tk, tk)
        kblk = k_ref[0, pl.ds(start, tk), :]
        vt = vt_ref[0, :, pl.ds(start, tk)]
        scores = [_dot(kblk, q_heads[hh]) for hh in range(2)]
        for hh in range(2):
            s = scores[hh]
            if masked:
                s = jnp.where(start + key_pos <= q_pos, s, -jnp.inf)
            m_prev = m_ref[hh]
            m_new = jnp.maximum(m_prev, jnp.max(s, axis=0, keepdims=True))
            alpha = jnp.exp2(m_prev - m_new)
            p = jnp.exp2(s - m_new)
            l_ref[hh] = alpha * l_ref[hh] + jnp.sum(p, axis=0, keepdims=True)
            pv = _dot(vt[hh * HEAD_DIM:(hh + 1) * HEAD_DIM, :], p.astype(BF16))
            acc_ref[hh] = alpha * acc_ref[hh] + pv
            m_ref[hh] = m_new

    def reaches(j):
        base = (bi * nblk + jnp.maximum(j, 0)) * H_FOX + 2 * hp
        worst = None
        for hh in range(2):
            top = q_norms[hh] * knorm_max_ref[base + hh] + bias_max_ref[base + hh] - m_ref[hh]
            worst = top if worst is None else jnp.maximum(worst, top)
        return jnp.max(worst) >= -SKIP_LOG2

    _descend(_diag_block(i, tq, tk, past), step, reaches)
    o_ref[0] = _merge_heads(acc_ref[0] / l_ref[0], acc_ref[1] / l_ref[1]).astype(BF16)


def _fox(q, kt, vt, bias_max, knorm_max, past, tq, tk):
    b, t, _ = q.shape
    tkeys = kt.shape[1]
    nblk = tkeys // tk
    assert (tk % tq == 0 and past % tk == 0) or tk == tkeys
    smem = pl.BlockSpec(memory_space=pltpu.SMEM)
    per_head = lambda a: a[:, :, :H_FOX].reshape(-1)
    return pl.pallas_call(
        functools.partial(_fox_kernel, tq=tq, tk=tk, past=past, nblk=nblk),
        grid=(b, H_FOX // 2, t // tq),
        in_specs=[smem, smem,
                  pl.BlockSpec((1, tq, LANES), lambda bi, hp, i: (bi, i, hp)),
                  pl.BlockSpec((1, tkeys, 2 * LANES), lambda bi, hp, i: (bi, 0, hp)),
                  pl.BlockSpec((1, LANES, tkeys), lambda bi, hp, i: (bi, hp, 0))],
        out_specs=pl.BlockSpec((1, tq, LANES), lambda bi, hp, i: (bi, i, hp)),
        out_shape=jax.ShapeDtypeStruct((b, t, D_FOX), BF16),
        scratch_shapes=[pltpu.VMEM((2, 1, tq), F32), pltpu.VMEM((2, 1, tq), F32),
                        pltpu.VMEM((2, HEAD_DIM, tq), F32)],
        compiler_params=pltpu.CompilerParams(dimension_semantics=("arbitrary",) * 3,
                                             vmem_limit_bytes=VMEM_LIMIT),
        name="fox",
    )(per_head(bias_max), per_head(knorm_max), q, kt, vt)


def _sb_kernel(q_ref, k_ref, vt_ref, tri_ref, o_ref, carry_ref, acc_ref, *, tq, tk, past):
    i = pl.program_id(2)
    _, qa, qb = _query_heads(q_ref)
    q_heads = (qa.astype(BF16), qb.astype(BF16))
    key_pos, q_pos = _positions(i, tq, tk, past)

    carry_ref[...] = jnp.zeros_like(carry_ref)
    acc_ref[...] = jnp.zeros_like(acc_ref)

    def step(j, masked):
        start = pl.multiple_of(j * tk, tk)
        kblk = k_ref[0, pl.ds(start, tk), :]
        vt = vt_ref[0, :, pl.ds(start, tk)]
        tri2 = tri_ref[...]
        scores = [_dot(kblk, q_heads[hh]) for hh in range(2)]
        for hh in range(2):
            z = scores[hh]
            nl = jnp.maximum(z, 0.0) + jnp.log(1.0 + jnp.exp2(-jnp.abs(z))) * LOG2E
            if masked:
                valid = start + key_pos < q_pos
                nl = jnp.where(valid, nl, 0.0)
            hi = nl.astype(BF16)
            lo = (nl - hi.astype(F32)).astype(BF16)
            suffix = _dot(tri2, jnp.concatenate([hi, lo], axis=0))
            later = carry_ref[hh]
            a = jnp.exp2(z - nl - suffix - later)
            if masked:
                a = jnp.where(valid, a, 0.0)
            acc_ref[hh] = acc_ref[hh] + _dot(vt[hh * HEAD_DIM:(hh + 1) * HEAD_DIM, :], a.astype(BF16))
            carry_ref[hh] = later + jnp.sum(nl, axis=0, keepdims=True)

    def reaches(j):
        return jnp.min(jnp.minimum(carry_ref[0], carry_ref[1])) <= SKIP_LOG2

    _descend(_diag_block(i, tq, tk, past), step, reaches)
    o_ref[0] = _merge_heads(acc_ref[0], acc_ref[1]).astype(BF16)


def _sb(q, k, vt, past, tq, tk):
    b, t, _ = q.shape
    tkeys = k.shape[1]
    assert (tk % tq == 0 and past % tk == 0) or tk == tkeys
    r = jnp.arange(tk)
    tri = (r[None, :] > r[:, None]).astype(BF16)
    tri2 = jnp.concatenate([tri, tri], axis=1)
    return pl.pallas_call(
        functools.partial(_sb_kernel, tq=tq, tk=tk, past=past),
        grid=(b, H_SB // 2, t // tq),
        in_specs=[pl.BlockSpec((1, tq, LANES), lambda bi, hp, i: (bi, i, hp)),
                  pl.BlockSpec((1, tkeys, LANES), lambda bi, hp, i: (bi, 0, hp)),
                  pl.BlockSpec((1, LANES, tkeys), lambda bi, hp, i: (bi, hp, 0)),
                  _const_spec((tk, 2 * tk))],
        out_specs=pl.BlockSpec((1, tq, LANES), lambda bi, hp, i: (bi, i, hp)),
        out_shape=jax.ShapeDtypeStruct((b, t, D_SB), BF16),
        scratch_shapes=[pltpu.VMEM((2, 1, tq), F32), pltpu.VMEM((2, HEAD_DIM, tq), F32)],
        compiler_params=pltpu.CompilerParams(dimension_semantics=("arbitrary",) * 3,
                                             vmem_limit_bytes=VMEM_LIMIT),
        name="sb",
    )(q, k, vt, tri2)


def _pick_tile(n, pref):
    t = min(n, pref)
    while n % t:
        t //= 2
    return t


def _layer(x, past, w, final_norm, layer, depth, stacked):
    b, t, _ = x.shape
    past_k, past_v, past_lf, past_sk, past_sv, conv_buf = past
    p = past_k.shape[1]
    assert t >= CONV_W - 1 and t % HALO == 0
    n = b * t
    tm = _pick_tile(n, 512)

    (x1, qf, kf, vf, lfp, u, qs, ks, vs, ksb) = _dense_in(
        x.reshape(n, D_MODEL), w["n1"], w["wg1"], w["wu1"], w["wd1"], w["n2"], w["wp"], w["bfp"], tm,
        layer, depth, stacked)

    r3 = lambda a: a.reshape(-1, t, a.shape[-1])
    lfp3, u3 = r3(lfp), r3(u)
    if p:
        fill = -(p + t) % LANES
        cat = lambda old, new: jnp.concatenate(
            [old.reshape(b, p, -1).astype(new.dtype), new,
             jnp.zeros((b, fill, new.shape[-1]), new.dtype)], axis=1)
        mine = lambda a: r3(a)[layer * b:(layer + 1) * b]
        kf_all, vf_all, vs_all = cat(past_k, mine(kf)), cat(past_v, mine(vf)), cat(past_sv, mine(vs))
        lf_all = cat(jnp.pad(past_lf, ((0, 0), (0, 0), (0, LANES - H_FOX))), lfp3)
        ks_all = cat(past_sk, r3(ksb))
        boff = 0
        tq = t
        tk_fox = tk_sb = p + t + fill
    else:
        kf_all, vf_all, vs_all, lf_all, ks_all = r3(kf), r3(vf), r3(vs), lfp3, r3(ksb)
        boff = layer * b
        tq = tk_sb = _pick_tile(t, 256)
        tk_fox = _pick_tile(t, 512)

    kt, vft, vst, bias_max, knorm_max = _prep(kf_all, vf_all, vs_all, lf_all, tk_fox, boff)
    yf = _fox(r3(qf), kt, vft, bias_max, knorm_max, p, tq, tk_fox)
    ys = _sb(r3(qs), ks_all, vst, p, tq, tk_sb)
    buf_pad = jnp.pad(conv_buf, ((0, 0), (HALO - (CONV_W - 1), 0), (0, 0)))
    yc = _conv(u3, buf_pad, w["conv_w"], w["conv_b"], w["ln_g"], w["ln_b"], _pick_tile(t, 256))

    flat = lambda a: a.reshape(n, a.shape[-1])
    xo = _dense_out(x1, flat(yf), flat(yc), flat(ys), w["wo"], w["n3"], w["wg2"], w["wu2"], w["wd2"],
                    final_norm, tm, layer == depth - 1)
    return xo.reshape(b, t, D_MODEL), (kf, vf, ks, vs), (lfp3[..., :H_FOX], u3[:, t - (CONV_W - 1):, :])


def _layer_weights(l, ffn_norm, ffn_gate, ffn_up, ffn_down, mix_norm, w_in, b_forget,
                   conv_w, conv_b, conv_ln_g, conv_ln_b, w_out):
    wi = w_in[l]
    off_f = 3 * D_FOX
    off_glu = off_f + H_FOX
    off_qc = off_glu + 2 * C_CONV
    wp = jnp.concatenate([wi[:, :off_f], jnp.pad(wi[:, off_f:off_glu], ((0, 0), (0, LANES - H_FOX))),
                          wi[:, off_glu:off_qc], wi[:, off_qc:]], axis=1).astype(BF16)
    row = lambda a: a.reshape(1, -1).astype(F32)
    return dict(
        n1=row(ffn_norm[l, 0]), wg1=ffn_gate[l, 0].astype(BF16), wu1=ffn_up[l, 0].astype(BF16),
        wd1=ffn_down[l, 0].astype(BF16), n2=row(mix_norm[l]), wp=wp,
        bfp=jnp.pad(row(b_forget[l]), ((0, 0), (0, LANES - H_FOX))),
        conv_w=conv_w[l], conv_b=row(conv_b[l]), ln_g=row(conv_ln_g[l]), ln_b=row(conv_ln_b[l]),
        wo=w_out[l].astype(BF16), n3=row(ffn_norm[l, 1]), wg2=ffn_gate[l, 1].astype(BF16),
        wu2=ffn_up[l, 1].astype(BF16), wd2=ffn_down[l, 1].astype(BF16))


def kernel(x_prompt, x_sample, cache_fox_k, cache_fox_v, cache_fox_logf, cache_sb_k, cache_sb_v, state_conv,
           ffn_norm, ffn_gate, ffn_up, ffn_down, mix_norm, w_in, b_forget, conv_w, conv_b, conv_ln_g,
           conv_ln_b, w_out, final_norm):
    depth = w_in.shape[0]
    bp = x_prompt.shape[0]
    dt = x_prompt.dtype
    empty = (jnp.zeros((bp, 0, H_FOX, HEAD_DIM), dt), jnp.zeros((bp, 0, H_FOX, HEAD_DIM), dt),
             jnp.zeros((bp, 0, H_FOX), dt), jnp.zeros((bp, 0, H_SB, HEAD_DIM), dt),
             jnp.zeros((bp, 0, H_SB, HEAD_DIM), dt), jnp.zeros((bp, CONV_W - 1, C_CONV), dt))
    fn = final_norm.reshape(1, -1).astype(F32)
    xp, xs = x_prompt, x_sample
    stacked_p = stacked_s = None
    small_p, small_s = [], []
    for l in range(depth):
        w = _layer_weights(l, ffn_norm, ffn_gate, ffn_up, ffn_down, mix_norm, w_in, b_forget,
                           conv_w, conv_b, conv_ln_g, conv_ln_b, w_out)
        xp, stacked_p, sp = _layer(xp, empty, w, fn, l, depth, stacked_p)
        cache = (cache_fox_k[l], cache_fox_v[l], cache_fox_logf[l], cache_sb_k[l], cache_sb_v[l],
                 state_conv[l])
        xs, stacked_s, ss = _layer(xs, cache, w, fn, l, depth, stacked_s)
        small_p.append(sp)
        small_s.append(ss)

    def outputs(x, stacked, small):
        b, t, _ = x.shape
        heads = lambda a: a.reshape(depth, b, t, a.shape[-1] // HEAD_DIM, HEAD_DIM)
        kf, vf, ks, vs = stacked
        stack = lambda k: jnp.stack([it[k] for it in small], 0)
        return heads(kf), heads(vf), stack(0), heads(ks), heads(vs), stack(1)

    return (xp, xs) + outputs(xp, stacked_p, small_p) + outputs(xs, stacked_s, small_s)
```

```python
import functools

import jax
import jax.numpy as jnp
from jax import lax
from jax.experimental import pallas as pl
from jax.experimental.pallas import tpu as pltpu

D_MODEL = 1024
HEAD_DIM = 64
D_FOX = 512
C_CONV = 256
D_SB = 256
H_FOX = 8
H_SB = 4
CONV_W = 31
D_FF = 2816
EPS = 1e-6

LANES = 128
SUBLANES = 8
HALO = 32
N_SPLIT = 3
VMEM_LIMIT = 56 * 1024 * 1024
SKIP_LOG = 110.0
LOG2E = 1.4426950408889634
SKIP_LOG2 = SKIP_LOG * LOG2E
NORM_SLACK = 1.01

ZQ, ZK, ZV, ZF, ZA, ZG, ZQS, ZKS, ZVS, ZEND = 0, 512, 1024, 1536, 1664, 1920, 2176, 2432, 2688, 2944

F32 = jnp.float32
BF16 = jnp.bfloat16


def _dot(a, b):
    return jnp.dot(a, b, preferred_element_type=F32)


def _sigmoid(x):
    return 1.0 / (1.0 + jnp.exp(-x))


def _softplus(x):
    return jnp.maximum(x, 0.0) + jnp.log1p(jnp.exp(-jnp.abs(x)))


def _rms(x, g):
    return x * lax.rsqrt(jnp.mean(x * x, axis=-1, keepdims=True) + EPS) * g


def _ffn_half(x, g_ref, wg_ref, wu_ref, wd_ref):
    hn = _rms(x, g_ref[...]).astype(BF16)
    g = _dot(hn, wg_ref[...])
    u = _dot(hn, wu_ref[...])
    act = (g * _sigmoid(g) * u).astype(BF16)
    return x + 0.5 * _dot(act, wd_ref[...])


def _split3(x):
    a = x.astype(BF16)
    r = x - a.astype(F32)
    b = r.astype(BF16)
    c = (r - b.astype(F32)).astype(BF16)
    return a, b, c


def _dense_in_kernel(x_ref, n1_ref, wg_ref, wu_ref, wd_ref, n2_ref, wp_ref, bf_ref, *refs):
    x1_ref, qf_ref, kf_ref, vf_ref, lfp_ref, u_ref, qs_ref, ks_ref, vs_ref, ksb_ref = refs[-10:]
    x1 = _ffn_half(x_ref[...], n1_ref, wg_ref, wu_ref, wd_ref)
    x1_ref[...] = x1
    h = _rms(x1, n2_ref[...]).astype(BF16)
    z = _dot(h, wp_ref[...])
    scale = HEAD_DIM ** -0.5 * LOG2E
    qf_ref[...] = (z[:, ZQ:ZK] * scale).astype(BF16)
    kf_ref[...] = z[:, ZK:ZV]
    vf_ref[...] = z[:, ZV:ZF]
    zf = z[:, ZF:ZA] + bf_ref[...]
    lane = lax.broadcasted_iota(jnp.int32, zf.shape, 1)
    lfp_ref[...] = jnp.where(lane < H_FOX, -_softplus(-zf), 0.0)
    u_ref[...] = z[:, ZA:ZG] * _sigmoid(z[:, ZG:ZQS])
    qs_ref[...] = (z[:, ZQS:ZKS] * scale).astype(BF16)
    ks = z[:, ZKS:ZVS]
    ks_ref[...] = ks
    ksb_ref[...] = ks.astype(BF16)
    vs_ref[...] = z[:, ZVS:ZEND]


def _const_spec(shape):
    return pl.BlockSpec(shape, lambda *_: (0,) * len(shape), pipeline_mode=pl.Buffered(1))


STACKED = (2, 3, 7, 8)


def _dense_in(x, n1, wg, wu, wd, n2, wp, bfp, tm, layer, depth, stacked):
    n = x.shape[0]
    nt = n // tm
    row = lambda w: pl.BlockSpec((tm, w), lambda i: (i, 0))
    layer_row = lambda w: pl.BlockSpec((tm, w), lambda i: (i + layer * nt, 0))
    widths = (D_MODEL, D_FOX, D_FOX, D_FOX, LANES, C_CONV, D_SB, D_SB, D_SB, D_SB)
    dtypes = (F32, BF16, F32, F32, F32, F32, BF16, F32, F32, BF16)
    rows = [depth * n if k in STACKED else n for k in range(len(widths))]
    prev = () if stacked is None else tuple(stacked)
    n_fixed = 8
    return pl.pallas_call(
        _dense_in_kernel,
        grid=(nt,),
        in_specs=[row(D_MODEL), _const_spec((1, D_MODEL)), _const_spec((D_MODEL, D_FF)),
                  _const_spec((D_MODEL, D_FF)), _const_spec((D_FF, D_MODEL)), _const_spec((1, D_MODEL)),
                  _const_spec((D_MODEL, ZEND)), _const_spec((1, LANES))]
                 + [pl.BlockSpec(memory_space=pl.ANY)] * len(prev),
        out_specs=[layer_row(w) if k in STACKED else row(w) for k, w in enumerate(widths)],
        out_shape=[jax.ShapeDtypeStruct((r, w), d) for r, w, d in zip(rows, widths, dtypes)],
        input_output_aliases={n_fixed + a: k for a, k in enumerate(STACKED)} if prev else {},
        compiler_params=pltpu.CompilerParams(dimension_semantics=("arbitrary",),
                                             vmem_limit_bytes=VMEM_LIMIT),
        name="dense_in",
    )(x, n1, wg, wu, wd, n2, wp, bfp, *prev)


def _dense_out_kernel(x1_ref, yf_ref, yc_ref, ys_ref, wo_ref, n_ref, wg_ref, wu_ref, wd_ref, fn_ref,
                      o_ref, *, final):
    x2 = (x1_ref[...] + _dot(yf_ref[...], wo_ref[0:D_FOX, :])
          + _dot(yc_ref[...], wo_ref[D_FOX:D_FOX + C_CONV, :])
          + _dot(ys_ref[...], wo_ref[D_FOX + C_CONV:, :]))
    x3 = _ffn_half(x2, n_ref, wg_ref, wu_ref, wd_ref)
    o_ref[...] = _rms(x3, fn_ref[...]) if final else x3


def _dense_out(x1, yf, yc, ys, wo, n, wg, wu, wd, fn, tm, final):
    nrow = x1.shape[0]
    row = lambda w: pl.BlockSpec((tm, w), lambda i: (i, 0))
    return pl.pallas_call(
        functools.partial(_dense_out_kernel, final=final),
        grid=(nrow // tm,),
        in_specs=[row(D_MODEL), row(D_FOX), row(C_CONV), row(D_SB), _const_spec((D_MODEL, D_MODEL)),
                  _const_spec((1, D_MODEL)), _const_spec((D_MODEL, D_FF)), _const_spec((D_MODEL, D_FF)),
                  _const_spec((D_FF, D_MODEL)), _const_spec((1, D_MODEL))],
        out_specs=row(D_MODEL),
        out_shape=jax.ShapeDtypeStruct((nrow, D_MODEL), F32),
        compiler_params=pltpu.CompilerParams(dimension_semantics=("arbitrary",),
                                             vmem_limit_bytes=VMEM_LIMIT),
        name="dense_out",
    )(x1, yf, yc, ys, wo, n, wg, wu, wd, fn)


def _prep_kernel(kf_ref, vf_ref, vs_ref, lfp_ref, tri_ref, sel_ref, hsel_ref,
                 kt_ref, vft_ref, vst_ref, bias_max_ref, knorm_max_ref, carry_ref, bmax_ref, kmax_ref):
    i = pl.program_id(1)

    @pl.when(i == 0)
    def _():
        carry_ref[...] = jnp.zeros_like(carry_ref)
        bmax_ref[...] = jnp.full_like(bmax_ref, -jnp.inf)
        kmax_ref[...] = jnp.zeros_like(kmax_ref)

    tri = tri_ref[...]
    cum = carry_ref[...]
    for part in _split3(lfp_ref[0]):
        cum = cum + _dot(tri, part)
    carry_ref[...] = cum[cum.shape[0] - 1:, :]
    bias = -cum * LOG2E
    aug = None
    for s, part in enumerate(_split3(bias)):
        term = _dot(part, sel_ref[s])
        aug = term if aug is None else aug + term
    kb = kf_ref[0].astype(BF16)
    for hp in range(H_FOX // 2):
        base = 2 * LANES * hp
        kt_ref[0, :, base:base + LANES] = kb[:, hp * LANES:(hp + 1) * LANES]
        kt_ref[0, :, base + LANES:base + 2 * LANES] = aug[:, hp * LANES:(hp + 1) * LANES].astype(BF16)
    vft_ref[0] = jnp.transpose(vf_ref[0]).astype(BF16)
    vst_ref[0] = jnp.transpose(vs_ref[0]).astype(BF16)

    ksq = kb.astype(F32)
    ksq = ksq * ksq
    hi = ksq.astype(BF16)
    lo = (ksq - hi.astype(F32)).astype(BF16)
    norm2 = _dot(hi, hsel_ref[...]) + _dot(lo, hsel_ref[...])
    kmax = jnp.maximum(kmax_ref[...], jnp.max(norm2, axis=0, keepdims=True))
    bmax = jnp.maximum(bmax_ref[...], jnp.max(bias, axis=0, keepdims=True))
    kmax_ref[...] = kmax
    bmax_ref[...] = bmax
    knorm_max_ref[0, pl.ds(i, 1), :] = jnp.sqrt(kmax) * NORM_SLACK
    bias_max_ref[0, pl.ds(i, 1), :] = bmax


def _prep(kf, vf, vs, lfp, tp, boff):
    b, tk, _ = lfp.shape
    nblk = tk // tp
    r = jnp.arange(tp)
    tri = (r[None, :] <= r[:, None]).astype(BF16)
    h = jnp.arange(LANES)
    col = jnp.arange(D_FOX)
    sel = jnp.stack([(col[None, :] == ((h // 2) * LANES + (h % 2) * N_SPLIT + s)[:, None])
                     & (h[:, None] < H_FOX) for s in range(N_SPLIT)]).astype(BF16)
    hsel = (col[:, None] // HEAD_DIM == h[None, :]).astype(BF16)
    blk = lambda w: pl.BlockSpec((1, tp, w), lambda bi, i: (bi, i, 0))
    blk_l = lambda w: pl.BlockSpec((1, tp, w), lambda bi, i: (bi + boff, i, 0))
    blk_t = lambda w: pl.BlockSpec((1, w, tp), lambda bi, i: (bi, 0, i))
    per_batch = pl.BlockSpec((1, nblk, LANES), lambda bi, i: (bi, 0, 0))
    return pl.pallas_call(
        _prep_kernel,
        grid=(b, nblk),
        in_specs=[blk_l(D_FOX), blk_l(D_FOX), blk_l(D_SB), blk(LANES), _const_spec((tp, tp)),
                  _const_spec((N_SPLIT, LANES, D_FOX)), _const_spec((D_FOX, LANES))],
        out_specs=[blk(2 * D_FOX), blk_t(D_FOX), blk_t(D_SB), per_batch, per_batch],
        out_shape=[jax.ShapeDtypeStruct((b, tk, 2 * D_FOX), BF16),
                   jax.ShapeDtypeStruct((b, D_FOX, tk), BF16),
                   jax.ShapeDtypeStruct((b, D_SB, tk), BF16),
                   jax.ShapeDtypeStruct((b, nblk, LANES), F32),
                   jax.ShapeDtypeStruct((b, nblk, LANES), F32)],
        scratch_shapes=[pltpu.VMEM((1, LANES), F32)] * 3,
        compiler_params=pltpu.CompilerParams(dimension_semantics=("arbitrary", "arbitrary"),
                                             vmem_limit_bytes=VMEM_LIMIT),
        name="prep",
    )(kf, vf, vs, lfp, tri, sel, hsel)


def _conv_kernel(u_ref, prev_ref, buf_ref, w_ref, b_ref, g_ref, beta_ref, y_ref, xw_ref, sh_ref, *, tt, rows):
    first = pl.program_id(1) == 0
    xw_ref[0:HALO, :] = jnp.where(first, buf_ref[0], prev_ref[0])
    xw_ref[HALO:HALO + tt, :] = u_ref[0]
    off = HALO - (CONV_W - 1)
    for s in range(1, SUBLANES):
        sh_ref[s - 1] = xw_ref[s:s + tt + HALO - SUBLANES, :]
    for r0 in range(0, tt, rows):
        acc = jnp.zeros((rows, C_CONV), F32)
        for j in range(CONV_W):
            s, base = (j + off) % SUBLANES, (j + off) // SUBLANES * SUBLANES
            src = xw_ref if s == 0 else sh_ref.at[s - 1]
            acc = acc + w_ref[j:j + 1, :] * src[base + r0:base + r0 + rows, :]
        y = acc + b_ref[...]
        mu = jnp.mean(y, axis=-1, keepdims=True)
        yc = y - mu
        var = jnp.mean(yc * yc, axis=-1, keepdims=True)
        y = yc * lax.rsqrt(var + EPS) * g_ref[...] + beta_ref[...]
        y_ref[0, r0:r0 + rows, :] = (y * _sigmoid(y)).astype(BF16)


def _conv(u, buf_pad, w, b, g, beta, tt):
    bsz, t, _ = u.shape
    rows = min(tt, 64)
    per = tt // HALO
    return pl.pallas_call(
        functools.partial(_conv_kernel, tt=tt, rows=rows),
        grid=(bsz, t // tt),
        in_specs=[pl.BlockSpec((1, tt, C_CONV), lambda bi, i: (bi, i, 0)),
                  pl.BlockSpec((1, HALO, C_CONV), lambda bi, i: (bi, jnp.maximum(i * per - 1, 0), 0)),
                  pl.BlockSpec((1, HALO, C_CONV), lambda bi, i: (bi, 0, 0)),
                  _const_spec((CONV_W, C_CONV)), _const_spec((1, C_CONV)), _const_spec((1, C_CONV)),
                  _const_spec((1, C_CONV))],
        out_specs=pl.BlockSpec((1, tt, C_CONV), lambda bi, i: (bi, i, 0)),
        out_shape=jax.ShapeDtypeStruct((bsz, t, C_CONV), BF16),
        scratch_shapes=[pltpu.VMEM((HALO + tt, C_CONV), F32),
                        pltpu.VMEM((SUBLANES - 1, HALO + tt - SUBLANES, C_CONV), F32)],
        compiler_params=pltpu.CompilerParams(dimension_semantics=("arbitrary", "arbitrary")),
        name="conv",
    )(u, u, buf_pad, w, b, g, beta)


def _query_heads(q_ref):
    qt = jnp.transpose(q_ref[0].astype(F32))
    row = lax.broadcasted_iota(jnp.int32, qt.shape, 0)
    return row, jnp.where(row < HEAD_DIM, qt, 0.0), jnp.where(row >= HEAD_DIM, qt, 0.0)


def _diag_block(i, tq, tk, past):
    return (past + i * tq + tq - 1) // tk


def _positions(i, tq, tk, past):
    key_pos = lax.broadcasted_iota(jnp.int32, (tk, tq), 0)
    q_pos = past + i * tq + lax.broadcasted_iota(jnp.int32, (tk, tq), 1)
    return key_pos, q_pos


def _merge_heads(out_a, out_b):
    return jnp.transpose(jnp.concatenate([out_a, out_b], axis=0))


def _descend(jd, step, reaches):
    step(jd, True)

    def cond(c):
        j, go = c
        return jnp.logical_and(j >= 0, go)

    def body(c):
        j, _ = c
        step(j, False)
        return j - 1, reaches(j - 1)

    lax.while_loop(cond, body, (jd - 1, reaches(jd - 1)))


def _fox_kernel(bias_max_ref, knorm_max_ref, q_ref, k_ref, vt_ref, o_ref,
                m_ref, l_ref, acc_ref, s_ref, mblk_ref, alpha_ref, *, tq, tk, past, nblk):
    bi = pl.program_id(0)
    hp = pl.program_id(1)
    i = pl.program_id(2)
    row, qa, qb = _query_heads(q_ref)
    ones_a = jnp.where(row < N_SPLIT, 1.0, 0.0)
    ones_b = jnp.where(row < 2 * N_SPLIT, 1.0, 0.0) - ones_a
    q_heads = (jnp.concatenate([qa, ones_a], axis=0).astype(BF16),
               jnp.concatenate([qb, ones_b], axis=0).astype(BF16))
    q_norms = [jnp.sqrt(jnp.sum(qh * qh, axis=0, keepdims=True)) * NORM_SLACK for qh in (qa, qb)]
    key_pos, q_pos = _positions(i, tq, tk, past)

    m_ref[...] = jnp.full_like(m_ref, -jnp.inf)
    l_ref[...] = jnp.zeros_like(l_ref)
    acc_ref[...] = jnp.zeros_like(acc_ref)

    def score(j, masked, slot):
        start = pl.multiple_of(j * tk, tk)
        kblk = k_ref[0, pl.ds(start, tk), :]
        scores = [_dot(kblk, q_heads[hh]) for hh in range(2)]
        for hh in range(2):
            s = scores[hh]
            if masked:
                s = jnp.where(start + key_pos <= q_pos, s, -jnp.inf)
            m_prev = m_ref[hh]
            m_new = jnp.maximum(m_prev, jnp.max(s, axis=0, keepdims=True))
            alpha_ref[slot, hh] = jnp.exp2(m_prev - m_new)
            mblk_ref[slot, hh] = m_new
            m_ref[hh] = m_new
            s_ref[slot, hh] = s

    def accumulate(j, slot):
        start = pl.multiple_of(j * tk, tk)
        vt = vt_ref[0, :, pl.ds(start, tk)]
        for hh in range(2):
            p = jnp.exp2(s_ref[slot, hh] - mblk_ref[slot, hh])
            alpha = alpha_ref[slot, hh]
            l_ref[hh] = alpha * l_ref[hh] + jnp.sum(p, axis=0, keepdims=True)
            pv = _dot(vt[hh * HEAD_DIM:(hh + 1) * HEAD_DIM, :], p.astype(BF16))
            acc_ref[hh] = alpha * acc_ref[hh] + pv

    def reaches(j):
        base = (bi * nblk + jnp.maximum(j, 0)) * H_FOX + 2 * hp
        worst = None
        for hh in range(2):
            top = q_norms[hh] * knorm_max_ref[base + hh] + bias_max_ref[base + hh] - m_ref[hh]
            worst = top if worst is None else jnp.maximum(worst, top)
        return jnp.max(worst) >= -SKIP_LOG2

    jd = _diag_block(i, tq, tk, past)
    score(jd, True, 0)
    waiting = lambda j: (jd - j - 1) & 1

    def cond(c):
        j, go = c
        return jnp.logical_and(j >= 0, go)

    def body(c):
        j, _ = c
        for slot in range(2):
            @pl.when(waiting(j) == slot)
            def _():
                score(j, False, 1 - slot)
                accumulate(j + 1, slot)
        return j - 1, reaches(j - 1)

    j_stop, _ = lax.while_loop(cond, body, (jd - 1, reaches(jd - 1)))
    for slot in range(2):
        @pl.when(waiting(j_stop) == slot)
        def _():
            accumulate(j_stop + 1, slot)
    o_ref[0] = _merge_heads(acc_ref[0] / l_ref[0], acc_ref[1] / l_ref[1]).astype(BF16)


def _fox(q, kt, vt, bias_max, knorm_max, past, tq, tk):
    b, t, _ = q.shape
    tkeys = kt.shape[1]
    nblk = tkeys // tk
    assert (tk % tq == 0 and past % tk == 0) or tk == tkeys
    smem = pl.BlockSpec(memory_space=pltpu.SMEM)
    per_head = lambda a: a[:, :, :H_FOX].reshape(-1)
    return pl.pallas_call(
        functools.partial(_fox_kernel, tq=tq, tk=tk, past=past, nblk=nblk),
        grid=(b, H_FOX // 2, t // tq),
        in_specs=[smem, smem,
                  pl.BlockSpec((1, tq, LANES), lambda bi, hp, i: (bi, i, hp)),
                  pl.BlockSpec((1, tkeys, 2 * LANES), lambda bi, hp, i: (bi, 0, hp)),
                  pl.BlockSpec((1, LANES, tkeys), lambda bi, hp, i: (bi, hp, 0))],
        out_specs=pl.BlockSpec((1, tq, LANES), lambda bi, hp, i: (bi, i, hp)),
        out_shape=jax.ShapeDtypeStruct((b, t, D_FOX), BF16),
        scratch_shapes=[pltpu.VMEM((2, 1, tq), F32), pltpu.VMEM((2, 1, tq), F32),
                        pltpu.VMEM((2, HEAD_DIM, tq), F32), pltpu.VMEM((2, 2, tk, tq), F32),
                        pltpu.VMEM((2, 2, 1, tq), F32), pltpu.VMEM((2, 2, 1, tq), F32)],
        compiler_params=pltpu.CompilerParams(dimension_semantics=("arbitrary",) * 3,
                                             vmem_limit_bytes=VMEM_LIMIT),
        name="fox",
    )(per_head(bias_max), per_head(knorm_max), q, kt, vt)


def _sb_kernel(q_ref, k_ref, vt_ref, tri_ref, o_ref, carry_ref, acc_ref, *, tq, tk, past):
    i = pl.program_id(2)
    _, qa, qb = _query_heads(q_ref)
    q_heads = (qa.astype(BF16), qb.astype(BF16))
    key_pos, q_pos = _positions(i, tq, tk, past)

    carry_ref[...] = jnp.zeros_like(carry_ref)
    acc_ref[...] = jnp.zeros_like(acc_ref)

    def step(j, masked):
        start = pl.multiple_of(j * tk, tk)
        kblk = k_ref[0, pl.ds(start, tk), :]
        vt = vt_ref[0, :, pl.ds(start, tk)]
        tri2 = tri_ref[...]
        scores = [_dot(kblk, q_heads[hh]) for hh in range(2)]
        for hh in range(2):
            z = scores[hh]
            nl = jnp.maximum(z, 0.0) + jnp.log(1.0 + jnp.exp2(-jnp.abs(z))) * LOG2E
            if masked:
                valid = start + key_pos < q_pos
                nl = jnp.where(valid, nl, 0.0)
            hi = nl.astype(BF16)
            lo = (nl - hi.astype(F32)).astype(BF16)
            suffix = _dot(tri2, jnp.concatenate([hi, lo], axis=0))
            later = carry_ref[hh]
            a = jnp.exp2(z - nl - suffix - later)
            if masked:
                a = jnp.where(valid, a, 0.0)
            acc_ref[hh] = acc_ref[hh] + _dot(vt[hh * HEAD_DIM:(hh + 1) * HEAD_DIM, :], a.astype(BF16))
            carry_ref[hh] = later + jnp.sum(nl, axis=0, keepdims=True)

    def reaches(j):
        return jnp.min(jnp.minimum(carry_ref[0], carry_ref[1])) <= SKIP_LOG2

    _descend(_diag_block(i, tq, tk, past), step, reaches)
    o_ref[0] = _merge_heads(acc_ref[0], acc_ref[1]).astype(BF16)


def _sb(q, k, vt, past, tq, tk):
    b, t, _ = q.shape
    tkeys = k.shape[1]
    assert (tk % tq == 0 and past % tk == 0) or tk == tkeys
    r = jnp.arange(tk)
    tri = (r[None, :] > r[:, None]).astype(BF16)
    tri2 = jnp.concatenate([tri, tri], axis=1)
    return pl.pallas_call(
        functools.partial(_sb_kernel, tq=tq, tk=tk, past=past),
        grid=(b, H_SB // 2, t // tq),
        in_specs=[pl.BlockSpec((1, tq, LANES), lambda bi, hp, i: (bi, i, hp)),
                  pl.BlockSpec((1, tkeys, LANES), lambda bi, hp, i: (bi, 0, hp)),
                  pl.BlockSpec((1, LANES, tkeys), lambda bi, hp, i: (bi, hp, 0)),
                  _const_spec((tk, 2 * tk))],
        out_specs=pl.BlockSpec((1, tq, LANES), lambda bi, hp, i: (bi, i, hp)),
        out_shape=jax.ShapeDtypeStruct((b, t, D_SB), BF16),
        scratch_shapes=[pltpu.VMEM((2, 1, tq), F32), pltpu.VMEM((2, HEAD_DIM, tq), F32)],
        compiler_params=pltpu.CompilerParams(dimension_semantics=("arbitrary",) * 3,
                                             vmem_limit_bytes=VMEM_LIMIT),
        name="sb",
    )(q, k, vt, tri2)


def _pick_tile(n, pref):
    t = min(n, pref)
    while n % t:
        t //= 2
    return t


def _layer(x, past, w, final_norm, layer, depth, stacked):
    b, t, _ = x.shape
    past_k, past_v, past_lf, past_sk, past_sv, conv_buf = past
    p = past_k.shape[1]
    assert t >= CONV_W - 1 and t % HALO == 0
    n = b * t
    tm = _pick_tile(n, 512)

    (x1, qf, kf, vf, lfp, u, qs, ks, vs, ksb) = _dense_in(
        x.reshape(n, D_MODEL), w["n1"], w["wg1"], w["wu1"], w["wd1"], w["n2"], w["wp"], w["bfp"], tm,
        layer, depth, stacked)

    r3 = lambda a: a.reshape(-1, t, a.shape[-1])
    lfp3, u3 = r3(lfp), r3(u)
    if p:
        fill = -(p + t) % LANES
        cat = lambda old, new: jnp.concatenate(
            [old.reshape(b, p, -1).astype(new.dtype), new,
             jnp.zeros((b, fill, new.shape[-1]), new.dtype)], axis=1)
        mine = lambda a: r3(a)[layer * b:(layer + 1) * b]
        kf_all, vf_all, vs_all = cat(past_k, mine(kf)), cat(past_v, mine(vf)), cat(past_sv, mine(vs))
        lf_all = cat(jnp.pad(past_lf, ((0, 0), (0, 0), (0, LANES - H_FOX))), lfp3)
        ks_all = cat(past_sk, r3(ksb))
        boff = 0
        tq = t
        tk_fox = p + t + fill
        tk_sb = LANES if (LANES % t == 0 and p % LANES == 0) else tk_fox
    else:
        kf_all, vf_all, vs_all, lf_all, ks_all = r3(kf), r3(vf), r3(vs), lfp3, r3(ksb)
        boff = layer * b
        tq = tk_sb = _pick_tile(t, 256)
        tk_fox = _pick_tile(t, 512)

    kt, vft, vst, bias_max, knorm_max = _prep(kf_all, vf_all, vs_all, lf_all, tk_fox, boff)
    yf = _fox(r3(qf), kt, vft, bias_max, knorm_max, p, tq, tk_fox)
    ys = _sb(r3(qs), ks_all, vst, p, tq, tk_sb)
    buf_pad = jnp.pad(conv_buf, ((0, 0), (HALO - (CONV_W - 1), 0), (0, 0)))
    yc = _conv(u3, buf_pad, w["conv_w"], w["conv_b"], w["ln_g"], w["ln_b"], _pick_tile(t, 256))

    flat = lambda a: a.reshape(n, a.shape[-1])
    xo = _dense_out(x1, flat(yf), flat(yc), flat(ys), w["wo"], w["n3"], w["wg2"], w["wu2"], w["wd2"],
                    final_norm, tm, layer == depth - 1)
    return xo.reshape(b, t, D_MODEL), (kf, vf, ks, vs), (lfp3[..., :H_FOX], u3[:, t - (CONV_W - 1):, :])


def _layer_weights(l, ffn_norm, ffn_gate, ffn_up, ffn_down, mix_norm, w_in, b_forget,
                   conv_w, conv_b, conv_ln_g, conv_ln_b, w_out):
    wi = w_in[l]
    off_f = 3 * D_FOX
    off_glu = off_f + H_FOX
    off_qc = off_glu + 2 * C_CONV
    wp = jnp.concatenate([wi[:, :off_f], jnp.pad(wi[:, off_f:off_glu], ((0, 0), (0, LANES - H_FOX))),
                          wi[:, off_glu:off_qc], wi[:, off_qc:]], axis=1).astype(BF16)
    row = lambda a: a.reshape(1, -1).astype(F32)
    return dict(
        n1=row(ffn_norm[l, 0]), wg1=ffn_gate[l, 0].astype(BF16), wu1=ffn_up[l, 0].astype(BF16),
        wd1=ffn_down[l, 0].astype(BF16), n2=row(mix_norm[l]), wp=wp,
        bfp=jnp.pad(row(b_forget[l]), ((0, 0), (0, LANES - H_FOX))),
        conv_w=conv_w[l], conv_b=row(conv_b[l]), ln_g=row(conv_ln_g[l]), ln_b=row(conv_ln_b[l]),
        wo=w_out[l].astype(BF16), n3=row(ffn_norm[l, 1]), wg2=ffn_gate[l, 1].astype(BF16),
        wu2=ffn_up[l, 1].astype(BF16), wd2=ffn_down[l, 1].astype(BF16))


def kernel(x_prompt, x_sample, cache_fox_k, cache_fox_v, cache_fox_logf, cache_sb_k, cache_sb_v, state_conv,
           ffn_norm, ffn_gate, ffn_up, ffn_down, mix_norm, w_in, b_forget, conv_w, conv_b, conv_ln_g,
           conv_ln_b, w_out, final_norm):
    depth = w_in.shape[0]
    bp = x_prompt.shape[0]
    dt = x_prompt.dtype
    empty = (jnp.zeros((bp, 0, H_FOX, HEAD_DIM), dt), jnp.zeros((bp, 0, H_FOX, HEAD_DIM), dt),
             jnp.zeros((bp, 0, H_FOX), dt), jnp.zeros((bp, 0, H_SB, HEAD_DIM), dt),
             jnp.zeros((bp, 0, H_SB, HEAD_DIM), dt), jnp.zeros((bp, CONV_W - 1, C_CONV), dt))
    fn = final_norm.reshape(1, -1).astype(F32)
    xp, xs = x_prompt, x_sample
    stacked_p = stacked_s = None
    small_p, small_s = [], []
    for l in range(depth):
        w = _layer_weights(l, ffn_norm, ffn_gate, ffn_up, ffn_down, mix_norm, w_in, b_forget,
                           conv_w, conv_b, conv_ln_g, conv_ln_b, w_out)
        xp, stacked_p, sp = _layer(xp, empty, w, fn, l, depth, stacked_p)
        cache = (cache_fox_k[l], cache_fox_v[l], cache_fox_logf[l], cache_sb_k[l], cache_sb_v[l],
                 state_conv[l])
        xs, stacked_s, ss = _layer(xs, cache, w, fn, l, depth, stacked_s)
        small_p.append(sp)
        small_s.append(ss)

    def outputs(x, stacked, small):
        b, t, _ = x.shape
        heads = lambda a: a.reshape(depth, b, t, a.shape[-1] // HEAD_DIM, HEAD_DIM)
        kf, vf, ks, vs = stacked
        stack = lambda k: jnp.stack([it[k] for it in small], 0)
        return heads(kf), heads(vf), stack(0), heads(ks), heads(vs), stack(1)

    return (xp, xs) + outputs(xp, stacked_p, small_p) + outputs(xs, stacked_s, small_s)
```

```python
import functools

import jax
import jax.numpy as jnp
from jax import lax
from jax.experimental import pallas as pl
from jax.experimental.pallas import tpu as pltpu

D_MODEL = 1024
HEAD_DIM = 64
D_FOX = 512
C_CONV = 256
D_SB = 256
H_FOX = 8
H_SB = 4
CONV_W = 31
D_FF = 2816
EPS = 1e-6

LANES = 128
SUBLANES = 8
HALO = 32
N_SPLIT = 3
VMEM_LIMIT = 56 * 1024 * 1024
SKIP_LOG = 110.0
LOG2E = 1.4426950408889634
SKIP_LOG2 = SKIP_LOG * LOG2E
NORM_SLACK = 1.01

ZQ, ZK, ZV, ZF, ZA, ZG, ZQS, ZKS, ZVS, ZEND = 0, 512, 1024, 1536, 1664, 1920, 2176, 2432, 2688, 2944

F32 = jnp.float32
BF16 = jnp.bfloat16


def _dot(a, b):
    return jnp.dot(a, b, preferred_element_type=F32)


def _sigmoid(x):
    return 1.0 / (1.0 + jnp.exp(-x))


def _softplus(x):
    return jnp.maximum(x, 0.0) + jnp.log1p(jnp.exp(-jnp.abs(x)))


def _rms(x, g):
    return x * lax.rsqrt(jnp.mean(x * x, axis=-1, keepdims=True) + EPS) * g


def _ffn_half(x, g_ref, wg_ref, wu_ref, wd_ref):
    hn = _rms(x, g_ref[...]).astype(BF16)
    g = _dot(hn, wg_ref[...])
    u = _dot(hn, wu_ref[...])
    act = (g * _sigmoid(g) * u).astype(BF16)
    return x + 0.5 * _dot(act, wd_ref[...])


def _split3(x):
    a = x.astype(BF16)
    r = x - a.astype(F32)
    b = r.astype(BF16)
    c = (r - b.astype(F32)).astype(BF16)
    return a, b, c


def _dense_in_kernel(x_ref, n1_ref, wg_ref, wu_ref, wd_ref, n2_ref, wp_ref, bf_ref, *refs):
    x1_ref, qf_ref, kf_ref, vf_ref, lfp_ref, u_ref, qs_ref, ks_ref, vs_ref, ksb_ref = refs[-10:]
    x1 = _ffn_half(x_ref[...], n1_ref, wg_ref, wu_ref, wd_ref)
    x1_ref[...] = x1
    h = _rms(x1, n2_ref[...]).astype(BF16)
    z = _dot(h, wp_ref[...])
    scale = HEAD_DIM ** -0.5 * LOG2E
    qf_ref[...] = (z[:, ZQ:ZK] * scale).astype(BF16)
    kf_ref[...] = z[:, ZK:ZV]
    vf_ref[...] = z[:, ZV:ZF]
    zf = z[:, ZF:ZA] + bf_ref[...]
    lane = lax.broadcasted_iota(jnp.int32, zf.shape, 1)
    lfp_ref[...] = jnp.where(lane < H_FOX, -_softplus(-zf), 0.0)
    u_ref[...] = z[:, ZA:ZG] * _sigmoid(z[:, ZG:ZQS])
    qs_ref[...] = (z[:, ZQS:ZKS] * scale).astype(BF16)
    ks = z[:, ZKS:ZVS]
    ks_ref[...] = ks
    ksb_ref[...] = ks.astype(BF16)
    vs_ref[...] = z[:, ZVS:ZEND]


def _const_spec(shape):
    return pl.BlockSpec(shape, lambda *_: (0,) * len(shape), pipeline_mode=pl.Buffered(1))


STACKED = (2, 3, 7, 8)


def _dense_in(x, n1, wg, wu, wd, n2, wp, bfp, tm, layer, depth, stacked):
    n = x.shape[0]
    nt = n // tm
    row = lambda w: pl.BlockSpec((tm, w), lambda i: (i, 0))
    layer_row = lambda w: pl.BlockSpec((tm, w), lambda i: (i + layer * nt, 0))
    widths = (D_MODEL, D_FOX, D_FOX, D_FOX, LANES, C_CONV, D_SB, D_SB, D_SB, D_SB)
    dtypes = (F32, BF16, F32, F32, F32, F32, BF16, F32, F32, BF16)
    rows = [depth * n if k in STACKED else n for k in range(len(widths))]
    prev = () if stacked is None else tuple(stacked)
    n_fixed = 8
    return pl.pallas_call(
        _dense_in_kernel,
        grid=(nt,),
        in_specs=[row(D_MODEL), _const_spec((1, D_MODEL)), _const_spec((D_MODEL, D_FF)),
                  _const_spec((D_MODEL, D_FF)), _const_spec((D_FF, D_MODEL)), _const_spec((1, D_MODEL)),
                  _const_spec((D_MODEL, ZEND)), _const_spec((1, LANES))]
                 + [pl.BlockSpec(memory_space=pl.ANY)] * len(prev),
        out_specs=[layer_row(w) if k in STACKED else row(w) for k, w in enumerate(widths)],
        out_shape=[jax.ShapeDtypeStruct((r, w), d) for r, w, d in zip(rows, widths, dtypes)],
        input_output_aliases={n_fixed + a: k for a, k in enumerate(STACKED)} if prev else {},
        compiler_params=pltpu.CompilerParams(dimension_semantics=("arbitrary",),
                                             vmem_limit_bytes=VMEM_LIMIT),
        name="dense_in",
    )(x, n1, wg, wu, wd, n2, wp, bfp, *prev)


def _dense_out_kernel(x1_ref, yf_ref, yc_ref, ys_ref, wo_ref, n_ref, wg_ref, wu_ref, wd_ref, fn_ref,
                      o_ref, *, final):
    x2 = (x1_ref[...] + _dot(yf_ref[...], wo_ref[0:D_FOX, :])
          + _dot(yc_ref[...], wo_ref[D_FOX:D_FOX + C_CONV, :])
          + _dot(ys_ref[...], wo_ref[D_FOX + C_CONV:, :]))
    x3 = _ffn_half(x2, n_ref, wg_ref, wu_ref, wd_ref)
    o_ref[...] = _rms(x3, fn_ref[...]) if final else x3


def _dense_out(x1, yf, yc, ys, wo, n, wg, wu, wd, fn, tm, final):
    nrow = x1.shape[0]
    row = lambda w: pl.BlockSpec((tm, w), lambda i: (i, 0))
    return pl.pallas_call(
        functools.partial(_dense_out_kernel, final=final),
        grid=(nrow // tm,),
        in_specs=[row(D_MODEL), row(D_FOX), row(C_CONV), row(D_SB), _const_spec((D_MODEL, D_MODEL)),
                  _const_spec((1, D_MODEL)), _const_spec((D_MODEL, D_FF)), _const_spec((D_MODEL, D_FF)),
                  _const_spec((D_FF, D_MODEL)), _const_spec((1, D_MODEL))],
        out_specs=row(D_MODEL),
        out_shape=jax.ShapeDtypeStruct((nrow, D_MODEL), F32),
        compiler_params=pltpu.CompilerParams(dimension_semantics=("arbitrary",),
                                             vmem_limit_bytes=VMEM_LIMIT),
        name="dense_out",
    )(x1, yf, yc, ys, wo, n, wg, wu, wd, fn)


def _prep_kernel(kf_ref, vf_ref, vs_ref, lfp_ref, tri_ref, sel_ref, hsel_ref,
                 kt_ref, vft_ref, vst_ref, bias_max_ref, knorm_max_ref, carry_ref, bmax_ref, kmax_ref):
    i = pl.program_id(1)

    @pl.when(i == 0)
    def _():
        carry_ref[...] = jnp.zeros_like(carry_ref)
        bmax_ref[...] = jnp.full_like(bmax_ref, -jnp.inf)
        kmax_ref[...] = jnp.zeros_like(kmax_ref)

    tri = tri_ref[...]
    cum = carry_ref[...]
    for part in _split3(lfp_ref[0]):
        cum = cum + _dot(tri, part)
    carry_ref[...] = cum[cum.shape[0] - 1:, :]
    bias = -cum * LOG2E
    aug = None
    for s, part in enumerate(_split3(bias)):
        term = _dot(part, sel_ref[s])
        aug = term if aug is None else aug + term
    kb = kf_ref[0].astype(BF16)
    for hp in range(H_FOX // 2):
        base = 2 * LANES * hp
        kt_ref[0, :, base:base + LANES] = kb[:, hp * LANES:(hp + 1) * LANES]
        kt_ref[0, :, base + LANES:base + 2 * LANES] = aug[:, hp * LANES:(hp + 1) * LANES].astype(BF16)
    vft_ref[0] = jnp.transpose(vf_ref[0]).astype(BF16)
    vst_ref[0] = jnp.transpose(vs_ref[0]).astype(BF16)

    ksq = kb.astype(F32)
    ksq = ksq * ksq
    hi = ksq.astype(BF16)
    lo = (ksq - hi.astype(F32)).astype(BF16)
    norm2 = _dot(hi, hsel_ref[...]) + _dot(lo, hsel_ref[...])
    kmax = jnp.maximum(kmax_ref[...], jnp.max(norm2, axis=0, keepdims=True))
    bmax = jnp.maximum(bmax_ref[...], jnp.max(bias, axis=0, keepdims=True))
    kmax_ref[...] = kmax
    bmax_ref[...] = bmax
    knorm_max_ref[0, pl.ds(i, 1), :] = jnp.sqrt(kmax) * NORM_SLACK
    bias_max_ref[0, pl.ds(i, 1), :] = bmax


def _prep(kf, vf, vs, lfp, tp, boff):
    b, tk, _ = lfp.shape
    nblk = tk // tp
    r = jnp.arange(tp)
    tri = (r[None, :] <= r[:, None]).astype(BF16)
    h = jnp.arange(LANES)
    col = jnp.arange(D_FOX)
    sel = jnp.stack([(col[None, :] == ((h // 2) * LANES + (h % 2) * N_SPLIT + s)[:, None])
                     & (h[:, None] < H_FOX) for s in range(N_SPLIT)]).astype(BF16)
    hsel = (col[:, None] // HEAD_DIM == h[None, :]).astype(BF16)
    blk = lambda w: pl.BlockSpec((1, tp, w), lambda bi, i: (bi, i, 0))
    blk_l = lambda w: pl.BlockSpec((1, tp, w), lambda bi, i: (bi + boff, i, 0))
    blk_t = lambda w: pl.BlockSpec((1, w, tp), lambda bi, i: (bi, 0, i))
    per_batch = pl.BlockSpec((1, nblk, LANES), lambda bi, i: (bi, 0, 0))
    return pl.pallas_call(
        _prep_kernel,
        grid=(b, nblk),
        in_specs=[blk_l(D_FOX), blk_l(D_FOX), blk_l(D_SB), blk(LANES), _const_spec((tp, tp)),
                  _const_spec((N_SPLIT, LANES, D_FOX)), _const_spec((D_FOX, LANES))],
        out_specs=[blk(2 * D_FOX), blk_t(D_FOX), blk_t(D_SB), per_batch, per_batch],
        out_shape=[jax.ShapeDtypeStruct((b, tk, 2 * D_FOX), BF16),
                   jax.ShapeDtypeStruct((b, D_FOX, tk), BF16),
                   jax.ShapeDtypeStruct((b, D_SB, tk), BF16),
                   jax.ShapeDtypeStruct((b, nblk, LANES), F32),
                   jax.ShapeDtypeStruct((b, nblk, LANES), F32)],
        scratch_shapes=[pltpu.VMEM((1, LANES), F32)] * 3,
        compiler_params=pltpu.CompilerParams(dimension_semantics=("arbitrary", "arbitrary"),
                                             vmem_limit_bytes=VMEM_LIMIT),
        name="prep",
    )(kf, vf, vs, lfp, tri, sel, hsel)


def _conv_kernel(u_ref, prev_ref, buf_ref, w_ref, b_ref, g_ref, beta_ref, y_ref, xw_ref, sh_ref, *, tt, rows):
    first = pl.program_id(1) == 0
    xw_ref[0:HALO, :] = jnp.where(first, buf_ref[0], prev_ref[0])
    xw_ref[HALO:HALO + tt, :] = u_ref[0]
    off = HALO - (CONV_W - 1)
    for s in range(1, SUBLANES):
        sh_ref[s - 1] = xw_ref[s:s + tt + HALO - SUBLANES, :]
    for r0 in range(0, tt, rows):
        acc = jnp.zeros((rows, C_CONV), F32)
        for j in range(CONV_W):
            s, base = (j + off) % SUBLANES, (j + off) // SUBLANES * SUBLANES
            src = xw_ref if s == 0 else sh_ref.at[s - 1]
            acc = acc + w_ref[j:j + 1, :] * src[base + r0:base + r0 + rows, :]
        y = acc + b_ref[...]
        mu = jnp.mean(y, axis=-1, keepdims=True)
        yc = y - mu
        var = jnp.mean(yc * yc, axis=-1, keepdims=True)
        y = yc * lax.rsqrt(var + EPS) * g_ref[...] + beta_ref[...]
        y_ref[0, r0:r0 + rows, :] = (y * _sigmoid(y)).astype(BF16)


def _conv(u, buf_pad, w, b, g, beta, tt):
    bsz, t, _ = u.shape
    rows = min(tt, 64)
    per = tt // HALO
    return pl.pallas_call(
        functools.partial(_conv_kernel, tt=tt, rows=rows),
        grid=(bsz, t // tt),
        in_specs=[pl.BlockSpec((1, tt, C_CONV), lambda bi, i: (bi, i, 0)),
                  pl.BlockSpec((1, HALO, C_CONV), lambda bi, i: (bi, jnp.maximum(i * per - 1, 0), 0)),
                  pl.BlockSpec((1, HALO, C_CONV), lambda bi, i: (bi, 0, 0)),
                  _const_spec((CONV_W, C_CONV)), _const_spec((1, C_CONV)), _const_spec((1, C_CONV)),
                  _const_spec((1, C_CONV))],
        out_specs=pl.BlockSpec((1, tt, C_CONV), lambda bi, i: (bi, i, 0)),
        out_shape=jax.ShapeDtypeStruct((bsz, t, C_CONV), BF16),
        scratch_shapes=[pltpu.VMEM((HALO + tt, C_CONV), F32),
                        pltpu.VMEM((SUBLANES - 1, HALO + tt - SUBLANES, C_CONV), F32)],
        compiler_params=pltpu.CompilerParams(dimension_semantics=("arbitrary", "arbitrary")),
        name="conv",
    )(u, u, buf_pad, w, b, g, beta)


def _query_heads(q_ref):
    qt = jnp.transpose(q_ref[0].astype(F32))
    row = lax.broadcasted_iota(jnp.int32, qt.shape, 0)
    return row, jnp.where(row < HEAD_DIM, qt, 0.0), jnp.where(row >= HEAD_DIM, qt, 0.0)


def _diag_block(i, tq, tk, past):
    return (past + i * tq + tq - 1) // tk


def _positions(i, tq, tk, past):
    key_pos = lax.broadcasted_iota(jnp.int32, (tk, tq), 0)
    q_pos = past + i * tq + lax.broadcasted_iota(jnp.int32, (tk, tq), 1)
    return key_pos, q_pos


def _merge_heads(out_a, out_b):
    return jnp.transpose(jnp.concatenate([out_a, out_b], axis=0))


def _descend(jd, step, reaches):
    step(jd, True)

    def cond(c):
        j, go = c
        return jnp.logical_and(j >= 0, go)

    def body(c):
        j, _ = c
        step(j, False)
        return j - 1, reaches(j - 1)

    lax.while_loop(cond, body, (jd - 1, reaches(jd - 1)))


def _fox_kernel(bias_max_ref, knorm_max_ref, q_ref, k_ref, vt_ref, o_ref,
                m_ref, l_ref, acc_ref, s_ref, mblk_ref, alpha_ref, go_ref, *, tq, tk, past, nblk):
    bi = pl.program_id(0)
    hp = pl.program_id(1)
    i = pl.program_id(2)
    row, qa, qb = _query_heads(q_ref)
    ones_a = jnp.where(row < N_SPLIT, 1.0, 0.0)
    ones_b = jnp.where(row < 2 * N_SPLIT, 1.0, 0.0) - ones_a
    q_heads = (jnp.concatenate([qa, ones_a], axis=0).astype(BF16),
               jnp.concatenate([qb, ones_b], axis=0).astype(BF16))
    q_norms = [jnp.sqrt(jnp.sum(qh * qh, axis=0, keepdims=True)) * NORM_SLACK for qh in (qa, qb)]
    key_pos, q_pos = _positions(i, tq, tk, past)

    m_ref[...] = jnp.full_like(m_ref, -jnp.inf)
    l_ref[...] = jnp.zeros_like(l_ref)
    acc_ref[...] = jnp.zeros_like(acc_ref)

    def score(j, masked, slot):
        start = pl.multiple_of(j * tk, tk)
        kblk = k_ref[0, pl.ds(start, tk), :]
        scores = [_dot(kblk, q_heads[hh]) for hh in range(2)]
        for hh in range(2):
            s = scores[hh]
            if masked:
                s = jnp.where(start + key_pos <= q_pos, s, -jnp.inf)
            m_prev = m_ref[hh]
            m_new = jnp.maximum(m_prev, jnp.max(s, axis=0, keepdims=True))
            alpha_ref[slot, hh] = jnp.exp2(m_prev - m_new)
            mblk_ref[slot, hh] = m_new
            m_ref[hh] = m_new
            s_ref[slot, hh] = s

    def accumulate(j, slot):
        start = pl.multiple_of(j * tk, tk)
        vt = vt_ref[0, :, pl.ds(start, tk)]
        for hh in range(2):
            p = jnp.exp2(s_ref[slot, hh] - mblk_ref[slot, hh])
            alpha = alpha_ref[slot, hh]
            l_ref[hh] = alpha * l_ref[hh] + jnp.sum(p, axis=0, keepdims=True)
            pv = _dot(vt[hh * HEAD_DIM:(hh + 1) * HEAD_DIM, :], p.astype(BF16))
            acc_ref[hh] = alpha * acc_ref[hh] + pv

    def reaches(j):
        base = (bi * nblk + jnp.maximum(j, 0)) * H_FOX + 2 * hp
        worst = None
        for hh in range(2):
            top = q_norms[hh] * knorm_max_ref[base + hh] + bias_max_ref[base + hh] - m_ref[hh]
            worst = top if worst is None else jnp.maximum(worst, top)
        return jnp.max(worst) >= -SKIP_LOG2

    jd = _diag_block(i, tq, tk, past)
    score(jd, True, 0)
    waiting = lambda j: (jd - j - 1) & 1

    def cond(c):
        j, go = c
        return jnp.logical_and(j >= 0, go)

    def body(c):
        j, _ = c
        for slot in range(2):
            @pl.when(waiting(j) == slot)
            def _():
                score(j, False, 1 - slot)
                go_ref[0] = reaches(j - 1).astype(jnp.int32)
                accumulate(j + 1, slot)
        return j - 1, go_ref[0] != 0

    j_stop, _ = lax.while_loop(cond, body, (jd - 1, reaches(jd - 1)))
    for slot in range(2):
        @pl.when(waiting(j_stop) == slot)
        def _():
            accumulate(j_stop + 1, slot)
    o_ref[0] = _merge_heads(acc_ref[0] / l_ref[0], acc_ref[1] / l_ref[1]).astype(BF16)


def _fox(q, kt, vt, bias_max, knorm_max, past, tq, tk):
    b, t, _ = q.shape
    tkeys = kt.shape[1]
    nblk = tkeys // tk
    assert (tk % tq == 0 and past % tk == 0) or tk == tkeys
    smem = pl.BlockSpec(memory_space=pltpu.SMEM)
    per_head = lambda a: a[:, :, :H_FOX].reshape(-1)
    return pl.pallas_call(
        functools.partial(_fox_kernel, tq=tq, tk=tk, past=past, nblk=nblk),
        grid=(b, H_FOX // 2, t // tq),
        in_specs=[smem, smem,
                  pl.BlockSpec((1, tq, LANES), lambda bi, hp, i: (bi, i, hp)),
                  pl.BlockSpec((1, tkeys, 2 * LANES), lambda bi, hp, i: (bi, 0, hp)),
                  pl.BlockSpec((1, LANES, tkeys), lambda bi, hp, i: (bi, hp, 0))],
        out_specs=pl.BlockSpec((1, tq, LANES), lambda bi, hp, i: (bi, i, hp)),
        out_shape=jax.ShapeDtypeStruct((b, t, D_FOX), BF16),
        scratch_shapes=[pltpu.VMEM((2, 1, tq), F32), pltpu.VMEM((2, 1, tq), F32),
                        pltpu.VMEM((2, HEAD_DIM, tq), F32), pltpu.VMEM((2, 2, tk, tq), F32),
                        pltpu.VMEM((2, 2, 1, tq), F32), pltpu.VMEM((2, 2, 1, tq), F32),
                        pltpu.SMEM((1,), jnp.int32)],
        compiler_params=pltpu.CompilerParams(dimension_semantics=("arbitrary",) * 3,
                                             vmem_limit_bytes=VMEM_LIMIT),
        name="fox",
    )(per_head(bias_max), per_head(knorm_max), q, kt, vt)


def _sb_kernel(q_ref, k_ref, vt_ref, tri_ref, o_ref, carry_ref, acc_ref, *, tq, tk, past):
    i = pl.program_id(1)
    q_heads = []
    for g in range(H_SB // 2):
        _, qa, qb = _query_heads(q_ref.at[:, :, g * LANES:(g + 1) * LANES])
        q_heads += [qa.astype(BF16), qb.astype(BF16)]
    key_pos, q_pos = _positions(i, tq, tk, past)

    carry_ref[...] = jnp.zeros_like(carry_ref)
    acc_ref[...] = jnp.zeros_like(acc_ref)

    def step(j, masked):
        start = pl.multiple_of(j * tk, tk)
        tri2 = tri_ref[...]
        scores = [_dot(k_ref[0, pl.ds(start, tk), (h // 2) * LANES:(h // 2 + 1) * LANES], q_heads[h])
                  for h in range(H_SB)]
        for h in range(H_SB):
            z = scores[h]
            nl = jnp.maximum(z, 0.0) + jnp.log(1.0 + jnp.exp2(-jnp.abs(z))) * LOG2E
            if masked:
                valid = start + key_pos < q_pos
                nl = jnp.where(valid, nl, 0.0)
            hi = nl.astype(BF16)
            lo = (nl - hi.astype(F32)).astype(BF16)
            suffix = _dot(tri2, jnp.concatenate([hi, lo], axis=0))
            later = carry_ref[h]
            a = jnp.exp2(z - nl - suffix - later)
            if masked:
                a = jnp.where(valid, a, 0.0)
            vt = vt_ref[0, h * HEAD_DIM:(h + 1) * HEAD_DIM, pl.ds(start, tk)]
            acc_ref[h] = acc_ref[h] + _dot(vt, a.astype(BF16))
            carry_ref[h] = later + jnp.sum(nl, axis=0, keepdims=True)

    def reaches(j):
        least = carry_ref[0]
        for h in range(1, H_SB):
            least = jnp.minimum(least, carry_ref[h])
        return jnp.min(least) <= SKIP_LOG2

    _descend(_diag_block(i, tq, tk, past), step, reaches)
    for g in range(H_SB // 2):
        o_ref[0, :, g * LANES:(g + 1) * LANES] = _merge_heads(acc_ref[2 * g], acc_ref[2 * g + 1]).astype(BF16)


def _sb(q, k, vt, past, tq, tk):
    b, t, _ = q.shape
    tkeys = k.shape[1]
    assert (tk % tq == 0 and past % tk == 0) or tk == tkeys
    r = jnp.arange(tk)
    tri = (r[None, :] > r[:, None]).astype(BF16)
    tri2 = jnp.concatenate([tri, tri], axis=1)
    per_batch = lambda shape: pl.BlockSpec(shape, lambda bi, i: (bi, 0, 0), pipeline_mode=pl.Buffered(1))
    return pl.pallas_call(
        functools.partial(_sb_kernel, tq=tq, tk=tk, past=past),
        grid=(b, t // tq),
        in_specs=[pl.BlockSpec((1, tq, D_SB), lambda bi, i: (bi, i, 0)),
                  per_batch((1, tkeys, D_SB)), per_batch((1, D_SB, tkeys)), _const_spec((tk, 2 * tk))],
        out_specs=pl.BlockSpec((1, tq, D_SB), lambda bi, i: (bi, i, 0)),
        out_shape=jax.ShapeDtypeStruct((b, t, D_SB), BF16),
        scratch_shapes=[pltpu.VMEM((H_SB, 1, tq), F32), pltpu.VMEM((H_SB, HEAD_DIM, tq), F32)],
        compiler_params=pltpu.CompilerParams(dimension_semantics=("arbitrary",) * 2,
                                             vmem_limit_bytes=VMEM_LIMIT),
        name="sb",
    )(q, k, vt, tri2)


def _pick_tile(n, pref):
    t = min(n, pref)
    while n % t:
        t //= 2
    return t


def _layer(x, past, w, final_norm, layer, depth, stacked):
    b, t, _ = x.shape
    past_k, past_v, past_lf, past_sk, past_sv, conv_buf = past
    p = past_k.shape[1]
    assert t >= CONV_W - 1 and t % HALO == 0
    n = b * t
    tm = _pick_tile(n, 512)

    (x1, qf, kf, vf, lfp, u, qs, ks, vs, ksb) = _dense_in(
        x.reshape(n, D_MODEL), w["n1"], w["wg1"], w["wu1"], w["wd1"], w["n2"], w["wp"], w["bfp"], tm,
        layer, depth, stacked)

    r3 = lambda a: a.reshape(-1, t, a.shape[-1])
    lfp3, u3 = r3(lfp), r3(u)
    if p:
        fill = -(p + t) % LANES
        cat = lambda old, new: jnp.concatenate(
            [old.reshape(b, p, -1).astype(new.dtype), new,
             jnp.zeros((b, fill, new.shape[-1]), new.dtype)], axis=1)
        mine = lambda a: r3(a)[layer * b:(layer + 1) * b]
        kf_all, vf_all, vs_all = cat(past_k, mine(kf)), cat(past_v, mine(vf)), cat(past_sv, mine(vs))
        lf_all = cat(jnp.pad(past_lf, ((0, 0), (0, 0), (0, LANES - H_FOX))), lfp3)
        ks_all = cat(past_sk, r3(ksb))
        boff = 0
        tq_fox = tq_sb = t
        tk_fox = p + t + fill
        tk_sb = LANES if (LANES % t == 0 and p % LANES == 0) else tk_fox
    else:
        kf_all, vf_all, vs_all, lf_all, ks_all = r3(kf), r3(vf), r3(vs), lfp3, r3(ksb)
        boff = layer * b
        tq_sb = tk_sb = _pick_tile(t, 256)
        tq_fox = tk_fox = _pick_tile(t, 512)

    kt, vft, vst, bias_max, knorm_max = _prep(kf_all, vf_all, vs_all, lf_all, tk_fox, boff)
    yf = _fox(r3(qf), kt, vft, bias_max, knorm_max, p, tq_fox, tk_fox)
    ys = _sb(r3(qs), ks_all, vst, p, tq_sb, tk_sb)
    buf_pad = jnp.pad(conv_buf, ((0, 0), (HALO - (CONV_W - 1), 0), (0, 0)))
    yc = _conv(u3, buf_pad, w["conv_w"], w["conv_b"], w["ln_g"], w["ln_b"], _pick_tile(t, 256))

    flat = lambda a: a.reshape(n, a.shape[-1])
    xo = _dense_out(x1, flat(yf), flat(yc), flat(ys), w["wo"], w["n3"], w["wg2"], w["wu2"], w["wd2"],
                    final_norm, tm, layer == depth - 1)
    return xo.reshape(b, t, D_MODEL), (kf, vf, ks, vs), (lfp3[..., :H_FOX], u3[:, t - (CONV_W - 1):, :])


def _layer_weights(l, ffn_norm, ffn_gate, ffn_up, ffn_down, mix_norm, w_in, b_forget,
                   conv_w, conv_b, conv_ln_g, conv_ln_b, w_out):
    wi = w_in[l]
    off_f = 3 * D_FOX
    off_glu = off_f + H_FOX
    off_qc = off_glu + 2 * C_CONV
    wp = jnp.concatenate([wi[:, :off_f], jnp.pad(wi[:, off_f:off_glu], ((0, 0), (0, LANES - H_FOX))),
                          wi[:, off_glu:off_qc], wi[:, off_qc:]], axis=1).astype(BF16)
    row = lambda a: a.reshape(1, -1).astype(F32)
    return dict(
        n1=row(ffn_norm[l, 0]), wg1=ffn_gate[l, 0].astype(BF16), wu1=ffn_up[l, 0].astype(BF16),
        wd1=ffn_down[l, 0].astype(BF16), n2=row(mix_norm[l]), wp=wp,
        bfp=jnp.pad(row(b_forget[l]), ((0, 0), (0, LANES - H_FOX))),
        conv_w=conv_w[l], conv_b=row(conv_b[l]), ln_g=row(conv_ln_g[l]), ln_b=row(conv_ln_b[l]),
        wo=w_out[l].astype(BF16), n3=row(ffn_norm[l, 1]), wg2=ffn_gate[l, 1].astype(BF16),
        wu2=ffn_up[l, 1].astype(BF16), wd2=ffn_down[l, 1].astype(BF16))


def kernel(x_prompt, x_sample, cache_fox_k, cache_fox_v, cache_fox_logf, cache_sb_k, cache_sb_v, state_conv,
           ffn_norm, ffn_gate, ffn_up, ffn_down, mix_norm, w_in, b_forget, conv_w, conv_b, conv_ln_g,
           conv_ln_b, w_out, final_norm):
    depth = w_in.shape[0]
    bp = x_prompt.shape[0]
    dt = x_prompt.dtype
    empty = (jnp.zeros((bp, 0, H_FOX, HEAD_DIM), dt), jnp.zeros((bp, 0, H_FOX, HEAD_DIM), dt),
             jnp.zeros((bp, 0, H_FOX), dt), jnp.zeros((bp, 0, H_SB, HEAD_DIM), dt),
             jnp.zeros((bp, 0, H_SB, HEAD_DIM), dt), jnp.zeros((bp, CONV_W - 1, C_CONV), dt))
    fn = final_norm.reshape(1, -1).astype(F32)
    xp, xs = x_prompt, x_sample
    stacked_p = stacked_s = None
    small_p, small_s = [], []
    for l in range(depth):
        w = _layer_weights(l, ffn_norm, ffn_gate, ffn_up, ffn_down, mix_norm, w_in, b_forget,
                           conv_w, conv_b, conv_ln_g, conv_ln_b, w_out)
        xp, stacked_p, sp = _layer(xp, empty, w, fn, l, depth, stacked_p)
        cache = (cache_fox_k[l], cache_fox_v[l], cache_fox_logf[l], cache_sb_k[l], cache_sb_v[l],
                 state_conv[l])
        xs, stacked_s, ss = _layer(xs, cache, w, fn, l, depth, stacked_s)
        small_p.append(sp)
        small_s.append(ss)

    def outputs(x, stacked, small):
        b, t, _ = x.shape
        heads = lambda a: a.reshape(depth, b, t, a.shape[-1] // HEAD_DIM, HEAD_DIM)
        kf, vf, ks, vs = stacked
        stack = lambda k: jnp.stack([it[k] for it in small], 0)
        return heads(kf), heads(vf), stack(0), heads(ks), heads(vs), stack(1)

    return (xp, xs) + outputs(xp, stacked_p, small_p) + outputs(xs, stacked_s, small_s)
```

```python
import functools

import jax
import jax.numpy as jnp
from jax import lax
from jax.experimental import pallas as pl
from jax.experimental.pallas import tpu as pltpu

D_MODEL = 1024
HEAD_DIM = 64
D_FOX = 512
C_CONV = 256
D_SB = 256
H_FOX = 8
H_SB = 4
CONV_W = 31
D_FF = 2816
EPS = 1e-6

LANES = 128
SUBLANES = 8
HALO = 32
N_SPLIT = 3
VMEM_LIMIT = 56 * 1024 * 1024
SKIP_LOG = 110.0
LOG2E = 1.4426950408889634
SKIP_LOG2 = SKIP_LOG * LOG2E
NORM_SLACK = 1.01

ZQ, ZK, ZV, ZF, ZA, ZG, ZQS, ZKS, ZVS, ZEND = 0, 512, 1024, 1536, 1664, 1920, 2176, 2432, 2688, 2944

F32 = jnp.float32
BF16 = jnp.bfloat16


def _dot(a, b):
    return jnp.dot(a, b, preferred_element_type=F32)


def _sigmoid(x):
    return 1.0 / (1.0 + jnp.exp(-x))


def _softplus(x):
    return jnp.maximum(x, 0.0) + jnp.log1p(jnp.exp(-jnp.abs(x)))


def _rms(x, g):
    return x * lax.rsqrt(jnp.mean(x * x, axis=-1, keepdims=True) + EPS) * g


def _ffn_half(x, g_ref, wg_ref, wu_ref, wd_ref):
    hn = _rms(x, g_ref[...]).astype(BF16)
    g = _dot(hn, wg_ref[...])
    u = _dot(hn, wu_ref[...])
    act = (g * _sigmoid(g) * u).astype(BF16)
    return x + 0.5 * _dot(act, wd_ref[...])


def _split3(x):
    a = x.astype(BF16)
    r = x - a.astype(F32)
    b = r.astype(BF16)
    c = (r - b.astype(F32)).astype(BF16)
    return a, b, c


def _store_heads(o_ref, x):
    heads = x.shape[1] // HEAD_DIM
    for h in range(heads):
        o_ref[pl.ds(h, x.shape[0], stride=heads), :] = x[:, h * HEAD_DIM:(h + 1) * HEAD_DIM]


def _dense_in_kernel(x_ref, n1_ref, wg_ref, wu_ref, wd_ref, n2_ref, wp_ref, bf_ref, *refs):
    (x1_ref, qf_ref, kf_ref, vf_ref, lfp_ref, u_ref, qs_ref, ks_ref, vs_ref,
     kfb_ref, vfb_ref, ksb_ref, vsb_ref) = refs[-13:]
    x1 = _ffn_half(x_ref[...], n1_ref, wg_ref, wu_ref, wd_ref)
    x1_ref[...] = x1
    h = _rms(x1, n2_ref[...]).astype(BF16)
    z = _dot(h, wp_ref[...])
    scale = HEAD_DIM ** -0.5 * LOG2E
    qf_ref[...] = (z[:, ZQ:ZK] * scale).astype(BF16)
    kf, vf = z[:, ZK:ZV], z[:, ZV:ZF]
    _store_heads(kf_ref, kf)
    _store_heads(vf_ref, vf)
    kfb_ref[...] = kf.astype(BF16)
    vfb_ref[...] = vf.astype(BF16)
    zf = z[:, ZF:ZA] + bf_ref[...]
    lane = lax.broadcasted_iota(jnp.int32, zf.shape, 1)
    lfp_ref[...] = jnp.where(lane < H_FOX, -_softplus(-zf), 0.0)
    u_ref[...] = z[:, ZA:ZG] * _sigmoid(z[:, ZG:ZQS])
    qs_ref[...] = (z[:, ZQS:ZKS] * scale).astype(BF16)
    ks, vs = z[:, ZKS:ZVS], z[:, ZVS:ZEND]
    _store_heads(ks_ref, ks)
    _store_heads(vs_ref, vs)
    ksb_ref[...] = ks.astype(BF16)
    vsb_ref[...] = vs.astype(BF16)


def _const_spec(shape):
    return pl.BlockSpec(shape, lambda *_: (0,) * len(shape), pipeline_mode=pl.Buffered(1))


STACKED = (2, 3, 7, 8)


def _dense_in(x, n1, wg, wu, wd, n2, wp, bfp, tm, layer, depth, stacked):
    n = x.shape[0]
    nt = n // tm
    widths = (D_MODEL, D_FOX, D_FOX, D_FOX, LANES, C_CONV, D_SB, D_SB, D_SB, D_FOX, D_FOX, D_SB, D_SB)
    dtypes = (F32, BF16, F32, F32, F32, F32, BF16, F32, F32, BF16, BF16, BF16, BF16)
    specs, shapes = [], []
    for k, (w, d) in enumerate(zip(widths, dtypes)):
        if k in STACKED:
            heads = w // HEAD_DIM
            specs.append(pl.BlockSpec((tm * heads, HEAD_DIM), lambda i: (i + layer * nt, 0)))
            shapes.append(jax.ShapeDtypeStruct((depth * n * heads, HEAD_DIM), d))
        else:
            specs.append(pl.BlockSpec((tm, w), lambda i: (i, 0)))
            shapes.append(jax.ShapeDtypeStruct((n, w), d))
    prev = () if stacked is None else tuple(stacked)
    n_fixed = 8
    return pl.pallas_call(
        _dense_in_kernel,
        grid=(nt,),
        in_specs=[pl.BlockSpec((tm, D_MODEL), lambda i: (i, 0)), _const_spec((1, D_MODEL)),
                  _const_spec((D_MODEL, D_FF)), _const_spec((D_MODEL, D_FF)), _const_spec((D_FF, D_MODEL)),
                  _const_spec((1, D_MODEL)), _const_spec((D_MODEL, ZEND)), _const_spec((1, LANES))]
                 + [pl.BlockSpec(memory_space=pl.ANY)] * len(prev),
        out_specs=specs,
        out_shape=shapes,
        input_output_aliases={n_fixed + a: k for a, k in enumerate(STACKED)} if prev else {},
        compiler_params=pltpu.CompilerParams(dimension_semantics=("arbitrary",),
                                             vmem_limit_bytes=VMEM_LIMIT),
        name="dense_in",
    )(x, n1, wg, wu, wd, n2, wp, bfp, *prev)


def _dense_out_kernel(x1_ref, yf_ref, yc_ref, ys_ref, wo_ref, n_ref, wg_ref, wu_ref, wd_ref, fn_ref,
                      o_ref, *, final):
    x2 = (x1_ref[...] + _dot(yf_ref[...], wo_ref[0:D_FOX, :])
          + _dot(yc_ref[...], wo_ref[D_FOX:D_FOX + C_CONV, :])
          + _dot(ys_ref[...], wo_ref[D_FOX + C_CONV:, :]))
    x3 = _ffn_half(x2, n_ref, wg_ref, wu_ref, wd_ref)
    o_ref[...] = _rms(x3, fn_ref[...]) if final else x3


def _dense_out(x1, yf, yc, ys, wo, n, wg, wu, wd, fn, tm, final):
    nrow = x1.shape[0]
    row = lambda w: pl.BlockSpec((tm, w), lambda i: (i, 0))
    return pl.pallas_call(
        functools.partial(_dense_out_kernel, final=final),
        grid=(nrow // tm,),
        in_specs=[row(D_MODEL), row(D_FOX), row(C_CONV), row(D_SB), _const_spec((D_MODEL, D_MODEL)),
                  _const_spec((1, D_MODEL)), _const_spec((D_MODEL, D_FF)), _const_spec((D_MODEL, D_FF)),
                  _const_spec((D_FF, D_MODEL)), _const_spec((1, D_MODEL))],
        out_specs=row(D_MODEL),
        out_shape=jax.ShapeDtypeStruct((nrow, D_MODEL), F32),
        compiler_params=pltpu.CompilerParams(dimension_semantics=("arbitrary",),
                                             vmem_limit_bytes=VMEM_LIMIT),
        name="dense_out",
    )(x1, yf, yc, ys, wo, n, wg, wu, wd, fn)


def _prep_kernel(kf_ref, vf_ref, vs_ref, lfp_ref, tri_ref, sel_ref, hsel_ref,
                 kt_ref, vft_ref, vst_ref, bias_max_ref, knorm_max_ref, carry_ref, bmax_ref, kmax_ref):
    i = pl.program_id(1)

    @pl.when(i == 0)
    def _():
        carry_ref[...] = jnp.zeros_like(carry_ref)
        bmax_ref[...] = jnp.full_like(bmax_ref, -jnp.inf)
        kmax_ref[...] = jnp.zeros_like(kmax_ref)

    tri = tri_ref[...]
    cum = carry_ref[...]
    for part in _split3(lfp_ref[0]):
        cum = cum + _dot(tri, part)
    carry_ref[...] = cum[cum.shape[0] - 1:, :]
    bias = -cum * LOG2E
    aug = None
    for s, part in enumerate(_split3(bias)):
        term = _dot(part, sel_ref[s])
        aug = term if aug is None else aug + term
    kb = kf_ref[0]
    for hp in range(H_FOX // 2):
        base = 2 * LANES * hp
        kt_ref[0, :, base:base + LANES] = kb[:, hp * LANES:(hp + 1) * LANES]
        kt_ref[0, :, base + LANES:base + 2 * LANES] = aug[:, hp * LANES:(hp + 1) * LANES].astype(BF16)
    vft_ref[0] = jnp.transpose(vf_ref[0].astype(F32)).astype(BF16)
    vst_ref[0] = jnp.transpose(vs_ref[0].astype(F32)).astype(BF16)

    ksq = kb.astype(F32)
    ksq = ksq * ksq
    hi = ksq.astype(BF16)
    lo = (ksq - hi.astype(F32)).astype(BF16)
    norm2 = _dot(hi, hsel_ref[...]) + _dot(lo, hsel_ref[...])
    kmax = jnp.maximum(kmax_ref[...], jnp.max(norm2, axis=0, keepdims=True))
    bmax = jnp.maximum(bmax_ref[...], jnp.max(bias, axis=0, keepdims=True))
    kmax_ref[...] = kmax
    bmax_ref[...] = bmax
    knorm_max_ref[0, pl.ds(i, 1), :] = jnp.sqrt(kmax) * NORM_SLACK
    bias_max_ref[0, pl.ds(i, 1), :] = bmax


def _prep(kf, vf, vs, lfp, tp):
    b, tk, _ = lfp.shape
    nblk = tk // tp
    r = jnp.arange(tp)
    tri = (r[None, :] <= r[:, None]).astype(BF16)
    h = jnp.arange(LANES)
    col = jnp.arange(D_FOX)
    sel = jnp.stack([(col[None, :] == ((h // 2) * LANES + (h % 2) * N_SPLIT + s)[:, None])
                     & (h[:, None] < H_FOX) for s in range(N_SPLIT)]).astype(BF16)
    hsel = (col[:, None] // HEAD_DIM == h[None, :]).astype(BF16)
    blk = lambda w: pl.BlockSpec((1, tp, w), lambda bi, i: (bi, i, 0))
    blk_t = lambda w: pl.BlockSpec((1, w, tp), lambda bi, i: (bi, 0, i))
    per_batch = pl.BlockSpec((1, nblk, LANES), lambda bi, i: (bi, 0, 0))
    return pl.pallas_call(
        _prep_kernel,
        grid=(b, nblk),
        in_specs=[blk(D_FOX), blk(D_FOX), blk(D_SB), blk(LANES), _const_spec((tp, tp)),
                  _const_spec((N_SPLIT, LANES, D_FOX)), _const_spec((D_FOX, LANES))],
        out_specs=[blk(2 * D_FOX), blk_t(D_FOX), blk_t(D_SB), per_batch, per_batch],
        out_shape=[jax.ShapeDtypeStruct((b, tk, 2 * D_FOX), BF16),
                   jax.ShapeDtypeStruct((b, D_FOX, tk), BF16),
                   jax.ShapeDtypeStruct((b, D_SB, tk), BF16),
                   jax.ShapeDtypeStruct((b, nblk, LANES), F32),
                   jax.ShapeDtypeStruct((b, nblk, LANES), F32)],
        scratch_shapes=[pltpu.VMEM((1, LANES), F32)] * 3,
        compiler_params=pltpu.CompilerParams(dimension_semantics=("arbitrary", "arbitrary"),
                                             vmem_limit_bytes=VMEM_LIMIT),
        name="prep",
    )(kf, vf, vs, lfp, tri, sel, hsel)


def _conv_kernel(u_ref, prev_ref, buf_ref, w_ref, b_ref, g_ref, beta_ref, y_ref, xw_ref, sh_ref, *, tt, rows):
    first = pl.program_id(1) == 0
    xw_ref[0:HALO, :] = jnp.where(first, buf_ref[0], prev_ref[0])
    xw_ref[HALO:HALO + tt, :] = u_ref[0]
    off = HALO - (CONV_W - 1)
    for s in range(1, SUBLANES):
        sh_ref[s - 1] = xw_ref[s:s + tt + HALO - SUBLANES, :]
    for r0 in range(0, tt, rows):
        acc = jnp.zeros((rows, C_CONV), F32)
        for j in range(CONV_W):
            s, base = (j + off) % SUBLANES, (j + off) // SUBLANES * SUBLANES
            src = xw_ref if s == 0 else sh_ref.at[s - 1]
            acc = acc + w_ref[j:j + 1, :] * src[base + r0:base + r0 + rows, :]
        y = acc + b_ref[...]
        mu = jnp.mean(y, axis=-1, keepdims=True)
        yc = y - mu
        var = jnp.mean(yc * yc, axis=-1, keepdims=True)
        y = yc * lax.rsqrt(var + EPS) * g_ref[...] + beta_ref[...]
        y_ref[0, r0:r0 + rows, :] = (y * _sigmoid(y)).astype(BF16)


def _conv(u, buf_pad, w, b, g, beta, tt):
    bsz, t, _ = u.shape
    rows = min(tt, 64)
    per = tt // HALO
    return pl.pallas_call(
        functools.partial(_conv_kernel, tt=tt, rows=rows),
        grid=(bsz, t // tt),
        in_specs=[pl.BlockSpec((1, tt, C_CONV), lambda bi, i: (bi, i, 0)),
                  pl.BlockSpec((1, HALO, C_CONV), lambda bi, i: (bi, jnp.maximum(i * per - 1, 0), 0)),
                  pl.BlockSpec((1, HALO, C_CONV), lambda bi, i: (bi, 0, 0)),
                  _const_spec((CONV_W, C_CONV)), _const_spec((1, C_CONV)), _const_spec((1, C_CONV)),
                  _const_spec((1, C_CONV))],
        out_specs=pl.BlockSpec((1, tt, C_CONV), lambda bi, i: (bi, i, 0)),
        out_shape=jax.ShapeDtypeStruct((bsz, t, C_CONV), BF16),
        scratch_shapes=[pltpu.VMEM((HALO + tt, C_CONV), F32),
                        pltpu.VMEM((SUBLANES - 1, HALO + tt - SUBLANES, C_CONV), F32)],
        compiler_params=pltpu.CompilerParams(dimension_semantics=("arbitrary", "arbitrary")),
        name="conv",
    )(u, u, buf_pad, w, b, g, beta)


def _query_heads(q_ref):
    qt = jnp.transpose(q_ref[0].astype(F32))
    row = lax.broadcasted_iota(jnp.int32, qt.shape, 0)
    return row, jnp.where(row < HEAD_DIM, qt, 0.0), jnp.where(row >= HEAD_DIM, qt, 0.0)


def _diag_block(i, tq, tk, past):
    return (past + i * tq + tq - 1) // tk


def _positions(i, tq, tk, past):
    key_pos = lax.broadcasted_iota(jnp.int32, (tk, tq), 0)
    q_pos = past + i * tq + lax.broadcasted_iota(jnp.int32, (tk, tq), 1)
    return key_pos, q_pos


def _merge_heads(out_a, out_b):
    return jnp.transpose(jnp.concatenate([out_a, out_b], axis=0))


def _descend(jd, step, reaches):
    step(jd, True)

    def cond(c):
        j, go = c
        return jnp.logical_and(j >= 0, go)

    def body(c):
        j, _ = c
        step(j, False)
        return j - 1, reaches(j - 1)

    lax.while_loop(cond, body, (jd - 1, reaches(jd - 1)))


def _fox_kernel(bias_max_ref, knorm_max_ref, q_ref, k_ref, vt_ref, o_ref,
                m_ref, l_ref, acc_ref, s_ref, mblk_ref, alpha_ref, go_ref, *, tq, tk, past, nblk):
    bi = pl.program_id(0)
    hp = pl.program_id(1)
    i = pl.program_id(2)
    row, qa, qb = _query_heads(q_ref)
    ones_a = jnp.where(row < N_SPLIT, 1.0, 0.0)
    ones_b = jnp.where(row < 2 * N_SPLIT, 1.0, 0.0) - ones_a
    q_heads = (jnp.concatenate([qa, ones_a], axis=0).astype(BF16),
               jnp.concatenate([qb, ones_b], axis=0).astype(BF16))
    q_norms = [jnp.sqrt(jnp.sum(qh * qh, axis=0, keepdims=True)) * NORM_SLACK for qh in (qa, qb)]
    key_pos, q_pos = _positions(i, tq, tk, past)

    m_ref[...] = jnp.full_like(m_ref, -jnp.inf)
    l_ref[...] = jnp.zeros_like(l_ref)
    acc_ref[...] = jnp.zeros_like(acc_ref)

    def score(j, masked, slot):
        start = pl.multiple_of(j * tk, tk)
        kblk = k_ref[0, pl.ds(start, tk), :]
        scores = [_dot(kblk, q_heads[hh]) for hh in range(2)]
        for hh in range(2):
            s = scores[hh]
            if masked:
                s = jnp.where(start + key_pos <= q_pos, s, -jnp.inf)
            m_prev = m_ref[hh]
            m_new = jnp.maximum(m_prev, jnp.max(s, axis=0, keepdims=True))
            alpha_ref[slot, hh] = jnp.exp2(m_prev - m_new)
            mblk_ref[slot, hh] = m_new
            m_ref[hh] = m_new
            s_ref[slot, hh] = s

    def accumulate(j, slot):
        start = pl.multiple_of(j * tk, tk)
        vt = vt_ref[0, :, pl.ds(start, tk)]
        for hh in range(2):
            p = jnp.exp2(s_ref[slot, hh] - mblk_ref[slot, hh])
            alpha = alpha_ref[slot, hh]
            l_ref[hh] = alpha * l_ref[hh] + jnp.sum(p, axis=0, keepdims=True)
            pv = _dot(vt[hh * HEAD_DIM:(hh + 1) * HEAD_DIM, :], p.astype(BF16))
            acc_ref[hh] = alpha * acc_ref[hh] + pv

    def reaches(j):
        base = (bi * nblk + jnp.maximum(j, 0)) * H_FOX + 2 * hp
        worst = None
        for hh in range(2):
            top = q_norms[hh] * knorm_max_ref[base + hh] + bias_max_ref[base + hh] - m_ref[hh]
            worst = top if worst is None else jnp.maximum(worst, top)
        return jnp.max(worst) >= -SKIP_LOG2

    jd = _diag_block(i, tq, tk, past)
    score(jd, True, 0)
    waiting = lambda j: (jd - j - 1) & 1

    def cond(c):
        j, go = c
        return jnp.logical_and(j >= 0, go)

    def body(c):
        j, _ = c
        for slot in range(2):
            @pl.when(waiting(j) == slot)
            def _():
                score(j, False, 1 - slot)
                go_ref[0] = reaches(j - 1).astype(jnp.int32)
                accumulate(j + 1, slot)
        return j - 1, go_ref[0] != 0

    j_stop, _ = lax.while_loop(cond, body, (jd - 1, reaches(jd - 1)))
    for slot in range(2):
        @pl.when(waiting(j_stop) == slot)
        def _():
            accumulate(j_stop + 1, slot)
    o_ref[0] = _merge_heads(acc_ref[0] / l_ref[0], acc_ref[1] / l_ref[1]).astype(BF16)


def _fox(q, kt, vt, bias_max, knorm_max, past, tq, tk):
    b, t, _ = q.shape
    tkeys = kt.shape[1]
    nblk = tkeys // tk
    assert (tk % tq == 0 and past % tk == 0) or tk == tkeys
    smem = pl.BlockSpec(memory_space=pltpu.SMEM)
    per_head = lambda a: a[:, :, :H_FOX].reshape(-1)
    return pl.pallas_call(
        functools.partial(_fox_kernel, tq=tq, tk=tk, past=past, nblk=nblk),
        grid=(b, H_FOX // 2, t // tq),
        in_specs=[smem, smem,
                  pl.BlockSpec((1, tq, LANES), lambda bi, hp, i: (bi, i, hp)),
                  pl.BlockSpec((1, tkeys, 2 * LANES), lambda bi, hp, i: (bi, 0, hp)),
                  pl.BlockSpec((1, LANES, tkeys), lambda bi, hp, i: (bi, hp, 0))],
        out_specs=pl.BlockSpec((1, tq, LANES), lambda bi, hp, i: (bi, i, hp)),
        out_shape=jax.ShapeDtypeStruct((b, t, D_FOX), BF16),
        scratch_shapes=[pltpu.VMEM((2, 1, tq), F32), pltpu.VMEM((2, 1, tq), F32),
                        pltpu.VMEM((2, HEAD_DIM, tq), F32), pltpu.VMEM((2, 2, tk, tq), F32),
                        pltpu.VMEM((2, 2, 1, tq), F32), pltpu.VMEM((2, 2, 1, tq), F32),
                        pltpu.SMEM((1,), jnp.int32)],
        compiler_params=pltpu.CompilerParams(dimension_semantics=("arbitrary",) * 3,
                                             vmem_limit_bytes=VMEM_LIMIT),
        name="fox",
    )(per_head(bias_max), per_head(knorm_max), q, kt, vt)


def _sb_kernel(q_ref, k_ref, vt_ref, tri_ref, o_ref, carry_ref, acc_ref, *, tq, tk, past):
    i = pl.program_id(1)
    q_heads = []
    for g in range(H_SB // 2):
        _, qa, qb = _query_heads(q_ref.at[:, :, g * LANES:(g + 1) * LANES])
        q_heads += [qa.astype(BF16), qb.astype(BF16)]
    key_pos, q_pos = _positions(i, tq, tk, past)

    carry_ref[...] = jnp.zeros_like(carry_ref)
    acc_ref[...] = jnp.zeros_like(acc_ref)

    def step(j, masked):
        start = pl.multiple_of(j * tk, tk)
        tri2 = tri_ref[...]
        scores = [_dot(k_ref[0, pl.ds(start, tk), (h // 2) * LANES:(h // 2 + 1) * LANES], q_heads[h])
                  for h in range(H_SB)]
        for h in range(H_SB):
            z = scores[h]
            nl = jnp.maximum(z, 0.0) + jnp.log(1.0 + jnp.exp2(-jnp.abs(z))) * LOG2E
            if masked:
                valid = start + key_pos < q_pos
                nl = jnp.where(valid, nl, 0.0)
            hi = nl.astype(BF16)
            lo = (nl - hi.astype(F32)).astype(BF16)
            suffix = _dot(tri2, jnp.concatenate([hi, lo], axis=0))
            later = carry_ref[h]
            a = jnp.exp2(z - nl - suffix - later)
            if masked:
                a = jnp.where(valid, a, 0.0)
            vt = vt_ref[0, h * HEAD_DIM:(h + 1) * HEAD_DIM, pl.ds(start, tk)]
            acc_ref[h] = acc_ref[h] + _dot(vt, a.astype(BF16))
            carry_ref[h] = later + jnp.sum(nl, axis=0, keepdims=True)

    def reaches(j):
        least = carry_ref[0]
        for h in range(1, H_SB):
            least = jnp.minimum(least, carry_ref[h])
        return jnp.min(least) <= SKIP_LOG2

    _descend(_diag_block(i, tq, tk, past), step, reaches)
    for g in range(H_SB // 2):
        o_ref[0, :, g * LANES:(g + 1) * LANES] = _merge_heads(acc_ref[2 * g], acc_ref[2 * g + 1]).astype(BF16)


def _sb(q, k, vt, past, tq, tk):
    b, t, _ = q.shape
    tkeys = k.shape[1]
    assert (tk % tq == 0 and past % tk == 0) or tk == tkeys
    r = jnp.arange(tk)
    tri = (r[None, :] > r[:, None]).astype(BF16)
    tri2 = jnp.concatenate([tri, tri], axis=1)
    per_batch = lambda shape: pl.BlockSpec(shape, lambda bi, i: (bi, 0, 0), pipeline_mode=pl.Buffered(1))
    return pl.pallas_call(
        functools.partial(_sb_kernel, tq=tq, tk=tk, past=past),
        grid=(b, t // tq),
        in_specs=[pl.BlockSpec((1, tq, D_SB), lambda bi, i: (bi, i, 0)),
                  per_batch((1, tkeys, D_SB)), per_batch((1, D_SB, tkeys)), _const_spec((tk, 2 * tk))],
        out_specs=pl.BlockSpec((1, tq, D_SB), lambda bi, i: (bi, i, 0)),
        out_shape=jax.ShapeDtypeStruct((b, t, D_SB), BF16),
        scratch_shapes=[pltpu.VMEM((H_SB, 1, tq), F32), pltpu.VMEM((H_SB, HEAD_DIM, tq), F32)],
        compiler_params=pltpu.CompilerParams(dimension_semantics=("arbitrary",) * 2,
                                             vmem_limit_bytes=VMEM_LIMIT),
        name="sb",
    )(q, k, vt, tri2)


def _pick_tile(n, pref):
    t = min(n, pref)
    while n % t:
        t //= 2
    return t


def _layer(x, past, w, final_norm, layer, depth, stacked):
    b, t, _ = x.shape
    past_k, past_v, past_lf, past_sk, past_sv, conv_buf = past
    p = past_k.shape[1]
    assert t >= CONV_W - 1 and t % HALO == 0
    n = b * t
    tm = _pick_tile(n, 512)

    (x1, qf, kf, vf, lfp, u, qs, ks, vs, kfb, vfb, ksb, vsb) = _dense_in(
        x.reshape(n, D_MODEL), w["n1"], w["wg1"], w["wu1"], w["wd1"], w["n2"], w["wp"], w["bfp"], tm,
        layer, depth, stacked)

    r3 = lambda a: a.reshape(-1, t, a.shape[-1])
    lfp3, u3 = r3(lfp), r3(u)
    if p:
        fill = -(p + t) % LANES
        cat = lambda old, new: jnp.concatenate(
            [old.reshape(b, p, -1).astype(new.dtype), new,
             jnp.zeros((b, fill, new.shape[-1]), new.dtype)], axis=1)
        kf_all, vf_all, vs_all = cat(past_k, r3(kfb)), cat(past_v, r3(vfb)), cat(past_sv, r3(vsb))
        lf_all = cat(jnp.pad(past_lf, ((0, 0), (0, 0), (0, LANES - H_FOX))), lfp3)
        ks_all = cat(past_sk, r3(ksb))
        tq_fox = tq_sb = t
        tk_fox = p + t + fill
        tk_sb = LANES if (LANES % t == 0 and p % LANES == 0) else tk_fox
    else:
        kf_all, vf_all, vs_all, lf_all, ks_all = r3(kfb), r3(vfb), r3(vsb), lfp3, r3(ksb)
        tq_sb = tk_sb = _pick_tile(t, 256)
        tq_fox = tk_fox = _pick_tile(t, 512)

    kt, vft, vst, bias_max, knorm_max = _prep(kf_all, vf_all, vs_all, lf_all, tk_fox)
    yf = _fox(r3(qf), kt, vft, bias_max, knorm_max, p, tq_fox, tk_fox)
    ys = _sb(r3(qs), ks_all, vst, p, tq_sb, tk_sb)
    buf_pad = jnp.pad(conv_buf, ((0, 0), (HALO - (CONV_W - 1), 0), (0, 0)))
    yc = _conv(u3, buf_pad, w["conv_w"], w["conv_b"], w["ln_g"], w["ln_b"], _pick_tile(t, 256))

    flat = lambda a: a.reshape(n, a.shape[-1])
    xo = _dense_out(x1, flat(yf), flat(yc), flat(ys), w["wo"], w["n3"], w["wg2"], w["wu2"], w["wd2"],
                    final_norm, tm, layer == depth - 1)
    return xo.reshape(b, t, D_MODEL), (kf, vf, ks, vs), (lfp3[..., :H_FOX], u3[:, t - (CONV_W - 1):, :])


def _layer_weights(l, ffn_norm, ffn_gate, ffn_up, ffn_down, mix_norm, w_in, b_forget,
                   conv_w, conv_b, conv_ln_g, conv_ln_b, w_out):
    wi = w_in[l]
    off_f = 3 * D_FOX
    off_glu = off_f + H_FOX
    off_qc = off_glu + 2 * C_CONV
    wp = jnp.concatenate([wi[:, :off_f], jnp.pad(wi[:, off_f:off_glu], ((0, 0), (0, LANES - H_FOX))),
                          wi[:, off_glu:off_qc], wi[:, off_qc:]], axis=1).astype(BF16)
    row = lambda a: a.reshape(1, -1).astype(F32)
    return dict(
        n1=row(ffn_norm[l, 0]), wg1=ffn_gate[l, 0].astype(BF16), wu1=ffn_up[l, 0].astype(BF16),
        wd1=ffn_down[l, 0].astype(BF16), n2=row(mix_norm[l]), wp=wp,
        bfp=jnp.pad(row(b_forget[l]), ((0, 0), (0, LANES - H_FOX))),
        conv_w=conv_w[l], conv_b=row(conv_b[l]), ln_g=row(conv_ln_g[l]), ln_b=row(conv_ln_b[l]),
        wo=w_out[l].astype(BF16), n3=row(ffn_norm[l, 1]), wg2=ffn_gate[l, 1].astype(BF16),
        wu2=ffn_up[l, 1].astype(BF16), wd2=ffn_down[l, 1].astype(BF16))


def kernel(x_prompt, x_sample, cache_fox_k, cache_fox_v, cache_fox_logf, cache_sb_k, cache_sb_v, state_conv,
           ffn_norm, ffn_gate, ffn_up, ffn_down, mix_norm, w_in, b_forget, conv_w, conv_b, conv_ln_g,
           conv_ln_b, w_out, final_norm):
    depth = w_in.shape[0]
    bp = x_prompt.shape[0]
    dt = x_prompt.dtype
    empty = (jnp.zeros((bp, 0, H_FOX, HEAD_DIM), dt), jnp.zeros((bp, 0, H_FOX, HEAD_DIM), dt),
             jnp.zeros((bp, 0, H_FOX), dt), jnp.zeros((bp, 0, H_SB, HEAD_DIM), dt),
             jnp.zeros((bp, 0, H_SB, HEAD_DIM), dt), jnp.zeros((bp, CONV_W - 1, C_CONV), dt))
    fn = final_norm.reshape(1, -1).astype(F32)
    xp, xs = x_prompt, x_sample
    stacked_p = stacked_s = None
    small_p, small_s = [], []
    for l in range(depth):
        w = _layer_weights(l, ffn_norm, ffn_gate, ffn_up, ffn_down, mix_norm, w_in, b_forget,
                           conv_w, conv_b, conv_ln_g, conv_ln_b, w_out)
        xp, stacked_p, sp = _layer(xp, empty, w, fn, l, depth, stacked_p)
        cache = (cache_fox_k[l], cache_fox_v[l], cache_fox_logf[l], cache_sb_k[l], cache_sb_v[l],
                 state_conv[l])
        xs, stacked_s, ss = _layer(xs, cache, w, fn, l, depth, stacked_s)
        small_p.append(sp)
        small_s.append(ss)

    def outputs(x, stacked, small):
        b, t, _ = x.shape
        heads = lambda a: a.reshape(depth, b, t, -1, HEAD_DIM)
        kf, vf, ks, vs = stacked
        stack = lambda k: jnp.stack([it[k] for it in small], 0)
        return heads(kf), heads(vf), stack(0), heads(ks), heads(vs), stack(1)

    return (xp, xs) + outputs(xp, stacked_p, small_p) + outputs(xs, stacked_s, small_s)
```

```python
import functools

import jax
import jax.numpy as jnp
from jax import lax
from jax.experimental import pallas as pl
from jax.experimental.pallas import tpu as pltpu

D_MODEL = 1024
HEAD_DIM = 64
D_FOX = 512
C_CONV = 256
D_SB = 256
H_FOX = 8
H_SB = 4
CONV_W = 31
D_FF = 2816
EPS = 1e-6

LANES = 128
SUBLANES = 8
HALO = 32
FOX_HEADS = 4
N_SPLIT = 3
VMEM_LIMIT = 56 * 1024 * 1024
SKIP_LOG = 110.0
LOG2E = 1.4426950408889634
SKIP_LOG2 = SKIP_LOG * LOG2E
NORM_SLACK = 1.01

ZQ, ZK, ZV, ZF, ZA, ZG, ZQS, ZKS, ZVS, ZEND = 0, 512, 1024, 1536, 1664, 1920, 2176, 2432, 2688, 2944

F32 = jnp.float32
BF16 = jnp.bfloat16


def _dot(a, b):
    return jnp.dot(a, b, preferred_element_type=F32)


def _sigmoid(x):
    return 1.0 / (1.0 + jnp.exp(-x))


def _softplus(x):
    return jnp.maximum(x, 0.0) + jnp.log1p(jnp.exp(-jnp.abs(x)))


def _rms(x, g):
    return x * lax.rsqrt(jnp.mean(x * x, axis=-1, keepdims=True) + EPS) * g


def _ffn_half(x, g_ref, wg_ref, wu_ref, wd_ref):
    hn = _rms(x, g_ref[...]).astype(BF16)
    g = _dot(hn, wg_ref[...])
    u = _dot(hn, wu_ref[...])
    act = (g * _sigmoid(g) * u).astype(BF16)
    return x + 0.5 * _dot(act, wd_ref[...])


def _split3(x):
    a = x.astype(BF16)
    r = x - a.astype(F32)
    b = r.astype(BF16)
    c = (r - b.astype(F32)).astype(BF16)
    return a, b, c


def _store_heads(o_ref, x):
    heads = x.shape[1] // HEAD_DIM
    for h in range(heads):
        o_ref[pl.ds(h, x.shape[0], stride=heads), :] = x[:, h * HEAD_DIM:(h + 1) * HEAD_DIM]


def _dense_in_kernel(x_ref, n1_ref, wg_ref, wu_ref, wd_ref, n2_ref, wp_ref, bf_ref, *refs):
    (x1_ref, qf_ref, kf_ref, vf_ref, lfp_ref, u_ref, qs_ref, ks_ref, vs_ref,
     kfb_ref, vfb_ref, ksb_ref, vsb_ref) = refs[-13:]
    x1 = _ffn_half(x_ref[...], n1_ref, wg_ref, wu_ref, wd_ref)
    x1_ref[...] = x1
    h = _rms(x1, n2_ref[...]).astype(BF16)
    z = _dot(h, wp_ref[...])
    scale = HEAD_DIM ** -0.5 * LOG2E
    qf_ref[...] = (z[:, ZQ:ZK] * scale).astype(BF16)
    kf, vf = z[:, ZK:ZV], z[:, ZV:ZF]
    _store_heads(kf_ref, kf)
    _store_heads(vf_ref, vf)
    kfb_ref[...] = kf.astype(BF16)
    vfb_ref[...] = vf.astype(BF16)
    zf = z[:, ZF:ZA] + bf_ref[...]
    lane = lax.broadcasted_iota(jnp.int32, zf.shape, 1)
    lfp_ref[...] = jnp.where(lane < H_FOX, -_softplus(-zf), 0.0)
    u_ref[...] = z[:, ZA:ZG] * _sigmoid(z[:, ZG:ZQS])
    qs_ref[...] = (z[:, ZQS:ZKS] * scale).astype(BF16)
    ks, vs = z[:, ZKS:ZVS], z[:, ZVS:ZEND]
    _store_heads(ks_ref, ks)
    _store_heads(vs_ref, vs)
    ksb_ref[...] = ks.astype(BF16)
    vsb_ref[...] = vs.astype(BF16)


def _const_spec(shape):
    return pl.BlockSpec(shape, lambda *_: (0,) * len(shape), pipeline_mode=pl.Buffered(1))


STACKED = (2, 3, 7, 8)


def _dense_in(x, n1, wg, wu, wd, n2, wp, bfp, tm, layer, depth, stacked):
    n = x.shape[0]
    nt = n // tm
    widths = (D_MODEL, D_FOX, D_FOX, D_FOX, LANES, C_CONV, D_SB, D_SB, D_SB, D_FOX, D_FOX, D_SB, D_SB)
    dtypes = (F32, BF16, F32, F32, F32, F32, BF16, F32, F32, BF16, BF16, BF16, BF16)
    specs, shapes = [], []
    for k, (w, d) in enumerate(zip(widths, dtypes)):
        if k in STACKED:
            heads = w // HEAD_DIM
            specs.append(pl.BlockSpec((tm * heads, HEAD_DIM), lambda i: (i + layer * nt, 0)))
            shapes.append(jax.ShapeDtypeStruct((depth * n * heads, HEAD_DIM), d))
        else:
            specs.append(pl.BlockSpec((tm, w), lambda i: (i, 0)))
            shapes.append(jax.ShapeDtypeStruct((n, w), d))
    prev = () if stacked is None else tuple(stacked)
    n_fixed = 8
    return pl.pallas_call(
        _dense_in_kernel,
        grid=(nt,),
        in_specs=[pl.BlockSpec((tm, D_MODEL), lambda i: (i, 0)), _const_spec((1, D_MODEL)),
                  _const_spec((D_MODEL, D_FF)), _const_spec((D_MODEL, D_FF)), _const_spec((D_FF, D_MODEL)),
                  _const_spec((1, D_MODEL)), _const_spec((D_MODEL, ZEND)), _const_spec((1, LANES))]
                 + [pl.BlockSpec(memory_space=pl.ANY)] * len(prev),
        out_specs=specs,
        out_shape=shapes,
        input_output_aliases={n_fixed + a: k for a, k in enumerate(STACKED)} if prev else {},
        compiler_params=pltpu.CompilerParams(dimension_semantics=("arbitrary",),
                                             vmem_limit_bytes=VMEM_LIMIT),
        name="dense_in",
    )(x, n1, wg, wu, wd, n2, wp, bfp, *prev)


def _dense_out_kernel(x1_ref, yf_ref, yc_ref, ys_ref, wo_ref, n_ref, wg_ref, wu_ref, wd_ref, fn_ref,
                      o_ref, *, final):
    x2 = (x1_ref[...] + _dot(yf_ref[...], wo_ref[0:D_FOX, :])
          + _dot(yc_ref[...], wo_ref[D_FOX:D_FOX + C_CONV, :])
          + _dot(ys_ref[...], wo_ref[D_FOX + C_CONV:, :]))
    x3 = _ffn_half(x2, n_ref, wg_ref, wu_ref, wd_ref)
    o_ref[...] = _rms(x3, fn_ref[...]) if final else x3


def _dense_out(x1, yf, yc, ys, wo, n, wg, wu, wd, fn, tm, final):
    nrow = x1.shape[0]
    row = lambda w: pl.BlockSpec((tm, w), lambda i: (i, 0))
    return pl.pallas_call(
        functools.partial(_dense_out_kernel, final=final),
        grid=(nrow // tm,),
        in_specs=[row(D_MODEL), row(D_FOX), row(C_CONV), row(D_SB), _const_spec((D_MODEL, D_MODEL)),
                  _const_spec((1, D_MODEL)), _const_spec((D_MODEL, D_FF)), _const_spec((D_MODEL, D_FF)),
                  _const_spec((D_FF, D_MODEL)), _const_spec((1, D_MODEL))],
        out_specs=row(D_MODEL),
        out_shape=jax.ShapeDtypeStruct((nrow, D_MODEL), F32),
        compiler_params=pltpu.CompilerParams(dimension_semantics=("arbitrary",),
                                             vmem_limit_bytes=VMEM_LIMIT),
        name="dense_out",
    )(x1, yf, yc, ys, wo, n, wg, wu, wd, fn)


def _prep_kernel(kf_ref, vf_ref, vs_ref, lfp_ref, tri_ref, sel_ref, hsel_ref,
                 kt_ref, vft_ref, vst_ref, bias_max_ref, knorm_max_ref, carry_ref, bmax_ref, kmax_ref):
    i = pl.program_id(1)

    @pl.when(i == 0)
    def _():
        carry_ref[...] = jnp.zeros_like(carry_ref)
        bmax_ref[...] = jnp.full_like(bmax_ref, -jnp.inf)
        kmax_ref[...] = jnp.zeros_like(kmax_ref)

    tri = tri_ref[...]
    cum = carry_ref[...]
    for part in _split3(lfp_ref[0]):
        cum = cum + _dot(tri, part)
    carry_ref[...] = cum[cum.shape[0] - 1:, :]
    bias = -cum * LOG2E
    aug = None
    for s, part in enumerate(_split3(bias)):
        term = _dot(part, sel_ref[s])
        aug = term if aug is None else aug + term
    kb = kf_ref[0]
    for hp in range(H_FOX // 2):
        base = 2 * LANES * hp
        kt_ref[0, :, base:base + LANES] = kb[:, hp * LANES:(hp + 1) * LANES]
        kt_ref[0, :, base + LANES:base + 2 * LANES] = aug[:, hp * LANES:(hp + 1) * LANES].astype(BF16)
    vft_ref[0] = jnp.transpose(vf_ref[0].astype(F32)).astype(BF16)
    vst_ref[0] = jnp.transpose(vs_ref[0].astype(F32)).astype(BF16)

    ksq = kb.astype(F32)
    ksq = ksq * ksq
    hi = ksq.astype(BF16)
    lo = (ksq - hi.astype(F32)).astype(BF16)
    norm2 = _dot(hi, hsel_ref[...]) + _dot(lo, hsel_ref[...])
    kmax = jnp.maximum(kmax_ref[...], jnp.max(norm2, axis=0, keepdims=True))
    bmax = jnp.maximum(bmax_ref[...], jnp.max(bias, axis=0, keepdims=True))
    kmax_ref[...] = kmax
    bmax_ref[...] = bmax
    knorm_max_ref[0, pl.ds(i, 1), :] = jnp.sqrt(kmax) * NORM_SLACK
    bias_max_ref[0, pl.ds(i, 1), :] = bmax


def _prep(kf, vf, vs, lfp, tp):
    b, tk, _ = lfp.shape
    nblk = tk // tp
    r = jnp.arange(tp)
    tri = (r[None, :] <= r[:, None]).astype(BF16)
    h = jnp.arange(LANES)
    col = jnp.arange(D_FOX)
    sel = jnp.stack([(col[None, :] == ((h // 2) * LANES + (h % 2) * N_SPLIT + s)[:, None])
                     & (h[:, None] < H_FOX) for s in range(N_SPLIT)]).astype(BF16)
    hsel = (col[:, None] // HEAD_DIM == h[None, :]).astype(BF16)
    blk = lambda w: pl.BlockSpec((1, tp, w), lambda bi, i: (bi, i, 0))
    blk_t = lambda w: pl.BlockSpec((1, w, tp), lambda bi, i: (bi, 0, i))
    per_batch = pl.BlockSpec((1, nblk, LANES), lambda bi, i: (bi, 0, 0))
    return pl.pallas_call(
        _prep_kernel,
        grid=(b, nblk),
        in_specs=[blk(D_FOX), blk(D_FOX), blk(D_SB), blk(LANES), _const_spec((tp, tp)),
                  _const_spec((N_SPLIT, LANES, D_FOX)), _const_spec((D_FOX, LANES))],
        out_specs=[blk(2 * D_FOX), blk_t(D_FOX), blk_t(D_SB), per_batch, per_batch],
        out_shape=[jax.ShapeDtypeStruct((b, tk, 2 * D_FOX), BF16),
                   jax.ShapeDtypeStruct((b, D_FOX, tk), BF16),
                   jax.ShapeDtypeStruct((b, D_SB, tk), BF16),
                   jax.ShapeDtypeStruct((b, nblk, LANES), F32),
                   jax.ShapeDtypeStruct((b, nblk, LANES), F32)],
        scratch_shapes=[pltpu.VMEM((1, LANES), F32)] * 3,
        compiler_params=pltpu.CompilerParams(dimension_semantics=("arbitrary", "arbitrary"),
                                             vmem_limit_bytes=VMEM_LIMIT),
        name="prep",
    )(kf, vf, vs, lfp, tri, sel, hsel)


def _conv_kernel(u_ref, prev_ref, buf_ref, w_ref, b_ref, g_ref, beta_ref, y_ref, xw_ref, sh_ref, *, tt, rows):
    first = pl.program_id(1) == 0
    xw_ref[0:HALO, :] = jnp.where(first, buf_ref[0], prev_ref[0])
    xw_ref[HALO:HALO + tt, :] = u_ref[0]
    off = HALO - (CONV_W - 1)
    for s in range(1, SUBLANES):
        sh_ref[s - 1] = xw_ref[s:s + tt + HALO - SUBLANES, :]
    for r0 in range(0, tt, rows):
        acc = jnp.zeros((rows, C_CONV), F32)
        for j in range(CONV_W):
            s, base = (j + off) % SUBLANES, (j + off) // SUBLANES * SUBLANES
            src = xw_ref if s == 0 else sh_ref.at[s - 1]
            acc = acc + w_ref[j:j + 1, :] * src[base + r0:base + r0 + rows, :]
        y = acc + b_ref[...]
        mu = jnp.mean(y, axis=-1, keepdims=True)
        yc = y - mu
        var = jnp.mean(yc * yc, axis=-1, keepdims=True)
        y = yc * lax.rsqrt(var + EPS) * g_ref[...] + beta_ref[...]
        y_ref[0, r0:r0 + rows, :] = (y * _sigmoid(y)).astype(BF16)


def _conv(u, buf_pad, w, b, g, beta, tt):
    bsz, t, _ = u.shape
    rows = min(tt, 64)
    per = tt // HALO
    return pl.pallas_call(
        functools.partial(_conv_kernel, tt=tt, rows=rows),
        grid=(bsz, t // tt),
        in_specs=[pl.BlockSpec((1, tt, C_CONV), lambda bi, i: (bi, i, 0)),
                  pl.BlockSpec((1, HALO, C_CONV), lambda bi, i: (bi, jnp.maximum(i * per - 1, 0), 0)),
                  pl.BlockSpec((1, HALO, C_CONV), lambda bi, i: (bi, 0, 0)),
                  _const_spec((CONV_W, C_CONV)), _const_spec((1, C_CONV)), _const_spec((1, C_CONV)),
                  _const_spec((1, C_CONV))],
        out_specs=pl.BlockSpec((1, tt, C_CONV), lambda bi, i: (bi, i, 0)),
        out_shape=jax.ShapeDtypeStruct((bsz, t, C_CONV), BF16),
        scratch_shapes=[pltpu.VMEM((HALO + tt, C_CONV), F32),
                        pltpu.VMEM((SUBLANES - 1, HALO + tt - SUBLANES, C_CONV), F32)],
        compiler_params=pltpu.CompilerParams(dimension_semantics=("arbitrary", "arbitrary")),
        name="conv",
    )(u, u, buf_pad, w, b, g, beta)


def _query_heads(q_ref):
    qt = jnp.transpose(q_ref[0].astype(F32))
    row = lax.broadcasted_iota(jnp.int32, qt.shape, 0)
    return row, jnp.where(row < HEAD_DIM, qt, 0.0), jnp.where(row >= HEAD_DIM, qt, 0.0)


def _diag_block(i, tq, tk, past):
    return (past + i * tq + tq - 1) // tk


def _positions(i, tq, tk, past):
    key_pos = lax.broadcasted_iota(jnp.int32, (tk, tq), 0)
    q_pos = past + i * tq + lax.broadcasted_iota(jnp.int32, (tk, tq), 1)
    return key_pos, q_pos


def _merge_heads(out_a, out_b):
    return jnp.transpose(jnp.concatenate([out_a, out_b], axis=0))


def _descend(jd, step, reaches):
    step(jd, True)

    def cond(c):
        j, go = c
        return jnp.logical_and(j >= 0, go)

    def body(c):
        j, _ = c
        step(j, False)
        return j - 1, reaches(j - 1)

    lax.while_loop(cond, body, (jd - 1, reaches(jd - 1)))


def _fox_kernel(bias_max_ref, knorm_max_ref, q_ref, k_ref, vt_ref, o_ref,
                m_ref, l_ref, acc_ref, s_ref, mblk_ref, alpha_ref, go_ref, *, tq, tk, past, nblk):
    bi = pl.program_id(0)
    hg = pl.program_id(1)
    i = pl.program_id(2)
    q_heads, q_norms = [], []
    for g in range(FOX_HEADS // 2):
        row, qa, qb = _query_heads(q_ref.at[:, :, g * LANES:(g + 1) * LANES])
        ones_a = jnp.where(row < N_SPLIT, 1.0, 0.0)
        ones_b = jnp.where(row < 2 * N_SPLIT, 1.0, 0.0) - ones_a
        q_heads += [jnp.concatenate([qa, ones_a], axis=0).astype(BF16),
                    jnp.concatenate([qb, ones_b], axis=0).astype(BF16)]
        q_norms += [jnp.sqrt(jnp.sum(qh * qh, axis=0, keepdims=True)) * NORM_SLACK for qh in (qa, qb)]
    key_pos, q_pos = _positions(i, tq, tk, past)

    m_ref[...] = jnp.full_like(m_ref, -jnp.inf)
    l_ref[...] = jnp.zeros_like(l_ref)
    acc_ref[...] = jnp.zeros_like(acc_ref)

    def score(j, masked, slot):
        start = pl.multiple_of(j * tk, tk)
        scores = [_dot(k_ref[0, pl.ds(start, tk), (h // 2) * 2 * LANES:(h // 2 + 1) * 2 * LANES], q_heads[h])
                  for h in range(FOX_HEADS)]
        for h in range(FOX_HEADS):
            s = scores[h]
            if masked:
                s = jnp.where(start + key_pos <= q_pos, s, -jnp.inf)
            m_prev = m_ref[h]
            m_new = jnp.maximum(m_prev, jnp.max(s, axis=0, keepdims=True))
            alpha_ref[slot, h] = jnp.exp2(m_prev - m_new)
            mblk_ref[slot, h] = m_new
            m_ref[h] = m_new
            s_ref[slot, h] = s

    def accumulate(j, slot):
        start = pl.multiple_of(j * tk, tk)
        for h in range(FOX_HEADS):
            p = jnp.exp2(s_ref[slot, h] - mblk_ref[slot, h])
            alpha = alpha_ref[slot, h]
            l_ref[h] = alpha * l_ref[h] + jnp.sum(p, axis=0, keepdims=True)
            vt = vt_ref[0, h * HEAD_DIM:(h + 1) * HEAD_DIM, pl.ds(start, tk)]
            acc_ref[h] = alpha * acc_ref[h] + _dot(vt, p.astype(BF16))

    def reaches(j):
        base = (bi * nblk + jnp.maximum(j, 0)) * H_FOX + FOX_HEADS * hg
        worst = None
        for h in range(FOX_HEADS):
            top = q_norms[h] * knorm_max_ref[base + h] + bias_max_ref[base + h] - m_ref[h]
            worst = top if worst is None else jnp.maximum(worst, top)
        return jnp.max(worst) >= -SKIP_LOG2

    jd = _diag_block(i, tq, tk, past)
    score(jd, True, 0)
    waiting = lambda j: (jd - j - 1) & 1

    def cond(c):
        j, go = c
        return jnp.logical_and(j >= 0, go)

    def body(c):
        j, _ = c
        for slot in range(2):
            @pl.when(waiting(j) == slot)
            def _():
                score(j, False, 1 - slot)
                go_ref[0] = reaches(j - 1).astype(jnp.int32)
                accumulate(j + 1, slot)
        return j - 1, go_ref[0] != 0

    j_stop, _ = lax.while_loop(cond, body, (jd - 1, reaches(jd - 1)))
    for slot in range(2):
        @pl.when(waiting(j_stop) == slot)
        def _():
            accumulate(j_stop + 1, slot)
    for g in range(FOX_HEADS // 2):
        o_ref[0, :, g * LANES:(g + 1) * LANES] = _merge_heads(
            acc_ref[2 * g] / l_ref[2 * g], acc_ref[2 * g + 1] / l_ref[2 * g + 1]).astype(BF16)


def _fox(q, kt, vt, bias_max, knorm_max, past, tq, tk):
    b, t, _ = q.shape
    tkeys = kt.shape[1]
    nblk = tkeys // tk
    assert (tk % tq == 0 and past % tk == 0) or tk == tkeys
    smem = pl.BlockSpec(memory_space=pltpu.SMEM)
    per_head = lambda a: a[:, :, :H_FOX].reshape(-1)
    width = FOX_HEADS * HEAD_DIM
    resident = lambda shape, imap: pl.BlockSpec(shape, imap, pipeline_mode=pl.Buffered(1))
    return pl.pallas_call(
        functools.partial(_fox_kernel, tq=tq, tk=tk, past=past, nblk=nblk),
        grid=(b, H_FOX // FOX_HEADS, t // tq),
        in_specs=[smem, smem,
                  pl.BlockSpec((1, tq, width), lambda bi, hg, i: (bi, i, hg)),
                  resident((1, tkeys, 2 * width), lambda bi, hg, i: (bi, 0, hg)),
                  resident((1, width, tkeys), lambda bi, hg, i: (bi, hg, 0))],
        out_specs=pl.BlockSpec((1, tq, width), lambda bi, hg, i: (bi, i, hg)),
        out_shape=jax.ShapeDtypeStruct((b, t, D_FOX), BF16),
        scratch_shapes=[pltpu.VMEM((FOX_HEADS, 1, tq), F32), pltpu.VMEM((FOX_HEADS, 1, tq), F32),
                        pltpu.VMEM((FOX_HEADS, HEAD_DIM, tq), F32), pltpu.VMEM((2, FOX_HEADS, tk, tq), F32),
                        pltpu.VMEM((2, FOX_HEADS, 1, tq), F32), pltpu.VMEM((2, FOX_HEADS, 1, tq), F32),
                        pltpu.SMEM((1,), jnp.int32)],
        compiler_params=pltpu.CompilerParams(dimension_semantics=("arbitrary",) * 3,
                                             vmem_limit_bytes=VMEM_LIMIT),
        name="fox",
    )(per_head(bias_max), per_head(knorm_max), q, kt, vt)


def _sb_kernel(q_ref, k_ref, vt_ref, tri_ref, o_ref, carry_ref, acc_ref, *, tq, tk, past):
    i = pl.program_id(1)
    q_heads = []
    for g in range(H_SB // 2):
        _, qa, qb = _query_heads(q_ref.at[:, :, g * LANES:(g + 1) * LANES])
        q_heads += [qa.astype(BF16), qb.astype(BF16)]
    key_pos, q_pos = _positions(i, tq, tk, past)

    carry_ref[...] = jnp.zeros_like(carry_ref)
    acc_ref[...] = jnp.zeros_like(acc_ref)

    def step(j, masked):
        start = pl.multiple_of(j * tk, tk)
        tri2 = tri_ref[...]
        scores = [_dot(k_ref[0, pl.ds(start, tk), (h // 2) * LANES:(h // 2 + 1) * LANES], q_heads[h])
                  for h in range(H_SB)]
        for h in range(H_SB):
            z = scores[h]
            nl = jnp.maximum(z, 0.0) + jnp.log(1.0 + jnp.exp2(-jnp.abs(z))) * LOG2E
            if masked:
                valid = start + key_pos < q_pos
                nl = jnp.where(valid, nl, 0.0)
            hi = nl.astype(BF16)
            lo = (nl - hi.astype(F32)).astype(BF16)
            suffix = _dot(tri2, jnp.concatenate([hi, lo], axis=0))
            later = carry_ref[h]
            a = jnp.exp2(z - nl - suffix - later)
            if masked:
                a = jnp.where(valid, a, 0.0)
            vt = vt_ref[0, h * HEAD_DIM:(h + 1) * HEAD_DIM, pl.ds(start, tk)]
            acc_ref[h] = acc_ref[h] + _dot(vt, a.astype(BF16))
            carry_ref[h] = later + jnp.sum(nl, axis=0, keepdims=True)

    def reaches(j):
        least = carry_ref[0]
        for h in range(1, H_SB):
            least = jnp.minimum(least, carry_ref[h])
        return jnp.min(least) <= SKIP_LOG2

    _descend(_diag_block(i, tq, tk, past), step, reaches)
    for g in range(H_SB // 2):
        o_ref[0, :, g * LANES:(g + 1) * LANES] = _merge_heads(acc_ref[2 * g], acc_ref[2 * g + 1]).astype(BF16)


def _sb(q, k, vt, past, tq, tk):
    b, t, _ = q.shape
    tkeys = k.shape[1]
    assert (tk % tq == 0 and past % tk == 0) or tk == tkeys
    r = jnp.arange(tk)
    tri = (r[None, :] > r[:, None]).astype(BF16)
    tri2 = jnp.concatenate([tri, tri], axis=1)
    per_batch = lambda shape: pl.BlockSpec(shape, lambda bi, i: (bi, 0, 0), pipeline_mode=pl.Buffered(1))
    return pl.pallas_call(
        functools.partial(_sb_kernel, tq=tq, tk=tk, past=past),
        grid=(b, t // tq),
        in_specs=[pl.BlockSpec((1, tq, D_SB), lambda bi, i: (bi, i, 0)),
                  per_batch((1, tkeys, D_SB)), per_batch((1, D_SB, tkeys)), _const_spec((tk, 2 * tk))],
        out_specs=pl.BlockSpec((1, tq, D_SB), lambda bi, i: (bi, i, 0)),
        out_shape=jax.ShapeDtypeStruct((b, t, D_SB), BF16),
        scratch_shapes=[pltpu.VMEM((H_SB, 1, tq), F32), pltpu.VMEM((H_SB, HEAD_DIM, tq), F32)],
        compiler_params=pltpu.CompilerParams(dimension_semantics=("arbitrary",) * 2,
                                             vmem_limit_bytes=VMEM_LIMIT),
        name="sb",
    )(q, k, vt, tri2)


def _pick_tile(n, pref):
    t = min(n, pref)
    while n % t:
        t //= 2
    return t


def _layer(x, past, w, final_norm, layer, depth, stacked):
    b, t, _ = x.shape
    past_k, past_v, past_lf, past_sk, past_sv, conv_buf = past
    p = past_k.shape[1]
    assert t >= CONV_W - 1 and t % HALO == 0
    n = b * t
    tm = _pick_tile(n, 512)

    (x1, qf, kf, vf, lfp, u, qs, ks, vs, kfb, vfb, ksb, vsb) = _dense_in(
        x.reshape(n, D_MODEL), w["n1"], w["wg1"], w["wu1"], w["wd1"], w["n2"], w["wp"], w["bfp"], tm,
        layer, depth, stacked)

    r3 = lambda a: a.reshape(-1, t, a.shape[-1])
    lfp3, u3 = r3(lfp), r3(u)
    if p:
        fill = -(p + t) % LANES
        cat = lambda old, new: jnp.concatenate(
            [old.reshape(b, p, -1).astype(new.dtype), new,
             jnp.zeros((b, fill, new.shape[-1]), new.dtype)], axis=1)
        kf_all, vf_all, vs_all = cat(past_k, r3(kfb)), cat(past_v, r3(vfb)), cat(past_sv, r3(vsb))
        lf_all = cat(jnp.pad(past_lf, ((0, 0), (0, 0), (0, LANES - H_FOX))), lfp3)
        ks_all = cat(past_sk, r3(ksb))
        tq_fox = tq_sb = t
        tk_fox = p + t + fill
        tk_sb = LANES if (LANES % t == 0 and p % LANES == 0) else tk_fox
    else:
        kf_all, vf_all, vs_all, lf_all, ks_all = r3(kfb), r3(vfb), r3(vsb), lfp3, r3(ksb)
        tq_sb = tk_sb = _pick_tile(t, 256)
        tq_fox = tk_fox = _pick_tile(t, 512)

    kt, vft, vst, bias_max, knorm_max = _prep(kf_all, vf_all, vs_all, lf_all, tk_fox)
    yf = _fox(r3(qf), kt, vft, bias_max, knorm_max, p, tq_fox, tk_fox)
    ys = _sb(r3(qs), ks_all, vst, p, tq_sb, tk_sb)
    buf_pad = jnp.pad(conv_buf, ((0, 0), (HALO - (CONV_W - 1), 0), (0, 0)))
    yc = _conv(u3, buf_pad, w["conv_w"], w["conv_b"], w["ln_g"], w["ln_b"], _pick_tile(t, 256))

    flat = lambda a: a.reshape(n, a.shape[-1])
    xo = _dense_out(x1, flat(yf), flat(yc), flat(ys), w["wo"], w["n3"], w["wg2"], w["wu2"], w["wd2"],
                    final_norm, tm, layer == depth - 1)
    return xo.reshape(b, t, D_MODEL), (kf, vf, ks, vs), (lfp3[..., :H_FOX], u3[:, t - (CONV_W - 1):, :])


def _layer_weights(l, ffn_norm, ffn_gate, ffn_up, ffn_down, mix_norm, w_in, b_forget,
                   conv_w, conv_b, conv_ln_g, conv_ln_b, w_out):
    wi = w_in[l]
    off_f = 3 * D_FOX
    off_glu = off_f + H_FOX
    off_qc = off_glu + 2 * C_CONV
    wp = jnp.concatenate([wi[:, :off_f], jnp.pad(wi[:, off_f:off_glu], ((0, 0), (0, LANES - H_FOX))),
                          wi[:, off_glu:off_qc], wi[:, off_qc:]], axis=1).astype(BF16)
    row = lambda a: a.reshape(1, -1).astype(F32)
    return dict(
        n1=row(ffn_norm[l, 0]), wg1=ffn_gate[l, 0].astype(BF16), wu1=ffn_up[l, 0].astype(BF16),
        wd1=ffn_down[l, 0].astype(BF16), n2=row(mix_norm[l]), wp=wp,
        bfp=jnp.pad(row(b_forget[l]), ((0, 0), (0, LANES - H_FOX))),
        conv_w=conv_w[l], conv_b=row(conv_b[l]), ln_g=row(conv_ln_g[l]), ln_b=row(conv_ln_b[l]),
        wo=w_out[l].astype(BF16), n3=row(ffn_norm[l, 1]), wg2=ffn_gate[l, 1].astype(BF16),
        wu2=ffn_up[l, 1].astype(BF16), wd2=ffn_down[l, 1].astype(BF16))


def kernel(x_prompt, x_sample, cache_fox_k, cache_fox_v, cache_fox_logf, cache_sb_k, cache_sb_v, state_conv,
           ffn_norm, ffn_gate, ffn_up, ffn_down, mix_norm, w_in, b_forget, conv_w, conv_b, conv_ln_g,
           conv_ln_b, w_out, final_norm):
    depth = w_in.shape[0]
    bp = x_prompt.shape[0]
    dt = x_prompt.dtype
    empty = (jnp.zeros((bp, 0, H_FOX, HEAD_DIM), dt), jnp.zeros((bp, 0, H_FOX, HEAD_DIM), dt),
             jnp.zeros((bp, 0, H_FOX), dt), jnp.zeros((bp, 0, H_SB, HEAD_DIM), dt),
             jnp.zeros((bp, 0, H_SB, HEAD_DIM), dt), jnp.zeros((bp, CONV_W - 1, C_CONV), dt))
    fn = final_norm.reshape(1, -1).astype(F32)
    xp, xs = x_prompt, x_sample
    stacked_p = stacked_s = None
    small_p, small_s = [], []
    for l in range(depth):
        w = _layer_weights(l, ffn_norm, ffn_gate, ffn_up, ffn_down, mix_norm, w_in, b_forget,
                           conv_w, conv_b, conv_ln_g, conv_ln_b, w_out)
        xp, stacked_p, sp = _layer(xp, empty, w, fn, l, depth, stacked_p)
        cache = (cache_fox_k[l], cache_fox_v[l], cache_fox_logf[l], cache_sb_k[l], cache_sb_v[l],
                 state_conv[l])
        xs, stacked_s, ss = _layer(xs, cache, w, fn, l, depth, stacked_s)
        small_p.append(sp)
        small_s.append(ss)

    def outputs(x, stacked, small):
        b, t, _ = x.shape
        heads = lambda a: a.reshape(depth, b, t, -1, HEAD_DIM)
        kf, vf, ks, vs = stacked
        stack = lambda k: jnp.stack([it[k] for it in small], 0)
        return heads(kf), heads(vf), stack(0), heads(ks), heads(vs), stack(1)

    return (xp, xs) + outputs(xp, stacked_p, small_p) + outputs(xs, stacked_s, small_s)
```

```python
import functools

import jax
import jax.numpy as jnp
from jax import lax
from jax.experimental import pallas as pl
from jax.experimental.pallas import tpu as pltpu

D_MODEL = 1024
HEAD_DIM = 64
D_FOX = 512
C_CONV = 256
D_SB = 256
H_FOX = 8
H_SB = 4
CONV_W = 31
D_FF = 2816
EPS = 1e-6

LANES = 128
SUBLANES = 8
HALO = 32
FOX_HEADS = 4
N_SPLIT = 3
VMEM_LIMIT = 56 * 1024 * 1024
SKIP_LOG = 110.0
LOG2E = 1.4426950408889634
SKIP_LOG2 = SKIP_LOG * LOG2E
NORM_SLACK = 1.01

ZQ, ZK, ZV, ZF, ZA, ZG, ZQS, ZKS, ZVS, ZEND = 0, 512, 1024, 1536, 1664, 1920, 2176, 2432, 2688, 2944

F32 = jnp.float32
BF16 = jnp.bfloat16


def _dot(a, b):
    return jnp.dot(a, b, preferred_element_type=F32)


def _sigmoid(x):
    return 1.0 / (1.0 + jnp.exp(-x))


def _softplus(x):
    return jnp.maximum(x, 0.0) + jnp.log1p(jnp.exp(-jnp.abs(x)))


def _rms(x, g):
    return x * lax.rsqrt(jnp.mean(x * x, axis=-1, keepdims=True) + EPS) * g


def _ffn_half(x, g_ref, wg_ref, wu_ref, wd_ref):
    hn = _rms(x, g_ref[...]).astype(BF16)
    g = _dot(hn, wg_ref[...])
    u = _dot(hn, wu_ref[...])
    act = (g * _sigmoid(g) * u).astype(BF16)
    return x + 0.5 * _dot(act, wd_ref[...])


def _split3(x):
    a = x.astype(BF16)
    r = x - a.astype(F32)
    b = r.astype(BF16)
    c = (r - b.astype(F32)).astype(BF16)
    return a, b, c


def _store_heads(o_ref, x):
    heads = x.shape[1] // HEAD_DIM
    for h in range(heads):
        o_ref[pl.ds(h, x.shape[0], stride=heads), :] = x[:, h * HEAD_DIM:(h + 1) * HEAD_DIM]


def _dense_in_kernel(x_ref, n1_ref, wg_ref, wu_ref, wd_ref, n2_ref, wp_ref, bf_ref, *refs):
    (x1_ref, qf_ref, kf_ref, vf_ref, lfp_ref, u_ref, qs_ref, ks_ref, vs_ref,
     kfb_ref, vfb_ref, ksb_ref, vsb_ref) = refs[-13:]
    x1 = _ffn_half(x_ref[...], n1_ref, wg_ref, wu_ref, wd_ref)
    x1_ref[...] = x1
    h = _rms(x1, n2_ref[...]).astype(BF16)
    z = _dot(h, wp_ref[...])
    scale = HEAD_DIM ** -0.5 * LOG2E
    qf_ref[...] = (z[:, ZQ:ZK] * scale).astype(BF16)
    kf, vf = z[:, ZK:ZV], z[:, ZV:ZF]
    _store_heads(kf_ref, kf)
    _store_heads(vf_ref, vf)
    kfb_ref[...] = kf.astype(BF16)
    vfb_ref[...] = vf.astype(BF16)
    zf = z[:, ZF:ZA] + bf_ref[...]
    lane = lax.broadcasted_iota(jnp.int32, zf.shape, 1)
    lfp_ref[...] = jnp.where(lane < H_FOX, -_softplus(-zf), 0.0)
    u_ref[...] = z[:, ZA:ZG] * _sigmoid(z[:, ZG:ZQS])
    qs_ref[...] = (z[:, ZQS:ZKS] * scale).astype(BF16)
    ks, vs = z[:, ZKS:ZVS], z[:, ZVS:ZEND]
    _store_heads(ks_ref, ks)
    _store_heads(vs_ref, vs)
    ksb_ref[...] = ks.astype(BF16)
    vsb_ref[...] = vs.astype(BF16)


def _const_spec(shape):
    return pl.BlockSpec(shape, lambda *_: (0,) * len(shape), pipeline_mode=pl.Buffered(1))


STACKED = (2, 3, 7, 8)


def _dense_in(x, n1, wg, wu, wd, n2, wp, bfp, tm, layer, depth, stacked):
    n = x.shape[0]
    nt = n // tm
    widths = (D_MODEL, D_FOX, D_FOX, D_FOX, LANES, C_CONV, D_SB, D_SB, D_SB, D_FOX, D_FOX, D_SB, D_SB)
    dtypes = (F32, BF16, F32, F32, F32, F32, BF16, F32, F32, BF16, BF16, BF16, BF16)
    specs, shapes = [], []
    for k, (w, d) in enumerate(zip(widths, dtypes)):
        if k in STACKED:
            heads = w // HEAD_DIM
            specs.append(pl.BlockSpec((tm * heads, HEAD_DIM), lambda i: (i + layer * nt, 0)))
            shapes.append(jax.ShapeDtypeStruct((depth * n * heads, HEAD_DIM), d))
        else:
            specs.append(pl.BlockSpec((tm, w), lambda i: (i, 0)))
            shapes.append(jax.ShapeDtypeStruct((n, w), d))
    prev = () if stacked is None else tuple(stacked)
    n_fixed = 8
    return pl.pallas_call(
        _dense_in_kernel,
        grid=(nt,),
        in_specs=[pl.BlockSpec((tm, D_MODEL), lambda i: (i, 0)), _const_spec((1, D_MODEL)),
                  _const_spec((D_MODEL, D_FF)), _const_spec((D_MODEL, D_FF)), _const_spec((D_FF, D_MODEL)),
                  _const_spec((1, D_MODEL)), _const_spec((D_MODEL, ZEND)), _const_spec((1, LANES))]
                 + [pl.BlockSpec(memory_space=pl.ANY)] * len(prev),
        out_specs=specs,
        out_shape=shapes,
        input_output_aliases={n_fixed + a: k for a, k in enumerate(STACKED)} if prev else {},
        compiler_params=pltpu.CompilerParams(dimension_semantics=("arbitrary",),
                                             vmem_limit_bytes=VMEM_LIMIT),
        name="dense_in",
    )(x, n1, wg, wu, wd, n2, wp, bfp, *prev)


def _dense_out_kernel(x1_ref, yf_ref, yc_ref, ys_ref, wo_ref, n_ref, wg_ref, wu_ref, wd_ref, fn_ref,
                      o_ref, *, final):
    x2 = (x1_ref[...] + _dot(yf_ref[...], wo_ref[0:D_FOX, :])
          + _dot(yc_ref[...], wo_ref[D_FOX:D_FOX + C_CONV, :])
          + _dot(ys_ref[...], wo_ref[D_FOX + C_CONV:, :]))
    x3 = _ffn_half(x2, n_ref, wg_ref, wu_ref, wd_ref)
    o_ref[...] = _rms(x3, fn_ref[...]) if final else x3


def _dense_out(x1, yf, yc, ys, wo, n, wg, wu, wd, fn, tm, final):
    nrow = x1.shape[0]
    row = lambda w: pl.BlockSpec((tm, w), lambda i: (i, 0))
    return pl.pallas_call(
        functools.partial(_dense_out_kernel, final=final),
        grid=(nrow // tm,),
        in_specs=[row(D_MODEL), row(D_FOX), row(C_CONV), row(D_SB), _const_spec((D_MODEL, D_MODEL)),
                  _const_spec((1, D_MODEL)), _const_spec((D_MODEL, D_FF)), _const_spec((D_MODEL, D_FF)),
                  _const_spec((D_FF, D_MODEL)), _const_spec((1, D_MODEL))],
        out_specs=row(D_MODEL),
        out_shape=jax.ShapeDtypeStruct((nrow, D_MODEL), F32),
        compiler_params=pltpu.CompilerParams(dimension_semantics=("arbitrary",),
                                             vmem_limit_bytes=VMEM_LIMIT),
        name="dense_out",
    )(x1, yf, yc, ys, wo, n, wg, wu, wd, fn)


def _prep_kernel(kf_ref, vf_ref, vs_ref, lfp_ref, tri_ref, sel_ref, hsel_ref,
                 kt_ref, vft_ref, vst_ref, bias_max_ref, knorm_max_ref, carry_ref, bmax_ref, kmax_ref):
    i = pl.program_id(1)

    @pl.when(i == 0)
    def _():
        carry_ref[...] = jnp.zeros_like(carry_ref)
        bmax_ref[...] = jnp.full_like(bmax_ref, -jnp.inf)
        kmax_ref[...] = jnp.zeros_like(kmax_ref)

    tri = tri_ref[...]
    cum = carry_ref[...]
    for part in _split3(lfp_ref[0]):
        cum = cum + _dot(tri, part)
    carry_ref[...] = cum[cum.shape[0] - 1:, :]
    bias = -cum * LOG2E
    aug = None
    for s, part in enumerate(_split3(bias)):
        term = _dot(part, sel_ref[s])
        aug = term if aug is None else aug + term
    kb = kf_ref[0]
    for hp in range(H_FOX // 2):
        base = 2 * LANES * hp
        kt_ref[0, :, base:base + LANES] = kb[:, hp * LANES:(hp + 1) * LANES]
        kt_ref[0, :, base + LANES:base + 2 * LANES] = aug[:, hp * LANES:(hp + 1) * LANES].astype(BF16)
    vft_ref[0] = jnp.transpose(vf_ref[0].astype(F32)).astype(BF16)
    vst_ref[0] = jnp.transpose(vs_ref[0].astype(F32)).astype(BF16)

    ksq = kb.astype(F32)
    ksq = ksq * ksq
    hi = ksq.astype(BF16)
    lo = (ksq - hi.astype(F32)).astype(BF16)
    norm2 = _dot(hi, hsel_ref[...]) + _dot(lo, hsel_ref[...])
    kmax = jnp.maximum(kmax_ref[...], jnp.max(norm2, axis=0, keepdims=True))
    bmax = jnp.maximum(bmax_ref[...], jnp.max(bias, axis=0, keepdims=True))
    kmax_ref[...] = kmax
    bmax_ref[...] = bmax
    knorm_max_ref[0, pl.ds(i, 1), :] = jnp.sqrt(kmax) * NORM_SLACK
    bias_max_ref[0, pl.ds(i, 1), :] = bmax


def _prep(kf, vf, vs, lfp, tp):
    b, tk, _ = lfp.shape
    nblk = tk // tp
    r = jnp.arange(tp)
    tri = (r[None, :] <= r[:, None]).astype(BF16)
    h = jnp.arange(LANES)
    col = jnp.arange(D_FOX)
    sel = jnp.stack([(col[None, :] == ((h // 2) * LANES + (h % 2) * N_SPLIT + s)[:, None])
                     & (h[:, None] < H_FOX) for s in range(N_SPLIT)]).astype(BF16)
    hsel = (col[:, None] // HEAD_DIM == h[None, :]).astype(BF16)
    blk = lambda w: pl.BlockSpec((1, tp, w), lambda bi, i: (bi, i, 0))
    blk_t = lambda w: pl.BlockSpec((1, w, tp), lambda bi, i: (bi, 0, i))
    per_batch = pl.BlockSpec((1, nblk, LANES), lambda bi, i: (bi, 0, 0))
    return pl.pallas_call(
        _prep_kernel,
        grid=(b, nblk),
        in_specs=[blk(D_FOX), blk(D_FOX), blk(D_SB), blk(LANES), _const_spec((tp, tp)),
                  _const_spec((N_SPLIT, LANES, D_FOX)), _const_spec((D_FOX, LANES))],
        out_specs=[blk(2 * D_FOX), blk_t(D_FOX), blk_t(D_SB), per_batch, per_batch],
        out_shape=[jax.ShapeDtypeStruct((b, tk, 2 * D_FOX), BF16),
                   jax.ShapeDtypeStruct((b, D_FOX, tk), BF16),
                   jax.ShapeDtypeStruct((b, D_SB, tk), BF16),
                   jax.ShapeDtypeStruct((b, nblk, LANES), F32),
                   jax.ShapeDtypeStruct((b, nblk, LANES), F32)],
        scratch_shapes=[pltpu.VMEM((1, LANES), F32)] * 3,
        compiler_params=pltpu.CompilerParams(dimension_semantics=("arbitrary", "arbitrary"),
                                             vmem_limit_bytes=VMEM_LIMIT),
        name="prep",
    )(kf, vf, vs, lfp, tri, sel, hsel)


def _conv_kernel(u_ref, prev_ref, buf_ref, w_ref, b_ref, g_ref, beta_ref, y_ref, xw_ref, sh_ref, *, tt, rows):
    first = pl.program_id(1) == 0
    xw_ref[0:HALO, :] = jnp.where(first, buf_ref[0], prev_ref[0])
    xw_ref[HALO:HALO + tt, :] = u_ref[0]
    off = HALO - (CONV_W - 1)
    for s in range(1, SUBLANES):
        sh_ref[s - 1] = xw_ref[s:s + tt + HALO - SUBLANES, :]
    for r0 in range(0, tt, rows):
        acc = jnp.zeros((rows, C_CONV), F32)
        for j in range(CONV_W):
            s, base = (j + off) % SUBLANES, (j + off) // SUBLANES * SUBLANES
            src = xw_ref if s == 0 else sh_ref.at[s - 1]
            acc = acc + w_ref[j:j + 1, :] * src[base + r0:base + r0 + rows, :]
        y = acc + b_ref[...]
        mu = jnp.mean(y, axis=-1, keepdims=True)
        yc = y - mu
        var = jnp.mean(yc * yc, axis=-1, keepdims=True)
        y = yc * lax.rsqrt(var + EPS) * g_ref[...] + beta_ref[...]
        y_ref[0, r0:r0 + rows, :] = (y * _sigmoid(y)).astype(BF16)


def _conv(u, buf_pad, w, b, g, beta, tt):
    bsz, t, _ = u.shape
    rows = min(tt, 64)
    per = tt // HALO
    return pl.pallas_call(
        functools.partial(_conv_kernel, tt=tt, rows=rows),
        grid=(bsz, t // tt),
        in_specs=[pl.BlockSpec((1, tt, C_CONV), lambda bi, i: (bi, i, 0)),
                  pl.BlockSpec((1, HALO, C_CONV), lambda bi, i: (bi, jnp.maximum(i * per - 1, 0), 0)),
                  pl.BlockSpec((1, HALO, C_CONV), lambda bi, i: (bi, 0, 0)),
                  _const_spec((CONV_W, C_CONV)), _const_spec((1, C_CONV)), _const_spec((1, C_CONV)),
                  _const_spec((1, C_CONV))],
        out_specs=pl.BlockSpec((1, tt, C_CONV), lambda bi, i: (bi, i, 0)),
        out_shape=jax.ShapeDtypeStruct((bsz, t, C_CONV), BF16),
        scratch_shapes=[pltpu.VMEM((HALO + tt, C_CONV), F32),
                        pltpu.VMEM((SUBLANES - 1, HALO + tt - SUBLANES, C_CONV), F32)],
        compiler_params=pltpu.CompilerParams(dimension_semantics=("arbitrary", "arbitrary")),
        name="conv",
    )(u, u, buf_pad, w, b, g, beta)


def _query_heads(q_ref):
    qt = jnp.transpose(q_ref[0].astype(F32))
    row = lax.broadcasted_iota(jnp.int32, qt.shape, 0)
    return row, jnp.where(row < HEAD_DIM, qt, 0.0), jnp.where(row >= HEAD_DIM, qt, 0.0)


def _diag_block(i, tq, tk, past):
    return (past + i * tq + tq - 1) // tk


def _positions(i, tq, tk, past):
    key_pos = lax.broadcasted_iota(jnp.int32, (tk, tq), 0)
    q_pos = past + i * tq + lax.broadcasted_iota(jnp.int32, (tk, tq), 1)
    return key_pos, q_pos


def _merge_heads(out_a, out_b):
    return jnp.transpose(jnp.concatenate([out_a, out_b], axis=0))


def _pipelined_descend(jd, first, second, reaches, go_ref):
    first(jd, True, 0)
    waiting = lambda j: (jd - j - 1) & 1

    def cond(c):
        j, go = c
        return jnp.logical_and(j >= 0, go)

    def body(c):
        j, _ = c
        for slot in range(2):
            @pl.when(waiting(j) == slot)
            def _():
                first(j, False, 1 - slot)
                go_ref[0] = reaches(j - 1).astype(jnp.int32)
                second(j + 1, slot)
        return j - 1, go_ref[0] != 0

    j_stop, _ = lax.while_loop(cond, body, (jd - 1, reaches(jd - 1)))
    for slot in range(2):
        @pl.when(waiting(j_stop) == slot)
        def _():
            second(j_stop + 1, slot)


def _fox_kernel(bias_max_ref, knorm_max_ref, q_ref, k_ref, vt_ref, o_ref,
                m_ref, l_ref, acc_ref, s_ref, mblk_ref, alpha_ref, go_ref, *, tq, tk, past, nblk):
    bi = pl.program_id(0)
    hg = pl.program_id(1)
    i = pl.program_id(2)
    q_heads, q_norms = [], []
    for g in range(FOX_HEADS // 2):
        row, qa, qb = _query_heads(q_ref.at[:, :, g * LANES:(g + 1) * LANES])
        ones_a = jnp.where(row < N_SPLIT, 1.0, 0.0)
        ones_b = jnp.where(row < 2 * N_SPLIT, 1.0, 0.0) - ones_a
        q_heads += [jnp.concatenate([qa, ones_a], axis=0).astype(BF16),
                    jnp.concatenate([qb, ones_b], axis=0).astype(BF16)]
        q_norms += [jnp.sqrt(jnp.sum(qh * qh, axis=0, keepdims=True)) * NORM_SLACK for qh in (qa, qb)]
    key_pos, q_pos = _positions(i, tq, tk, past)

    m_ref[...] = jnp.full_like(m_ref, -jnp.inf)
    l_ref[...] = jnp.zeros_like(l_ref)
    acc_ref[...] = jnp.zeros_like(acc_ref)

    def score(j, masked, slot):
        start = pl.multiple_of(j * tk, tk)
        scores = [_dot(k_ref[0, pl.ds(start, tk), (h // 2) * 2 * LANES:(h // 2 + 1) * 2 * LANES], q_heads[h])
                  for h in range(FOX_HEADS)]
        for h in range(FOX_HEADS):
            s = scores[h]
            if masked:
                s = jnp.where(start + key_pos <= q_pos, s, -jnp.inf)
            m_prev = m_ref[h]
            m_new = jnp.maximum(m_prev, jnp.max(s, axis=0, keepdims=True))
            alpha_ref[slot, h] = jnp.exp2(m_prev - m_new)
            mblk_ref[slot, h] = m_new
            m_ref[h] = m_new
            s_ref[slot, h] = s

    def accumulate(j, slot):
        start = pl.multiple_of(j * tk, tk)
        for h in range(FOX_HEADS):
            p = jnp.exp2(s_ref[slot, h] - mblk_ref[slot, h])
            alpha = alpha_ref[slot, h]
            l_ref[h] = alpha * l_ref[h] + jnp.sum(p, axis=0, keepdims=True)
            vt = vt_ref[0, h * HEAD_DIM:(h + 1) * HEAD_DIM, pl.ds(start, tk)]
            acc_ref[h] = alpha * acc_ref[h] + _dot(vt, p.astype(BF16))

    def reaches(j):
        base = (bi * nblk + jnp.maximum(j, 0)) * H_FOX + FOX_HEADS * hg
        worst = None
        for h in range(FOX_HEADS):
            top = q_norms[h] * knorm_max_ref[base + h] + bias_max_ref[base + h] - m_ref[h]
            worst = top if worst is None else jnp.maximum(worst, top)
        return jnp.max(worst) >= -SKIP_LOG2

    _pipelined_descend(_diag_block(i, tq, tk, past), score, accumulate, reaches, go_ref)
    for g in range(FOX_HEADS // 2):
        o_ref[0, :, g * LANES:(g + 1) * LANES] = _merge_heads(
            acc_ref[2 * g] / l_ref[2 * g], acc_ref[2 * g + 1] / l_ref[2 * g + 1]).astype(BF16)


def _fox(q, kt, vt, bias_max, knorm_max, past, tq, tk):
    b, t, _ = q.shape
    tkeys = kt.shape[1]
    nblk = tkeys // tk
    assert (tk % tq == 0 and past % tk == 0) or tk == tkeys
    smem = pl.BlockSpec(memory_space=pltpu.SMEM)
    per_head = lambda a: a[:, :, :H_FOX].reshape(-1)
    width = FOX_HEADS * HEAD_DIM
    resident = lambda shape, imap: pl.BlockSpec(
        shape, imap, pipeline_mode=pl.Buffered(1 if nblk > 1 else 2))
    return pl.pallas_call(
        functools.partial(_fox_kernel, tq=tq, tk=tk, past=past, nblk=nblk),
        grid=(b, H_FOX // FOX_HEADS, t // tq),
        in_specs=[smem, smem,
                  pl.BlockSpec((1, tq, width), lambda bi, hg, i: (bi, i, hg)),
                  resident((1, tkeys, 2 * width), lambda bi, hg, i: (bi, 0, hg)),
                  resident((1, width, tkeys), lambda bi, hg, i: (bi, hg, 0))],
        out_specs=pl.BlockSpec((1, tq, width), lambda bi, hg, i: (bi, i, hg)),
        out_shape=jax.ShapeDtypeStruct((b, t, D_FOX), BF16),
        scratch_shapes=[pltpu.VMEM((FOX_HEADS, 1, tq), F32), pltpu.VMEM((FOX_HEADS, 1, tq), F32),
                        pltpu.VMEM((FOX_HEADS, HEAD_DIM, tq), F32), pltpu.VMEM((2, FOX_HEADS, tk, tq), F32),
                        pltpu.VMEM((2, FOX_HEADS, 1, tq), F32), pltpu.VMEM((2, FOX_HEADS, 1, tq), F32),
                        pltpu.SMEM((1,), jnp.int32)],
        compiler_params=pltpu.CompilerParams(dimension_semantics=("arbitrary",) * 3,
                                             vmem_limit_bytes=VMEM_LIMIT),
        name="fox",
    )(per_head(bias_max), per_head(knorm_max), q, kt, vt)


def _sb_kernel(q_ref, k_ref, vt_ref, tri_ref, o_ref, carry_ref, acc_ref, t_ref, later_ref, go_ref,
               *, tq, tk, past):
    i = pl.program_id(1)
    q_heads = []
    for g in range(H_SB // 2):
        _, qa, qb = _query_heads(q_ref.at[:, :, g * LANES:(g + 1) * LANES])
        q_heads += [qa.astype(BF16), qb.astype(BF16)]
    key_pos, q_pos = _positions(i, tq, tk, past)

    carry_ref[...] = jnp.zeros_like(carry_ref)
    acc_ref[...] = jnp.zeros_like(acc_ref)

    def logits(j, masked, slot):
        start = pl.multiple_of(j * tk, tk)
        tri2 = tri_ref[...]
        scores = [_dot(k_ref[0, pl.ds(start, tk), (h // 2) * LANES:(h // 2 + 1) * LANES], q_heads[h])
                  for h in range(H_SB)]
        for h in range(H_SB):
            z = scores[h]
            nl = jnp.maximum(z, 0.0) + jnp.log(1.0 + jnp.exp2(-jnp.abs(z))) * LOG2E
            if masked:
                valid = start + key_pos < q_pos
                nl = jnp.where(valid, nl, 0.0)
            hi = nl.astype(BF16)
            lo = (nl - hi.astype(F32)).astype(BF16)
            suffix = _dot(tri2, jnp.concatenate([hi, lo], axis=0))
            t = z - nl - suffix
            t_ref[slot, h] = jnp.where(valid, t, -jnp.inf) if masked else t
            later = carry_ref[h]
            later_ref[slot, h] = later
            carry_ref[h] = later + jnp.sum(nl, axis=0, keepdims=True)

    def weigh(j, slot):
        start = pl.multiple_of(j * tk, tk)
        for h in range(H_SB):
            a = jnp.exp2(t_ref[slot, h] - later_ref[slot, h])
            vt = vt_ref[0, h * HEAD_DIM:(h + 1) * HEAD_DIM, pl.ds(start, tk)]
            acc_ref[h] = acc_ref[h] + _dot(vt, a.astype(BF16))

    def reaches(j):
        least = carry_ref[0]
        for h in range(1, H_SB):
            least = jnp.minimum(least, carry_ref[h])
        return jnp.min(least) <= SKIP_LOG2

    _pipelined_descend(_diag_block(i, tq, tk, past), logits, weigh, reaches, go_ref)
    for g in range(H_SB // 2):
        o_ref[0, :, g * LANES:(g + 1) * LANES] = _merge_heads(acc_ref[2 * g], acc_ref[2 * g + 1]).astype(BF16)


def _sb(q, k, vt, past, tq, tk):
    b, t, _ = q.shape
    tkeys = k.shape[1]
    assert (tk % tq == 0 and past % tk == 0) or tk == tkeys
    r = jnp.arange(tk)
    tri = (r[None, :] > r[:, None]).astype(BF16)
    tri2 = jnp.concatenate([tri, tri], axis=1)
    per_batch = lambda shape: pl.BlockSpec(shape, lambda bi, i: (bi, 0, 0), pipeline_mode=pl.Buffered(1))
    return pl.pallas_call(
        functools.partial(_sb_kernel, tq=tq, tk=tk, past=past),
        grid=(b, t // tq),
        in_specs=[pl.BlockSpec((1, tq, D_SB), lambda bi, i: (bi, i, 0)),
                  per_batch((1, tkeys, D_SB)), per_batch((1, D_SB, tkeys)), _const_spec((tk, 2 * tk))],
        out_specs=pl.BlockSpec((1, tq, D_SB), lambda bi, i: (bi, i, 0)),
        out_shape=jax.ShapeDtypeStruct((b, t, D_SB), BF16),
        scratch_shapes=[pltpu.VMEM((H_SB, 1, tq), F32), pltpu.VMEM((H_SB, HEAD_DIM, tq), F32),
                        pltpu.VMEM((2, H_SB, tk, tq), F32), pltpu.VMEM((2, H_SB, 1, tq), F32),
                        pltpu.SMEM((1,), jnp.int32)],
        compiler_params=pltpu.CompilerParams(dimension_semantics=("arbitrary",) * 2,
                                             vmem_limit_bytes=VMEM_LIMIT),
        name="sb",
    )(q, k, vt, tri2)


def _pick_tile(n, pref):
    t = min(n, pref)
    while n % t:
        t //= 2
    return t


def _layer(x, past, w, final_norm, layer, depth, stacked):
    b, t, _ = x.shape
    past_k, past_v, past_lf, past_sk, past_sv, conv_buf = past
    p = past_k.shape[1]
    assert t >= CONV_W - 1 and t % HALO == 0
    n = b * t
    tm = _pick_tile(n, 512)

    (x1, qf, kf, vf, lfp, u, qs, ks, vs, kfb, vfb, ksb, vsb) = _dense_in(
        x.reshape(n, D_MODEL), w["n1"], w["wg1"], w["wu1"], w["wd1"], w["n2"], w["wp"], w["bfp"], tm,
        layer, depth, stacked)

    r3 = lambda a: a.reshape(-1, t, a.shape[-1])
    lfp3, u3 = r3(lfp), r3(u)
    if p:
        fill = -(p + t) % LANES
        cat = lambda old, new: jnp.concatenate(
            [old.reshape(b, p, -1).astype(new.dtype), new,
             jnp.zeros((b, fill, new.shape[-1]), new.dtype)], axis=1)
        kf_all, vf_all, vs_all = cat(past_k, r3(kfb)), cat(past_v, r3(vfb)), cat(past_sv, r3(vsb))
        lf_all = cat(jnp.pad(past_lf, ((0, 0), (0, 0), (0, LANES - H_FOX))), lfp3)
        ks_all = cat(past_sk, r3(ksb))
        tq_fox = tq_sb = t
        tk_fox = p + t + fill
        tk_sb = LANES if (LANES % t == 0 and p % LANES == 0) else tk_fox
    else:
        kf_all, vf_all, vs_all, lf_all, ks_all = r3(kfb), r3(vfb), r3(vsb), lfp3, r3(ksb)
        tq_sb = tk_sb = _pick_tile(t, 256)
        tq_fox = tk_fox = _pick_tile(t, 512)

    kt, vft, vst, bias_max, knorm_max = _prep(kf_all, vf_all, vs_all, lf_all, tk_fox)
    yf = _fox(r3(qf), kt, vft, bias_max, knorm_max, p, tq_fox, tk_fox)
    ys = _sb(r3(qs), ks_all, vst, p, tq_sb, tk_sb)
    buf_pad = jnp.pad(conv_buf, ((0, 0), (HALO - (CONV_W - 1), 0), (0, 0)))
    yc = _conv(u3, buf_pad, w["conv_w"], w["conv_b"], w["ln_g"], w["ln_b"], _pick_tile(t, 256))

    flat = lambda a: a.reshape(n, a.shape[-1])
    xo = _dense_out(x1, flat(yf), flat(yc), flat(ys), w["wo"], w["n3"], w["wg2"], w["wu2"], w["wd2"],
                    final_norm, tm, layer == depth - 1)
    return xo.reshape(b, t, D_MODEL), (kf, vf, ks, vs), (lfp3[..., :H_FOX], u3[:, t - (CONV_W - 1):, :])


def _layer_weights(l, ffn_norm, ffn_gate, ffn_up, ffn_down, mix_norm, w_in, b_forget,
                   conv_w, conv_b, conv_ln_g, conv_ln_b, w_out):
    wi = w_in[l]
    off_f = 3 * D_FOX
    off_glu = off_f + H_FOX
    off_qc = off_glu + 2 * C_CONV
    wp = jnp.concatenate([wi[:, :off_f], jnp.pad(wi[:, off_f:off_glu], ((0, 0), (0, LANES - H_FOX))),
                          wi[:, off_glu:off_qc], wi[:, off_qc:]], axis=1).astype(BF16)
    row = lambda a: a.reshape(1, -1).astype(F32)
    return dict(
        n1=row(ffn_norm[l, 0]), wg1=ffn_gate[l, 0].astype(BF16), wu1=ffn_up[l, 0].astype(BF16),
        wd1=ffn_down[l, 0].astype(BF16), n2=row(mix_norm[l]), wp=wp,
        bfp=jnp.pad(row(b_forget[l]), ((0, 0), (0, LANES - H_FOX))),
        conv_w=conv_w[l], conv_b=row(conv_b[l]), ln_g=row(conv_ln_g[l]), ln_b=row(conv_ln_b[l]),
        wo=w_out[l].astype(BF16), n3=row(ffn_norm[l, 1]), wg2=ffn_gate[l, 1].astype(BF16),
        wu2=ffn_up[l, 1].astype(BF16), wd2=ffn_down[l, 1].astype(BF16))


def kernel(x_prompt, x_sample, cache_fox_k, cache_fox_v, cache_fox_logf, cache_sb_k, cache_sb_v, state_conv,
           ffn_norm, ffn_gate, ffn_up, ffn_down, mix_norm, w_in, b_forget, conv_w, conv_b, conv_ln_g,
           conv_ln_b, w_out, final_norm):
    depth = w_in.shape[0]
    bp = x_prompt.shape[0]
    dt = x_prompt.dtype
    empty = (jnp.zeros((bp, 0, H_FOX, HEAD_DIM), dt), jnp.zeros((bp, 0, H_FOX, HEAD_DIM), dt),
             jnp.zeros((bp, 0, H_FOX), dt), jnp.zeros((bp, 0, H_SB, HEAD_DIM), dt),
             jnp.zeros((bp, 0, H_SB, HEAD_DIM), dt), jnp.zeros((bp, CONV_W - 1, C_CONV), dt))
    fn = final_norm.reshape(1, -1).astype(F32)
    xp, xs = x_prompt, x_sample
    stacked_p = stacked_s = None
    small_p, small_s = [], []
    for l in range(depth):
        w = _layer_weights(l, ffn_norm, ffn_gate, ffn_up, ffn_down, mix_norm, w_in, b_forget,
                           conv_w, conv_b, conv_ln_g, conv_ln_b, w_out)
        xp, stacked_p, sp = _layer(xp, empty, w, fn, l, depth, stacked_p)
        cache = (cache_fox_k[l], cache_fox_v[l], cache_fox_logf[l], cache_sb_k[l], cache_sb_v[l],
                 state_conv[l])
        xs, stacked_s, ss = _layer(xs, cache, w, fn, l, depth, stacked_s)
        small_p.append(sp)
        small_s.append(ss)

    def outputs(x, stacked, small):
        b, t, _ = x.shape
        heads = lambda a: a.reshape(depth, b, t, -1, HEAD_DIM)
        kf, vf, ks, vs = stacked
        stack = lambda k: jnp.stack([it[k] for it in small], 0)
        return heads(kf), heads(vf), stack(0), heads(ks), heads(vs), stack(1)

    return (xp, xs) + outputs(xp, stacked_p, small_p) + outputs(xs, stacked_s, small_s)
```

```python
import functools

import jax
import jax.numpy as jnp
from jax import lax
from jax.experimental import pallas as pl
from jax.experimental.pallas import tpu as pltpu

D_MODEL = 1024
HEAD_DIM = 64
D_FOX = 512
C_CONV = 256
D_SB = 256
H_FOX = 8
H_SB = 4
CONV_W = 31
D_FF = 2816
EPS = 1e-6

LANES = 128
SUBLANES = 8
HALO = 32
ONES_ROWS = 16
FOX_HEADS = 4
DENSE_ROWS = 512
FOX_BLOCK = 512
SB_BLOCK = 256
CONV_ROWS = 512
CONV_CHUNK = 64
N_SPLIT = 3
VMEM_LIMIT = 56 * 1024 * 1024
SKIP_LOG = 110.0
LOG2E = 1.4426950408889634
SKIP_LOG2 = SKIP_LOG * LOG2E
NORM_SLACK = 1.01

ZQ, ZK, ZV, ZF, ZA, ZG, ZQS, ZKS, ZVS, ZEND = 0, 512, 1024, 1536, 1664, 1920, 2176, 2432, 2688, 2944

F32 = jnp.float32
BF16 = jnp.bfloat16


def _dot(a, b):
    return jnp.dot(a, b, preferred_element_type=F32)


def _sigmoid(x):
    return 1.0 / (1.0 + jnp.exp(-x))


def _softplus(x):
    return jnp.maximum(x, 0.0) + jnp.log1p(jnp.exp(-jnp.abs(x)))


def _rms(x, g):
    return x * lax.rsqrt(jnp.mean(x * x, axis=-1, keepdims=True) + EPS) * g


def _ffn_half(x, g_ref, wg_ref, wu_ref, wd_ref):
    hn = _rms(x, g_ref[...]).astype(BF16)
    g = _dot(hn, wg_ref[...])
    u = _dot(hn, wu_ref[...])
    act = (g * _sigmoid(g) * u).astype(BF16)
    return x + 0.5 * _dot(act, wd_ref[...])


def _split3(x):
    a = x.astype(BF16)
    r = x - a.astype(F32)
    b = r.astype(BF16)
    c = (r - b.astype(F32)).astype(BF16)
    return a, b, c


def _store_heads(o_ref, x):
    heads = x.shape[1] // HEAD_DIM
    for h in range(heads):
        o_ref[pl.ds(h, x.shape[0], stride=heads), :] = x[:, h * HEAD_DIM:(h + 1) * HEAD_DIM]


def _dense_in_kernel(x_ref, n1_ref, wg_ref, wu_ref, wd_ref, n2_ref, wp_ref, bf_ref, *refs):
    (x1_ref, qf_ref, kf_ref, vf_ref, lfp_ref, u_ref, qs_ref, ks_ref, vs_ref,
     kfb_ref, vfb_ref, ksb_ref, vsb_ref) = refs[-13:]
    x1 = _ffn_half(x_ref[...], n1_ref, wg_ref, wu_ref, wd_ref)
    x1_ref[...] = x1
    h = _rms(x1, n2_ref[...]).astype(BF16)
    z = _dot(h, wp_ref[...])
    scale = HEAD_DIM ** -0.5 * LOG2E
    qf_ref[...] = (z[:, ZQ:ZK] * scale).astype(BF16)
    kf, vf = z[:, ZK:ZV], z[:, ZV:ZF]
    _store_heads(kf_ref, kf)
    _store_heads(vf_ref, vf)
    kfb_ref[...] = kf.astype(BF16)
    vfb_ref[...] = vf.astype(BF16)
    zf = z[:, ZF:ZA] + bf_ref[...]
    lane = lax.broadcasted_iota(jnp.int32, zf.shape, 1)
    lfp_ref[...] = jnp.where(lane < H_FOX, -_softplus(-zf), 0.0)
    u_ref[...] = z[:, ZA:ZG] * _sigmoid(z[:, ZG:ZQS])
    qs_ref[...] = (z[:, ZQS:ZKS] * scale).astype(BF16)
    ks, vs = z[:, ZKS:ZVS], z[:, ZVS:ZEND]
    _store_heads(ks_ref, ks)
    _store_heads(vs_ref, vs)
    ksb_ref[...] = ks.astype(BF16)
    vsb_ref[...] = vs.astype(BF16)


def _const_spec(shape):
    return pl.BlockSpec(shape, lambda *_: (0,) * len(shape), pipeline_mode=pl.Buffered(1))


STACKED = (2, 3, 7, 8)


def _dense_in(x, n1, wg, wu, wd, n2, wp, bfp, tm, layer, depth, stacked):
    n = x.shape[0]
    nt = n // tm
    widths = (D_MODEL, D_FOX, D_FOX, D_FOX, LANES, C_CONV, D_SB, D_SB, D_SB, D_FOX, D_FOX, D_SB, D_SB)
    dtypes = (F32, BF16, F32, F32, F32, F32, BF16, F32, F32, BF16, BF16, BF16, BF16)
    specs, shapes = [], []
    for k, (w, d) in enumerate(zip(widths, dtypes)):
        if k in STACKED:
            heads = w // HEAD_DIM
            specs.append(pl.BlockSpec((tm * heads, HEAD_DIM), lambda i: (i + layer * nt, 0)))
            shapes.append(jax.ShapeDtypeStruct((depth * n * heads, HEAD_DIM), d))
        else:
            specs.append(pl.BlockSpec((tm, w), lambda i: (i, 0)))
            shapes.append(jax.ShapeDtypeStruct((n, w), d))
    prev = () if stacked is None else tuple(stacked)
    n_fixed = 8
    return pl.pallas_call(
        _dense_in_kernel,
        grid=(nt,),
        in_specs=[pl.BlockSpec((tm, D_MODEL), lambda i: (i, 0)), _const_spec((1, D_MODEL)),
                  _const_spec((D_MODEL, D_FF)), _const_spec((D_MODEL, D_FF)), _const_spec((D_FF, D_MODEL)),
                  _const_spec((1, D_MODEL)), _const_spec((D_MODEL, ZEND)), _const_spec((1, LANES))]
                 + [pl.BlockSpec(memory_space=pl.ANY)] * len(prev),
        out_specs=specs,
        out_shape=shapes,
        input_output_aliases={n_fixed + a: k for a, k in enumerate(STACKED)} if prev else {},
        compiler_params=pltpu.CompilerParams(dimension_semantics=("arbitrary",),
                                             vmem_limit_bytes=VMEM_LIMIT),
        name="dense_in",
    )(x, n1, wg, wu, wd, n2, wp, bfp, *prev)


def _dense_out_kernel(x1_ref, yf_ref, yc_ref, ys_ref, wo_ref, n_ref, wg_ref, wu_ref, wd_ref, fn_ref,
                      o_ref, *, final):
    x2 = (x1_ref[...] + _dot(yf_ref[...], wo_ref[0:D_FOX, :])
          + _dot(yc_ref[...], wo_ref[D_FOX:D_FOX + C_CONV, :])
          + _dot(ys_ref[...], wo_ref[D_FOX + C_CONV:, :]))
    x3 = _ffn_half(x2, n_ref, wg_ref, wu_ref, wd_ref)
    o_ref[...] = _rms(x3, fn_ref[...]) if final else x3


def _dense_out(x1, yf, yc, ys, wo, n, wg, wu, wd, fn, tm, final):
    nrow = x1.shape[0]
    row = lambda w: pl.BlockSpec((tm, w), lambda i: (i, 0))
    return pl.pallas_call(
        functools.partial(_dense_out_kernel, final=final),
        grid=(nrow // tm,),
        in_specs=[row(D_MODEL), row(D_FOX), row(C_CONV), row(D_SB), _const_spec((D_MODEL, D_MODEL)),
                  _const_spec((1, D_MODEL)), _const_spec((D_MODEL, D_FF)), _const_spec((D_MODEL, D_FF)),
                  _const_spec((D_FF, D_MODEL)), _const_spec((1, D_MODEL))],
        out_specs=row(D_MODEL),
        out_shape=jax.ShapeDtypeStruct((nrow, D_MODEL), F32),
        compiler_params=pltpu.CompilerParams(dimension_semantics=("arbitrary",),
                                             vmem_limit_bytes=VMEM_LIMIT),
        name="dense_out",
    )(x1, yf, yc, ys, wo, n, wg, wu, wd, fn)


def _prep_kernel(kf_ref, vf_ref, vs_ref, lfp_ref, tri_ref, sel_ref, hsel_ref,
                 kt_ref, vft_ref, vst_ref, bias_max_ref, knorm_max_ref, carry_ref, bmax_ref, kmax_ref):
    i = pl.program_id(1)

    @pl.when(i == 0)
    def _():
        carry_ref[...] = jnp.zeros_like(carry_ref)
        bmax_ref[...] = jnp.full_like(bmax_ref, -jnp.inf)
        kmax_ref[...] = jnp.zeros_like(kmax_ref)

    tri = tri_ref[...]
    cum = carry_ref[...]
    for part in _split3(lfp_ref[0]):
        cum = cum + _dot(tri, part)
    carry_ref[...] = cum[cum.shape[0] - 1:, :]
    bias = -cum * LOG2E
    aug = None
    for s, part in enumerate(_split3(bias)):
        term = _dot(part, sel_ref[s])
        aug = term if aug is None else aug + term
    kb = kf_ref[0]
    for hp in range(H_FOX // 2):
        base = 2 * LANES * hp
        kt_ref[0, :, base:base + LANES] = kb[:, hp * LANES:(hp + 1) * LANES]
        kt_ref[0, :, base + LANES:base + 2 * LANES] = aug[:, hp * LANES:(hp + 1) * LANES].astype(BF16)
    vft_ref[0] = jnp.transpose(vf_ref[0].astype(F32)).astype(BF16)
    vst_ref[0] = jnp.transpose(vs_ref[0].astype(F32)).astype(BF16)

    ksq = kb.astype(F32)
    ksq = ksq * ksq
    hi = ksq.astype(BF16)
    lo = (ksq - hi.astype(F32)).astype(BF16)
    norm2 = _dot(hi, hsel_ref[...]) + _dot(lo, hsel_ref[...])
    kmax = jnp.maximum(kmax_ref[...], jnp.max(norm2, axis=0, keepdims=True))
    bmax = jnp.maximum(bmax_ref[...], jnp.max(bias, axis=0, keepdims=True))
    kmax_ref[...] = kmax
    bmax_ref[...] = bmax
    knorm_max_ref[0, pl.ds(i, 1), :] = jnp.sqrt(kmax) * NORM_SLACK
    bias_max_ref[0, pl.ds(i, 1), :] = bmax


def _prep(kf, vf, vs, lfp, tp):
    b, tk, _ = lfp.shape
    nblk = tk // tp
    r = jnp.arange(tp)
    tri = (r[None, :] <= r[:, None]).astype(BF16)
    h = jnp.arange(LANES)
    col = jnp.arange(D_FOX)
    sel = jnp.stack([(col[None, :] == ((h // 2) * LANES + (h % 2) * N_SPLIT + s)[:, None])
                     & (h[:, None] < H_FOX) for s in range(N_SPLIT)]).astype(BF16)
    hsel = (col[:, None] // HEAD_DIM == h[None, :]).astype(BF16)
    blk = lambda w: pl.BlockSpec((1, tp, w), lambda bi, i: (bi, i, 0))
    blk_t = lambda w: pl.BlockSpec((1, w, tp), lambda bi, i: (bi, 0, i))
    per_batch = pl.BlockSpec((1, nblk, LANES), lambda bi, i: (bi, 0, 0))
    return pl.pallas_call(
        _prep_kernel,
        grid=(b, nblk),
        in_specs=[blk(D_FOX), blk(D_FOX), blk(D_SB), blk(LANES), _const_spec((tp, tp)),
                  _const_spec((N_SPLIT, LANES, D_FOX)), _const_spec((D_FOX, LANES))],
        out_specs=[blk(2 * D_FOX), blk_t(D_FOX), blk_t(D_SB), per_batch, per_batch],
        out_shape=[jax.ShapeDtypeStruct((b, tk, 2 * D_FOX), BF16),
                   jax.ShapeDtypeStruct((b, D_FOX, tk), BF16),
                   jax.ShapeDtypeStruct((b, D_SB, tk), BF16),
                   jax.ShapeDtypeStruct((b, nblk, LANES), F32),
                   jax.ShapeDtypeStruct((b, nblk, LANES), F32)],
        scratch_shapes=[pltpu.VMEM((1, LANES), F32)] * 3,
        compiler_params=pltpu.CompilerParams(dimension_semantics=("arbitrary", "arbitrary"),
                                             vmem_limit_bytes=VMEM_LIMIT),
        name="prep",
    )(kf, vf, vs, lfp, tri, sel, hsel)


def _conv_kernel(u_ref, prev_ref, buf_ref, w_ref, b_ref, g_ref, beta_ref, y_ref, xw_ref, sh_ref, *, tt, rows):
    first = pl.program_id(1) == 0
    xw_ref[0:HALO, :] = jnp.where(first, buf_ref[0], prev_ref[0])
    xw_ref[HALO:HALO + tt, :] = u_ref[0]
    off = HALO - (CONV_W - 1)
    for s in range(1, SUBLANES):
        sh_ref[s - 1] = xw_ref[s:s + tt + HALO - SUBLANES, :]
    for r0 in range(0, tt, rows):
        acc = jnp.zeros((rows, C_CONV), F32)
        for j in range(CONV_W):
            s, base = (j + off) % SUBLANES, (j + off) // SUBLANES * SUBLANES
            src = xw_ref if s == 0 else sh_ref.at[s - 1]
            acc = acc + w_ref[j:j + 1, :] * src[base + r0:base + r0 + rows, :]
        y = acc + b_ref[...]
        mu = jnp.mean(y, axis=-1, keepdims=True)
        yc = y - mu
        var = jnp.mean(yc * yc, axis=-1, keepdims=True)
        y = yc * lax.rsqrt(var + EPS) * g_ref[...] + beta_ref[...]
        y_ref[0, r0:r0 + rows, :] = (y * _sigmoid(y)).astype(BF16)


def _conv(u, buf_pad, w, b, g, beta, tt):
    bsz, t, _ = u.shape
    rows = min(tt, CONV_CHUNK)
    per = tt // HALO
    return pl.pallas_call(
        functools.partial(_conv_kernel, tt=tt, rows=rows),
        grid=(bsz, t // tt),
        in_specs=[pl.BlockSpec((1, tt, C_CONV), lambda bi, i: (bi, i, 0)),
                  pl.BlockSpec((1, HALO, C_CONV), lambda bi, i: (bi, jnp.maximum(i * per - 1, 0), 0)),
                  pl.BlockSpec((1, HALO, C_CONV), lambda bi, i: (bi, 0, 0)),
                  _const_spec((CONV_W, C_CONV)), _const_spec((1, C_CONV)), _const_spec((1, C_CONV)),
                  _const_spec((1, C_CONV))],
        out_specs=pl.BlockSpec((1, tt, C_CONV), lambda bi, i: (bi, i, 0)),
        out_shape=jax.ShapeDtypeStruct((bsz, t, C_CONV), BF16),
        scratch_shapes=[pltpu.VMEM((HALO + tt, C_CONV), F32),
                        pltpu.VMEM((SUBLANES - 1, HALO + tt - SUBLANES, C_CONV), F32)],
        compiler_params=pltpu.CompilerParams(dimension_semantics=("arbitrary", "arbitrary")),
        name="conv",
    )(u, u, buf_pad, w, b, g, beta)


def _query_heads(q_ref):
    qt = jnp.transpose(q_ref[0].astype(F32))
    row = lax.broadcasted_iota(jnp.int32, qt.shape, 0)
    return row, jnp.where(row < HEAD_DIM, qt, 0.0), jnp.where(row >= HEAD_DIM, qt, 0.0)


def _diag_block(i, tq, tk, past):
    return (past + i * tq + tq - 1) // tk


def _positions(i, tq, tk, past):
    key_pos = lax.broadcasted_iota(jnp.int32, (tk, tq), 0)
    q_pos = past + i * tq + lax.broadcasted_iota(jnp.int32, (tk, tq), 1)
    return key_pos, q_pos


def _merge_heads(out_a, out_b):
    return jnp.transpose(jnp.concatenate([out_a, out_b], axis=0))


def _pipelined_descend(jd, first, second, reaches, go_ref):
    first(jd, True, 0)
    waiting = lambda j: (jd - j - 1) & 1

    def cond(c):
        j, go = c
        return jnp.logical_and(j >= 0, go)

    def body(c):
        j, _ = c
        for slot in range(2):
            @pl.when(waiting(j) == slot)
            def _():
                first(j, False, 1 - slot)
                go_ref[0] = reaches(j - 1).astype(jnp.int32)
                second(j + 1, slot)
        return j - 1, go_ref[0] != 0

    j_stop, _ = lax.while_loop(cond, body, (jd - 1, reaches(jd - 1)))
    for slot in range(2):
        @pl.when(waiting(j_stop) == slot)
        def _():
            second(j_stop + 1, slot)


def _fox_kernel(bias_max_ref, knorm_max_ref, q_ref, k_ref, vt_ref, o_ref,
                m_ref, acc_ref, s_ref, mblk_ref, alpha_ref, go_ref, *, tq, tk, past, nblk):
    bi = pl.program_id(0)
    hg = pl.program_id(1)
    i = pl.program_id(2)
    q_heads, q_norms = [], []
    for g in range(FOX_HEADS // 2):
        row, qa, qb = _query_heads(q_ref.at[:, :, g * LANES:(g + 1) * LANES])
        ones_a = jnp.where(row < N_SPLIT, 1.0, 0.0)
        ones_b = jnp.where(row < 2 * N_SPLIT, 1.0, 0.0) - ones_a
        q_heads += [jnp.concatenate([qa, ones_a], axis=0).astype(BF16),
                    jnp.concatenate([qb, ones_b], axis=0).astype(BF16)]
        q_norms += [jnp.sqrt(jnp.sum(qh * qh, axis=0, keepdims=True)) * NORM_SLACK for qh in (qa, qb)]
    key_pos, q_pos = _positions(i, tq, tk, past)

    m_ref[...] = jnp.full_like(m_ref, -jnp.inf)
    acc_ref[...] = jnp.zeros_like(acc_ref)

    def score(j, masked, slot):
        start = pl.multiple_of(j * tk, tk)
        scores = [_dot(k_ref[0, pl.ds(start, tk), (h // 2) * 2 * LANES:(h // 2 + 1) * 2 * LANES], q_heads[h])
                  for h in range(FOX_HEADS)]
        for h in range(FOX_HEADS):
            s = scores[h]
            if masked:
                s = jnp.where(start + key_pos <= q_pos, s, -jnp.inf)
            m_prev = m_ref[h]
            m_new = jnp.maximum(m_prev, jnp.max(s, axis=0, keepdims=True))
            alpha_ref[slot, h] = jnp.exp2(m_prev - m_new)
            mblk_ref[slot, h] = m_new
            m_ref[h] = m_new
            s_ref[slot, h] = s

    def accumulate(j, slot):
        start = pl.multiple_of(j * tk, tk)
        ones = jnp.ones((ONES_ROWS, tk), BF16)
        for h in range(FOX_HEADS):
            p = jnp.exp2(s_ref[slot, h] - mblk_ref[slot, h])
            vt = vt_ref[0, h * HEAD_DIM:(h + 1) * HEAD_DIM, pl.ds(start, tk)]
            pv = _dot(jnp.concatenate([vt, ones], axis=0), p.astype(BF16))
            acc_ref[h] = alpha_ref[slot, h] * acc_ref[h] + pv

    def reaches(j):
        base = (bi * nblk + jnp.maximum(j, 0)) * H_FOX + FOX_HEADS * hg
        worst = None
        for h in range(FOX_HEADS):
            top = q_norms[h] * knorm_max_ref[base + h] + bias_max_ref[base + h] - m_ref[h]
            worst = top if worst is None else jnp.maximum(worst, top)
        return jnp.max(worst) >= -SKIP_LOG2

    _pipelined_descend(_diag_block(i, tq, tk, past), score, accumulate, reaches, go_ref)
    for g in range(FOX_HEADS // 2):
        o_ref[0, :, g * LANES:(g + 1) * LANES] = _merge_heads(
            *[acc_ref[h, :HEAD_DIM] / acc_ref[h, HEAD_DIM:HEAD_DIM + 1] for h in (2 * g, 2 * g + 1)]).astype(BF16)


def _fox(q, kt, vt, bias_max, knorm_max, past, tq, tk):
    b, t, _ = q.shape
    tkeys = kt.shape[1]
    nblk = tkeys // tk
    assert (tk % tq == 0 and past % tk == 0) or tk == tkeys
    smem = pl.BlockSpec(memory_space=pltpu.SMEM)
    per_head = lambda a: a[:, :, :H_FOX].reshape(-1)
    width = FOX_HEADS * HEAD_DIM
    resident = lambda shape, imap: pl.BlockSpec(
        shape, imap, pipeline_mode=pl.Buffered(1 if nblk > 1 else 2))
    return pl.pallas_call(
        functools.partial(_fox_kernel, tq=tq, tk=tk, past=past, nblk=nblk),
        grid=(b, H_FOX // FOX_HEADS, t // tq),
        in_specs=[smem, smem,
                  pl.BlockSpec((1, tq, width), lambda bi, hg, i: (bi, i, hg)),
                  resident((1, tkeys, 2 * width), lambda bi, hg, i: (bi, 0, hg)),
                  resident((1, width, tkeys), lambda bi, hg, i: (bi, hg, 0))],
        out_specs=pl.BlockSpec((1, tq, width), lambda bi, hg, i: (bi, i, hg)),
        out_shape=jax.ShapeDtypeStruct((b, t, D_FOX), BF16),
        scratch_shapes=[pltpu.VMEM((FOX_HEADS, 1, tq), F32),
                        pltpu.VMEM((FOX_HEADS, HEAD_DIM + ONES_ROWS, tq), F32),
                        pltpu.VMEM((2, FOX_HEADS, tk, tq), F32),
                        pltpu.VMEM((2, FOX_HEADS, 1, tq), F32), pltpu.VMEM((2, FOX_HEADS, 1, tq), F32),
                        pltpu.SMEM((1,), jnp.int32)],
        compiler_params=pltpu.CompilerParams(dimension_semantics=("arbitrary",) * 3,
                                             vmem_limit_bytes=VMEM_LIMIT),
        name="fox",
    )(per_head(bias_max), per_head(knorm_max), q, kt, vt)


def _sb_kernel(q_ref, k_ref, vt_ref, tri_ref, o_ref, carry_ref, acc_ref, t_ref, later_ref, go_ref,
               *, tq, tk, past):
    i = pl.program_id(1)
    q_heads = []
    for g in range(H_SB // 2):
        _, qa, qb = _query_heads(q_ref.at[:, :, g * LANES:(g + 1) * LANES])
        q_heads += [qa.astype(BF16), qb.astype(BF16)]
    key_pos, q_pos = _positions(i, tq, tk, past)

    carry_ref[...] = jnp.zeros_like(carry_ref)
    acc_ref[...] = jnp.zeros_like(acc_ref)

    def logits(j, masked, slot):
        start = pl.multiple_of(j * tk, tk)
        tri2 = tri_ref[...]
        scores = [_dot(k_ref[0, pl.ds(start, tk), (h // 2) * LANES:(h // 2 + 1) * LANES], q_heads[h])
                  for h in range(H_SB)]
        for h in range(H_SB):
            z = scores[h]
            nl = jnp.maximum(z, 0.0) + jnp.log(1.0 + jnp.exp2(-jnp.abs(z))) * LOG2E
            if masked:
                valid = start + key_pos < q_pos
                nl = jnp.where(valid, nl, 0.0)
            hi = nl.astype(BF16)
            lo = (nl - hi.astype(F32)).astype(BF16)
            suffix = _dot(tri2, jnp.concatenate([hi, lo], axis=0))
            t = z - nl - suffix
            t_ref[slot, h] = jnp.where(valid, t, -jnp.inf) if masked else t
            later = carry_ref[h]
            later_ref[slot, h] = later
            carry_ref[h] = later + jnp.sum(nl, axis=0, keepdims=True)

    def weigh(j, slot):
        start = pl.multiple_of(j * tk, tk)
        for h in range(H_SB):
            a = jnp.exp2(t_ref[slot, h] - later_ref[slot, h])
            vt = vt_ref[0, h * HEAD_DIM:(h + 1) * HEAD_DIM, pl.ds(start, tk)]
            acc_ref[h] = acc_ref[h] + _dot(vt, a.astype(BF16))

    def reaches(j):
        least = carry_ref[0]
        for h in range(1, H_SB):
            least = jnp.minimum(least, carry_ref[h])
        return jnp.min(least) <= SKIP_LOG2

    _pipelined_descend(_diag_block(i, tq, tk, past), logits, weigh, reaches, go_ref)
    for g in range(H_SB // 2):
        o_ref[0, :, g * LANES:(g + 1) * LANES] = _merge_heads(acc_ref[2 * g], acc_ref[2 * g + 1]).astype(BF16)


def _sb(q, k, vt, past, tq, tk):
    b, t, _ = q.shape
    tkeys = k.shape[1]
    assert (tk % tq == 0 and past % tk == 0) or tk == tkeys
    r = jnp.arange(tk)
    tri = (r[None, :] > r[:, None]).astype(BF16)
    tri2 = jnp.concatenate([tri, tri], axis=1)
    per_batch = lambda shape: pl.BlockSpec(shape, lambda bi, i: (bi, 0, 0), pipeline_mode=pl.Buffered(1))
    return pl.pallas_call(
        functools.partial(_sb_kernel, tq=tq, tk=tk, past=past),
        grid=(b, t // tq),
        in_specs=[pl.BlockSpec((1, tq, D_SB), lambda bi, i: (bi, i, 0)),
                  per_batch((1, tkeys, D_SB)), per_batch((1, D_SB, tkeys)), _const_spec((tk, 2 * tk))],
        out_specs=pl.BlockSpec((1, tq, D_SB), lambda bi, i: (bi, i, 0)),
        out_shape=jax.ShapeDtypeStruct((b, t, D_SB), BF16),
        scratch_shapes=[pltpu.VMEM((H_SB, 1, tq), F32), pltpu.VMEM((H_SB, HEAD_DIM, tq), F32),
                        pltpu.VMEM((2, H_SB, tk, tq), F32), pltpu.VMEM((2, H_SB, 1, tq), F32),
                        pltpu.SMEM((1,), jnp.int32)],
        compiler_params=pltpu.CompilerParams(dimension_semantics=("arbitrary",) * 2,
                                             vmem_limit_bytes=VMEM_LIMIT),
        name="sb",
    )(q, k, vt, tri2)


def _pick_tile(n, pref):
    t = min(n, pref)
    while n % t:
        t //= 2
    return t


def _layer(x, past, w, final_norm, layer, depth, stacked):
    b, t, _ = x.shape
    past_k, past_v, past_lf, past_sk, past_sv, conv_buf = past
    p = past_k.shape[1]
    assert t >= CONV_W - 1 and t % HALO == 0
    n = b * t
    tm = _pick_tile(n, DENSE_ROWS)

    (x1, qf, kf, vf, lfp, u, qs, ks, vs, kfb, vfb, ksb, vsb) = _dense_in(
        x.reshape(n, D_MODEL), w["n1"], w["wg1"], w["wu1"], w["wd1"], w["n2"], w["wp"], w["bfp"], tm,
        layer, depth, stacked)

    r3 = lambda a: a.reshape(-1, t, a.shape[-1])
    lfp3, u3 = r3(lfp), r3(u)
    if p:
        fill = -(p + t) % LANES
        cat = lambda old, new: jnp.concatenate(
            [old.reshape(b, p, -1).astype(new.dtype), new,
             jnp.zeros((b, fill, new.shape[-1]), new.dtype)], axis=1)
        kf_all, vf_all, vs_all = cat(past_k, r3(kfb)), cat(past_v, r3(vfb)), cat(past_sv, r3(vsb))
        lf_all = cat(jnp.pad(past_lf, ((0, 0), (0, 0), (0, LANES - H_FOX))), lfp3)
        ks_all = cat(past_sk, r3(ksb))
        tq_fox = tq_sb = t
        tk_fox = p + t + fill
        tk_sb = LANES if (LANES % t == 0 and p % LANES == 0) else tk_fox
    else:
        kf_all, vf_all, vs_all, lf_all, ks_all = r3(kfb), r3(vfb), r3(vsb), lfp3, r3(ksb)
        tq_sb = tk_sb = _pick_tile(t, SB_BLOCK)
        tq_fox = tk_fox = _pick_tile(t, FOX_BLOCK)

    kt, vft, vst, bias_max, knorm_max = _prep(kf_all, vf_all, vs_all, lf_all, tk_fox)
    yf = _fox(r3(qf), kt, vft, bias_max, knorm_max, p, tq_fox, tk_fox)
    ys = _sb(r3(qs), ks_all, vst, p, tq_sb, tk_sb)
    buf_pad = jnp.pad(conv_buf, ((0, 0), (HALO - (CONV_W - 1), 0), (0, 0)))
    yc = _conv(u3, buf_pad, w["conv_w"], w["conv_b"], w["ln_g"], w["ln_b"], _pick_tile(t, CONV_ROWS))

    flat = lambda a: a.reshape(n, a.shape[-1])
    xo = _dense_out(x1, flat(yf), flat(yc), flat(ys), w["wo"], w["n3"], w["wg2"], w["wu2"], w["wd2"],
                    final_norm, tm, layer == depth - 1)
    return xo.reshape(b, t, D_MODEL), (kf, vf, ks, vs), (lfp3[..., :H_FOX], u3[:, t - (CONV_W - 1):, :])


def _layer_weights(l, ffn_norm, ffn_gate, ffn_up, ffn_down, mix_norm, w_in, b_forget,
                   conv_w, conv_b, conv_ln_g, conv_ln_b, w_out):
    wi = w_in[l]
    off_f = 3 * D_FOX
    off_glu = off_f + H_FOX
    off_qc = off_glu + 2 * C_CONV
    wp = jnp.concatenate([wi[:, :off_f], jnp.pad(wi[:, off_f:off_glu], ((0, 0), (0, LANES - H_FOX))),
                          wi[:, off_glu:off_qc], wi[:, off_qc:]], axis=1).astype(BF16)
    row = lambda a: a.reshape(1, -1).astype(F32)
    return dict(
        n1=row(ffn_norm[l, 0]), wg1=ffn_gate[l, 0].astype(BF16), wu1=ffn_up[l, 0].astype(BF16),
        wd1=ffn_down[l, 0].astype(BF16), n2=row(mix_norm[l]), wp=wp,
        bfp=jnp.pad(row(b_forget[l]), ((0, 0), (0, LANES - H_FOX))),
        conv_w=conv_w[l], conv_b=row(conv_b[l]), ln_g=row(conv_ln_g[l]), ln_b=row(conv_ln_b[l]),
        wo=w_out[l].astype(BF16), n3=row(ffn_norm[l, 1]), wg2=ffn_gate[l, 1].astype(BF16),
        wu2=ffn_up[l, 1].astype(BF16), wd2=ffn_down[l, 1].astype(BF16))


def kernel(x_prompt, x_sample, cache_fox_k, cache_fox_v, cache_fox_logf, cache_sb_k, cache_sb_v, state_conv,
           ffn_norm, ffn_gate, ffn_up, ffn_down, mix_norm, w_in, b_forget, conv_w, conv_b, conv_ln_g,
           conv_ln_b, w_out, final_norm):
    depth = w_in.shape[0]
    bp = x_prompt.shape[0]
    dt = x_prompt.dtype
    empty = (jnp.zeros((bp, 0, H_FOX, HEAD_DIM), dt), jnp.zeros((bp, 0, H_FOX, HEAD_DIM), dt),
             jnp.zeros((bp, 0, H_FOX), dt), jnp.zeros((bp, 0, H_SB, HEAD_DIM), dt),
             jnp.zeros((bp, 0, H_SB, HEAD_DIM), dt), jnp.zeros((bp, CONV_W - 1, C_CONV), dt))
    fn = final_norm.reshape(1, -1).astype(F32)
    xp, xs = x_prompt, x_sample
    stacked_p = stacked_s = None
    small_p, small_s = [], []
    for l in range(depth):
        w = _layer_weights(l, ffn_norm, ffn_gate, ffn_up, ffn_down, mix_norm, w_in, b_forget,
                           conv_w, conv_b, conv_ln_g, conv_ln_b, w_out)
        xp, stacked_p, sp = _layer(xp, empty, w, fn, l, depth, stacked_p)
        cache = (cache_fox_k[l], cache_fox_v[l], cache_fox_logf[l], cache_sb_k[l], cache_sb_v[l],
                 state_conv[l])
        xs, stacked_s, ss = _layer(xs, cache, w, fn, l, depth, stacked_s)
        small_p.append(sp)
        small_s.append(ss)

    def outputs(x, stacked, small):
        b, t, _ = x.shape
        heads = lambda a: a.reshape(depth, b, t, -1, HEAD_DIM)
        kf, vf, ks, vs = stacked
        stack = lambda k: jnp.stack([it[k] for it in small], 0)
        return heads(kf), heads(vf), stack(0), heads(ks), heads(vs), stack(1)

    return (xp, xs) + outputs(xp, stacked_p, small_p) + outputs(xs, stacked_s, small_s)
```

```python
import functools

import jax
import jax.numpy as jnp
from jax import lax
from jax.experimental import pallas as pl
from jax.experimental.pallas import tpu as pltpu

D_MODEL = 1024
HEAD_DIM = 64
D_FOX = 512
C_CONV = 256
D_SB = 256
H_FOX = 8
H_SB = 4
CONV_W = 31
D_FF = 2816
EPS = 1e-6

LANES = 128
SUBLANES = 8
HALO = 32
ONES_ROWS = 16
FOX_HEADS = 4
DENSE_ROWS = 512
FOX_BLOCK = 512
SB_BLOCK = 256
CONV_ROWS = 512
CONV_CHUNK = 64
N_SPLIT = 3
VMEM_LIMIT = 56 * 1024 * 1024
SKIP_LOG = 110.0
LOG2E = 1.4426950408889634
SKIP_LOG2 = SKIP_LOG * LOG2E
NORM_SLACK = 1.01

ZQ, ZK, ZV, ZF, ZA, ZG, ZQS, ZKS, ZVS, ZEND = 0, 512, 1024, 1536, 1664, 1920, 2176, 2432, 2688, 2944

F32 = jnp.float32
BF16 = jnp.bfloat16


def _dot(a, b):
    return jnp.dot(a, b, preferred_element_type=F32)


def _sigmoid(x):
    return 1.0 / (1.0 + jnp.exp(-x))


def _softplus(x):
    return jnp.maximum(x, 0.0) + jnp.log1p(jnp.exp(-jnp.abs(x)))


def _rms(x, g):
    return x * lax.rsqrt(jnp.mean(x * x, axis=-1, keepdims=True) + EPS) * g


def _ffn_half(x, g_ref, wg_ref, wu_ref, wd_ref):
    hn = _rms(x, g_ref[...]).astype(BF16)
    g = _dot(hn, wg_ref[...])
    u = _dot(hn, wu_ref[...])
    act = (g * _sigmoid(g) * u).astype(BF16)
    return x + 0.5 * _dot(act, wd_ref[...])


def _split3(x):
    a = x.astype(BF16)
    r = x - a.astype(F32)
    b = r.astype(BF16)
    c = (r - b.astype(F32)).astype(BF16)
    return a, b, c


def _store_heads(o_ref, x):
    heads = x.shape[1] // HEAD_DIM
    for h in range(heads):
        o_ref[pl.ds(h, x.shape[0], stride=heads), :] = x[:, h * HEAD_DIM:(h + 1) * HEAD_DIM]


def _dense_in_kernel(x_ref, n1_ref, wg_ref, wu_ref, wd_ref, n2_ref, wp_ref, bf_ref, *refs):
    (x1_ref, qf_ref, kf_ref, vf_ref, lfp_ref, u_ref, qs_ref, ks_ref, vs_ref,
     kfb_ref, vfb_ref, ksb_ref, vsb_ref, lf_ref) = refs[-14:]
    x1 = _ffn_half(x_ref[...], n1_ref, wg_ref, wu_ref, wd_ref)
    x1_ref[...] = x1
    h = _rms(x1, n2_ref[...]).astype(BF16)
    z = _dot(h, wp_ref[...])
    scale = HEAD_DIM ** -0.5 * LOG2E
    qf_ref[...] = (z[:, ZQ:ZK] * scale).astype(BF16)
    kf, vf = z[:, ZK:ZV], z[:, ZV:ZF]
    _store_heads(kf_ref, kf)
    _store_heads(vf_ref, vf)
    kfb_ref[...] = kf.astype(BF16)
    vfb_ref[...] = vf.astype(BF16)
    zf = z[:, ZF:ZA] + bf_ref[...]
    lane = lax.broadcasted_iota(jnp.int32, zf.shape, 1)
    lf = jnp.where(lane < H_FOX, -_softplus(-zf), 0.0)
    lfp_ref[...] = lf
    lf_ref[...] = lf[:, :H_FOX]
    u_ref[...] = z[:, ZA:ZG] * _sigmoid(z[:, ZG:ZQS])
    qs_ref[...] = (z[:, ZQS:ZKS] * scale).astype(BF16)
    ks, vs = z[:, ZKS:ZVS], z[:, ZVS:ZEND]
    _store_heads(ks_ref, ks)
    _store_heads(vs_ref, vs)
    ksb_ref[...] = ks.astype(BF16)
    vsb_ref[...] = vs.astype(BF16)


def _const_spec(shape):
    return pl.BlockSpec(shape, lambda *_: (0,) * len(shape), pipeline_mode=pl.Buffered(1))


STACKED = {2: (H_FOX, HEAD_DIM), 3: (H_FOX, HEAD_DIM), 7: (H_SB, HEAD_DIM), 8: (H_SB, HEAD_DIM), 13: (1, H_FOX)}


def _dense_in(x, n1, wg, wu, wd, n2, wp, bfp, tm, layer, depth, stacked):
    n = x.shape[0]
    nt = n // tm
    widths = (D_MODEL, D_FOX, D_FOX, D_FOX, LANES, C_CONV, D_SB, D_SB, D_SB, D_FOX, D_FOX, D_SB, D_SB, H_FOX)
    dtypes = (F32, BF16, F32, F32, F32, F32, BF16, F32, F32, BF16, BF16, BF16, BF16, F32)
    specs, shapes = [], []
    for k, (w, d) in enumerate(zip(widths, dtypes)):
        if k in STACKED:
            per_token, width = STACKED[k]
            specs.append(pl.BlockSpec((tm * per_token, width), lambda i: (i + layer * nt, 0)))
            shapes.append(jax.ShapeDtypeStruct((depth * n * per_token, width), d))
        else:
            specs.append(pl.BlockSpec((tm, w), lambda i: (i, 0)))
            shapes.append(jax.ShapeDtypeStruct((n, w), d))
    prev = () if stacked is None else tuple(stacked)
    n_fixed = 8
    return pl.pallas_call(
        _dense_in_kernel,
        grid=(nt,),
        in_specs=[pl.BlockSpec((tm, D_MODEL), lambda i: (i, 0)), _const_spec((1, D_MODEL)),
                  _const_spec((D_MODEL, D_FF)), _const_spec((D_MODEL, D_FF)), _const_spec((D_FF, D_MODEL)),
                  _const_spec((1, D_MODEL)), _const_spec((D_MODEL, ZEND)), _const_spec((1, LANES))]
                 + [pl.BlockSpec(memory_space=pl.ANY)] * len(prev),
        out_specs=specs,
        out_shape=shapes,
        input_output_aliases={n_fixed + a: k for a, k in enumerate(STACKED)} if prev else {},
        compiler_params=pltpu.CompilerParams(dimension_semantics=("arbitrary",),
                                             vmem_limit_bytes=VMEM_LIMIT),
        name="dense_in",
    )(x, n1, wg, wu, wd, n2, wp, bfp, *prev)


def _dense_out_kernel(x1_ref, yf_ref, yc_ref, ys_ref, wo_ref, n_ref, wg_ref, wu_ref, wd_ref, fn_ref,
                      o_ref, *, final):
    x2 = (x1_ref[...] + _dot(yf_ref[...], wo_ref[0:D_FOX, :])
          + _dot(yc_ref[...], wo_ref[D_FOX:D_FOX + C_CONV, :])
          + _dot(ys_ref[...], wo_ref[D_FOX + C_CONV:, :]))
    x3 = _ffn_half(x2, n_ref, wg_ref, wu_ref, wd_ref)
    o_ref[...] = _rms(x3, fn_ref[...]) if final else x3


def _dense_out(x1, yf, yc, ys, wo, n, wg, wu, wd, fn, tm, final):
    nrow = x1.shape[0]
    row = lambda w: pl.BlockSpec((tm, w), lambda i: (i, 0))
    return pl.pallas_call(
        functools.partial(_dense_out_kernel, final=final),
        grid=(nrow // tm,),
        in_specs=[row(D_MODEL), row(D_FOX), row(C_CONV), row(D_SB), _const_spec((D_MODEL, D_MODEL)),
                  _const_spec((1, D_MODEL)), _const_spec((D_MODEL, D_FF)), _const_spec((D_MODEL, D_FF)),
                  _const_spec((D_FF, D_MODEL)), _const_spec((1, D_MODEL))],
        out_specs=row(D_MODEL),
        out_shape=jax.ShapeDtypeStruct((nrow, D_MODEL), F32),
        compiler_params=pltpu.CompilerParams(dimension_semantics=("arbitrary",),
                                             vmem_limit_bytes=VMEM_LIMIT),
        name="dense_out",
    )(x1, yf, yc, ys, wo, n, wg, wu, wd, fn)


def _prep_kernel(kf_ref, vf_ref, vs_ref, lfp_ref, tri_ref, sel_ref, hsel_ref,
                 kt_ref, vft_ref, vst_ref, bias_max_ref, knorm_max_ref, carry_ref, bmax_ref, kmax_ref):
    i = pl.program_id(1)

    @pl.when(i == 0)
    def _():
        carry_ref[...] = jnp.zeros_like(carry_ref)
        bmax_ref[...] = jnp.full_like(bmax_ref, -jnp.inf)
        kmax_ref[...] = jnp.zeros_like(kmax_ref)

    tri = tri_ref[...]
    cum = carry_ref[...]
    for part in _split3(lfp_ref[0]):
        cum = cum + _dot(tri, part)
    carry_ref[...] = cum[cum.shape[0] - 1:, :]
    bias = -cum * LOG2E
    aug = None
    for s, part in enumerate(_split3(bias)):
        term = _dot(part, sel_ref[s])
        aug = term if aug is None else aug + term
    kb = kf_ref[0]
    for hp in range(H_FOX // 2):
        base = 2 * LANES * hp
        kt_ref[0, :, base:base + LANES] = kb[:, hp * LANES:(hp + 1) * LANES]
        kt_ref[0, :, base + LANES:base + 2 * LANES] = aug[:, hp * LANES:(hp + 1) * LANES].astype(BF16)
    vft_ref[0] = jnp.transpose(vf_ref[0].astype(F32)).astype(BF16)
    vst_ref[0] = jnp.transpose(vs_ref[0].astype(F32)).astype(BF16)

    ksq = kb.astype(F32)
    ksq = ksq * ksq
    hi = ksq.astype(BF16)
    lo = (ksq - hi.astype(F32)).astype(BF16)
    norm2 = _dot(hi, hsel_ref[...]) + _dot(lo, hsel_ref[...])
    kmax = jnp.maximum(kmax_ref[...], jnp.max(norm2, axis=0, keepdims=True))
    bmax = jnp.maximum(bmax_ref[...], jnp.max(bias, axis=0, keepdims=True))
    kmax_ref[...] = kmax
    bmax_ref[...] = bmax
    knorm_max_ref[0, pl.ds(i, 1), :] = jnp.sqrt(kmax) * NORM_SLACK
    bias_max_ref[0, pl.ds(i, 1), :] = bmax


def _prep(kf, vf, vs, lfp, tp):
    b, tk, _ = lfp.shape
    nblk = tk // tp
    r = jnp.arange(tp)
    tri = (r[None, :] <= r[:, None]).astype(BF16)
    h = jnp.arange(LANES)
    col = jnp.arange(D_FOX)
    sel = jnp.stack([(col[None, :] == ((h // 2) * LANES + (h % 2) * N_SPLIT + s)[:, None])
                     & (h[:, None] < H_FOX) for s in range(N_SPLIT)]).astype(BF16)
    hsel = (col[:, None] // HEAD_DIM == h[None, :]).astype(BF16)
    blk = lambda w: pl.BlockSpec((1, tp, w), lambda bi, i: (bi, i, 0))
    blk_t = lambda w: pl.BlockSpec((1, w, tp), lambda bi, i: (bi, 0, i))
    per_batch = pl.BlockSpec((1, nblk, LANES), lambda bi, i: (bi, 0, 0))
    return pl.pallas_call(
        _prep_kernel,
        grid=(b, nblk),
        in_specs=[blk(D_FOX), blk(D_FOX), blk(D_SB), blk(LANES), _const_spec((tp, tp)),
                  _const_spec((N_SPLIT, LANES, D_FOX)), _const_spec((D_FOX, LANES))],
        out_specs=[blk(2 * D_FOX), blk_t(D_FOX), blk_t(D_SB), per_batch, per_batch],
        out_shape=[jax.ShapeDtypeStruct((b, tk, 2 * D_FOX), BF16),
                   jax.ShapeDtypeStruct((b, D_FOX, tk), BF16),
                   jax.ShapeDtypeStruct((b, D_SB, tk), BF16),
                   jax.ShapeDtypeStruct((b, nblk, LANES), F32),
                   jax.ShapeDtypeStruct((b, nblk, LANES), F32)],
        scratch_shapes=[pltpu.VMEM((1, LANES), F32)] * 3,
        compiler_params=pltpu.CompilerParams(dimension_semantics=("arbitrary", "arbitrary"),
                                             vmem_limit_bytes=VMEM_LIMIT),
        name="prep",
    )(kf, vf, vs, lfp, tri, sel, hsel)


def _conv_kernel(u_ref, prev_ref, buf_ref, w_ref, b_ref, g_ref, beta_ref, y_ref, xw_ref, sh_ref, *, tt, rows):
    first = pl.program_id(1) == 0
    xw_ref[0:HALO, :] = jnp.where(first, buf_ref[0], prev_ref[0])
    xw_ref[HALO:HALO + tt, :] = u_ref[0]
    off = HALO - (CONV_W - 1)
    for s in range(1, SUBLANES):
        sh_ref[s - 1] = xw_ref[s:s + tt + HALO - SUBLANES, :]
    for r0 in range(0, tt, rows):
        acc = jnp.zeros((rows, C_CONV), F32)
        for j in range(CONV_W):
            s, base = (j + off) % SUBLANES, (j + off) // SUBLANES * SUBLANES
            src = xw_ref if s == 0 else sh_ref.at[s - 1]
            acc = acc + w_ref[j:j + 1, :] * src[base + r0:base + r0 + rows, :]
        y = acc + b_ref[...]
        mu = jnp.mean(y, axis=-1, keepdims=True)
        yc = y - mu
        var = jnp.mean(yc * yc, axis=-1, keepdims=True)
        y = yc * lax.rsqrt(var + EPS) * g_ref[...] + beta_ref[...]
        y_ref[0, r0:r0 + rows, :] = (y * _sigmoid(y)).astype(BF16)


def _conv(u, buf_pad, w, b, g, beta, tt):
    bsz, t, _ = u.shape
    rows = min(tt, CONV_CHUNK)
    per = tt // HALO
    return pl.pallas_call(
        functools.partial(_conv_kernel, tt=tt, rows=rows),
        grid=(bsz, t // tt),
        in_specs=[pl.BlockSpec((1, tt, C_CONV), lambda bi, i: (bi, i, 0)),
                  pl.BlockSpec((1, HALO, C_CONV), lambda bi, i: (bi, jnp.maximum(i * per - 1, 0), 0)),
                  pl.BlockSpec((1, HALO, C_CONV), lambda bi, i: (bi, 0, 0)),
                  _const_spec((CONV_W, C_CONV)), _const_spec((1, C_CONV)), _const_spec((1, C_CONV)),
                  _const_spec((1, C_CONV))],
        out_specs=pl.BlockSpec((1, tt, C_CONV), lambda bi, i: (bi, i, 0)),
        out_shape=jax.ShapeDtypeStruct((bsz, t, C_CONV), BF16),
        scratch_shapes=[pltpu.VMEM((HALO + tt, C_CONV), F32),
                        pltpu.VMEM((SUBLANES - 1, HALO + tt - SUBLANES, C_CONV), F32)],
        compiler_params=pltpu.CompilerParams(dimension_semantics=("arbitrary", "arbitrary")),
        name="conv",
    )(u, u, buf_pad, w, b, g, beta)


def _query_heads(q_ref):
    qt = jnp.transpose(q_ref[0].astype(F32))
    row = lax.broadcasted_iota(jnp.int32, qt.shape, 0)
    return row, jnp.where(row < HEAD_DIM, qt, 0.0), jnp.where(row >= HEAD_DIM, qt, 0.0)


def _diag_block(i, tq, tk, past):
    return (past + i * tq + tq - 1) // tk


def _positions(i, tq, tk, past):
    key_pos = lax.broadcasted_iota(jnp.int32, (tk, tq), 0)
    q_pos = past + i * tq + lax.broadcasted_iota(jnp.int32, (tk, tq), 1)
    return key_pos, q_pos


def _merge_heads(out_a, out_b):
    return jnp.transpose(jnp.concatenate([out_a, out_b], axis=0))


def _pipelined_descend(jd, first, second, reaches, go_ref):
    first(jd, True, 0)
    waiting = lambda j: (jd - j - 1) & 1

    def cond(c):
        j, go = c
        return jnp.logical_and(j >= 0, go)

    def body(c):
        j, _ = c
        for slot in range(2):
            @pl.when(waiting(j) == slot)
            def _():
                first(j, False, 1 - slot)
                go_ref[0] = reaches(j - 1).astype(jnp.int32)
                second(j + 1, slot)
        return j - 1, go_ref[0] != 0

    j_stop, _ = lax.while_loop(cond, body, (jd - 1, reaches(jd - 1)))
    for slot in range(2):
        @pl.when(waiting(j_stop) == slot)
        def _():
            second(j_stop + 1, slot)


def _fox_kernel(bias_max_ref, knorm_max_ref, q_ref, k_ref, vt_ref, o_ref,
                m_ref, acc_ref, s_ref, mblk_ref, alpha_ref, go_ref, *, tq, tk, past, nblk):
    bi = pl.program_id(0)
    hg = pl.program_id(1)
    i = pl.program_id(2)
    q_heads, q_norms = [], []
    for g in range(FOX_HEADS // 2):
        row, qa, qb = _query_heads(q_ref.at[:, :, g * LANES:(g + 1) * LANES])
        ones_a = jnp.where(row < N_SPLIT, 1.0, 0.0)
        ones_b = jnp.where(row < 2 * N_SPLIT, 1.0, 0.0) - ones_a
        q_heads += [jnp.concatenate([qa, ones_a], axis=0).astype(BF16),
                    jnp.concatenate([qb, ones_b], axis=0).astype(BF16)]
        q_norms += [jnp.sqrt(jnp.sum(qh * qh, axis=0, keepdims=True)) * NORM_SLACK for qh in (qa, qb)]
    key_pos, q_pos = _positions(i, tq, tk, past)

    m_ref[...] = jnp.full_like(m_ref, -jnp.inf)
    acc_ref[...] = jnp.zeros_like(acc_ref)

    def score(j, masked, slot):
        start = pl.multiple_of(j * tk, tk)
        scores = [_dot(k_ref[0, pl.ds(start, tk), (h // 2) * 2 * LANES:(h // 2 + 1) * 2 * LANES], q_heads[h])
                  for h in range(FOX_HEADS)]
        for h in range(FOX_HEADS):
            s = scores[h]
            if masked:
                s = jnp.where(start + key_pos <= q_pos, s, -jnp.inf)
            m_prev = m_ref[h]
            m_new = jnp.maximum(m_prev, jnp.max(s, axis=0, keepdims=True))
            alpha_ref[slot, h] = jnp.exp2(m_prev - m_new)
            mblk_ref[slot, h] = m_new
            m_ref[h] = m_new
            s_ref[slot, h] = s

    def accumulate(j, slot):
        start = pl.multiple_of(j * tk, tk)
        ones = jnp.ones((ONES_ROWS, tk), BF16)
        for h in range(FOX_HEADS):
            p = jnp.exp2(s_ref[slot, h] - mblk_ref[slot, h])
            vt = vt_ref[0, h * HEAD_DIM:(h + 1) * HEAD_DIM, pl.ds(start, tk)]
            pv = _dot(jnp.concatenate([vt, ones], axis=0), p.astype(BF16))
            acc_ref[h] = alpha_ref[slot, h] * acc_ref[h] + pv

    def reaches(j):
        base = (bi * nblk + jnp.maximum(j, 0)) * H_FOX + FOX_HEADS * hg
        worst = None
        for h in range(FOX_HEADS):
            top = q_norms[h] * knorm_max_ref[base + h] + bias_max_ref[base + h] - m_ref[h]
            worst = top if worst is None else jnp.maximum(worst, top)
        return jnp.max(worst) >= -SKIP_LOG2

    _pipelined_descend(_diag_block(i, tq, tk, past), score, accumulate, reaches, go_ref)
    for g in range(FOX_HEADS // 2):
        o_ref[0, :, g * LANES:(g + 1) * LANES] = _merge_heads(
            *[acc_ref[h, :HEAD_DIM] / acc_ref[h, HEAD_DIM:HEAD_DIM + 1] for h in (2 * g, 2 * g + 1)]).astype(BF16)


def _fox(q, kt, vt, bias_max, knorm_max, past, tq, tk):
    b, t, _ = q.shape
    tkeys = kt.shape[1]
    nblk = tkeys // tk
    assert (tk % tq == 0 and past % tk == 0) or tk == tkeys
    smem = pl.BlockSpec(memory_space=pltpu.SMEM)
    per_head = lambda a: a[:, :, :H_FOX].reshape(-1)
    width = FOX_HEADS * HEAD_DIM
    resident = lambda shape, imap: pl.BlockSpec(
        shape, imap, pipeline_mode=pl.Buffered(1 if nblk > 1 else 2))
    return pl.pallas_call(
        functools.partial(_fox_kernel, tq=tq, tk=tk, past=past, nblk=nblk),
        grid=(b, H_FOX // FOX_HEADS, t // tq),
        in_specs=[smem, smem,
                  pl.BlockSpec((1, tq, width), lambda bi, hg, i: (bi, i, hg)),
                  resident((1, tkeys, 2 * width), lambda bi, hg, i: (bi, 0, hg)),
                  resident((1, width, tkeys), lambda bi, hg, i: (bi, hg, 0))],
        out_specs=pl.BlockSpec((1, tq, width), lambda bi, hg, i: (bi, i, hg)),
        out_shape=jax.ShapeDtypeStruct((b, t, D_FOX), BF16),
        scratch_shapes=[pltpu.VMEM((FOX_HEADS, 1, tq), F32),
                        pltpu.VMEM((FOX_HEADS, HEAD_DIM + ONES_ROWS, tq), F32),
                        pltpu.VMEM((2, FOX_HEADS, tk, tq), F32),
                        pltpu.VMEM((2, FOX_HEADS, 1, tq), F32), pltpu.VMEM((2, FOX_HEADS, 1, tq), F32),
                        pltpu.SMEM((1,), jnp.int32)],
        compiler_params=pltpu.CompilerParams(dimension_semantics=("arbitrary",) * 3,
                                             vmem_limit_bytes=VMEM_LIMIT),
        name="fox",
    )(per_head(bias_max), per_head(knorm_max), q, kt, vt)


def _sb_kernel(q_ref, k_ref, vt_ref, tri_ref, o_ref, carry_ref, acc_ref, t_ref, later_ref, go_ref,
               *, tq, tk, past):
    i = pl.program_id(1)
    q_heads = []
    for g in range(H_SB // 2):
        _, qa, qb = _query_heads(q_ref.at[:, :, g * LANES:(g + 1) * LANES])
        q_heads += [qa.astype(BF16), qb.astype(BF16)]
    key_pos, q_pos = _positions(i, tq, tk, past)

    carry_ref[...] = jnp.zeros_like(carry_ref)
    acc_ref[...] = jnp.zeros_like(acc_ref)

    def logits(j, masked, slot):
        start = pl.multiple_of(j * tk, tk)
        tri2 = tri_ref[...]
        scores = [_dot(k_ref[0, pl.ds(start, tk), (h // 2) * LANES:(h // 2 + 1) * LANES], q_heads[h])
                  for h in range(H_SB)]
        for h in range(H_SB):
            z = scores[h]
            nl = jnp.maximum(z, 0.0) + jnp.log(1.0 + jnp.exp2(-jnp.abs(z))) * LOG2E
            if masked:
                valid = start + key_pos < q_pos
                nl = jnp.where(valid, nl, 0.0)
            hi = nl.astype(BF16)
            lo = (nl - hi.astype(F32)).astype(BF16)
            suffix = _dot(tri2, jnp.concatenate([hi, lo], axis=0))
            t = z - nl - suffix
            t_ref[slot, h] = jnp.where(valid, t, -jnp.inf) if masked else t
            later = carry_ref[h]
            later_ref[slot, h] = later
            carry_ref[h] = later + jnp.sum(nl, axis=0, keepdims=True)

    def weigh(j, slot):
        start = pl.multiple_of(j * tk, tk)
        for h in range(H_SB):
            a = jnp.exp2(t_ref[slot, h] - later_ref[slot, h])
            vt = vt_ref[0, h * HEAD_DIM:(h + 1) * HEAD_DIM, pl.ds(start, tk)]
            acc_ref[h] = acc_ref[h] + _dot(vt, a.astype(BF16))

    def reaches(j):
        least = carry_ref[0]
        for h in range(1, H_SB):
            least = jnp.minimum(least, carry_ref[h])
        return jnp.min(least) <= SKIP_LOG2

    _pipelined_descend(_diag_block(i, tq, tk, past), logits, weigh, reaches, go_ref)
    for g in range(H_SB // 2):
        o_ref[0, :, g * LANES:(g + 1) * LANES] = _merge_heads(acc_ref[2 * g], acc_ref[2 * g + 1]).astype(BF16)


def _sb(q, k, vt, past, tq, tk):
    b, t, _ = q.shape
    tkeys = k.shape[1]
    assert (tk % tq == 0 and past % tk == 0) or tk == tkeys
    r = jnp.arange(tk)
    tri = (r[None, :] > r[:, None]).astype(BF16)
    tri2 = jnp.concatenate([tri, tri], axis=1)
    per_batch = lambda shape: pl.BlockSpec(shape, lambda bi, i: (bi, 0, 0), pipeline_mode=pl.Buffered(1))
    return pl.pallas_call(
        functools.partial(_sb_kernel, tq=tq, tk=tk, past=past),
        grid=(b, t // tq),
        in_specs=[pl.BlockSpec((1, tq, D_SB), lambda bi, i: (bi, i, 0)),
                  per_batch((1, tkeys, D_SB)), per_batch((1, D_SB, tkeys)), _const_spec((tk, 2 * tk))],
        out_specs=pl.BlockSpec((1, tq, D_SB), lambda bi, i: (bi, i, 0)),
        out_shape=jax.ShapeDtypeStruct((b, t, D_SB), BF16),
        scratch_shapes=[pltpu.VMEM((H_SB, 1, tq), F32), pltpu.VMEM((H_SB, HEAD_DIM, tq), F32),
                        pltpu.VMEM((2, H_SB, tk, tq), F32), pltpu.VMEM((2, H_SB, 1, tq), F32),
                        pltpu.SMEM((1,), jnp.int32)],
        compiler_params=pltpu.CompilerParams(dimension_semantics=("arbitrary",) * 2,
                                             vmem_limit_bytes=VMEM_LIMIT),
        name="sb",
    )(q, k, vt, tri2)


def _pick_tile(n, pref):
    t = min(n, pref)
    while n % t:
        t //= 2
    return t


def _layer(x, past, w, final_norm, layer, depth, stacked):
    b, t, _ = x.shape
    past_k, past_v, past_lf, past_sk, past_sv, conv_buf = past
    p = past_k.shape[1]
    assert t >= CONV_W - 1 and t % HALO == 0
    n = b * t
    tm = _pick_tile(n, DENSE_ROWS)

    (x1, qf, kf, vf, lfp, u, qs, ks, vs, kfb, vfb, ksb, vsb, lf) = _dense_in(
        x.reshape(n, D_MODEL), w["n1"], w["wg1"], w["wu1"], w["wd1"], w["n2"], w["wp"], w["bfp"], tm,
        layer, depth, stacked)

    r3 = lambda a: a.reshape(-1, t, a.shape[-1])
    lfp3, u3 = r3(lfp), r3(u)
    if p:
        fill = -(p + t) % LANES
        cat = lambda old, new: jnp.concatenate(
            [old.reshape(b, p, -1).astype(new.dtype), new,
             jnp.zeros((b, fill, new.shape[-1]), new.dtype)], axis=1)
        kf_all, vf_all, vs_all = cat(past_k, r3(kfb)), cat(past_v, r3(vfb)), cat(past_sv, r3(vsb))
        lf_all = cat(jnp.pad(past_lf, ((0, 0), (0, 0), (0, LANES - H_FOX))), lfp3)
        ks_all = cat(past_sk, r3(ksb))
        tq_fox = tq_sb = t
        tk_fox = p + t + fill
        tk_sb = LANES if (LANES % t == 0 and p % LANES == 0) else tk_fox
    else:
        kf_all, vf_all, vs_all, lf_all, ks_all = r3(kfb), r3(vfb), r3(vsb), lfp3, r3(ksb)
        tq_sb = tk_sb = _pick_tile(t, SB_BLOCK)
        tq_fox = tk_fox = _pick_tile(t, FOX_BLOCK)

    kt, vft, vst, bias_max, knorm_max = _prep(kf_all, vf_all, vs_all, lf_all, tk_fox)
    yf = _fox(r3(qf), kt, vft, bias_max, knorm_max, p, tq_fox, tk_fox)
    ys = _sb(r3(qs), ks_all, vst, p, tq_sb, tk_sb)
    buf_pad = jnp.pad(conv_buf, ((0, 0), (HALO - (CONV_W - 1), 0), (0, 0)))
    yc = _conv(u3, buf_pad, w["conv_w"], w["conv_b"], w["ln_g"], w["ln_b"], _pick_tile(t, CONV_ROWS))

    flat = lambda a: a.reshape(n, a.shape[-1])
    xo = _dense_out(x1, flat(yf), flat(yc), flat(ys), w["wo"], w["n3"], w["wg2"], w["wu2"], w["wd2"],
                    final_norm, tm, layer == depth - 1)
    return xo.reshape(b, t, D_MODEL), (kf, vf, ks, vs, lf), u3[:, t - (CONV_W - 1):, :]


def _layer_weights(l, ffn_norm, ffn_gate, ffn_up, ffn_down, mix_norm, w_in, b_forget,
                   conv_w, conv_b, conv_ln_g, conv_ln_b, w_out):
    wi = w_in[l]
    off_f = 3 * D_FOX
    off_glu = off_f + H_FOX
    off_qc = off_glu + 2 * C_CONV
    wp = jnp.concatenate([wi[:, :off_f], jnp.pad(wi[:, off_f:off_glu], ((0, 0), (0, LANES - H_FOX))),
                          wi[:, off_glu:off_qc], wi[:, off_qc:]], axis=1).astype(BF16)
    row = lambda a: a.reshape(1, -1).astype(F32)
    return dict(
        n1=row(ffn_norm[l, 0]), wg1=ffn_gate[l, 0].astype(BF16), wu1=ffn_up[l, 0].astype(BF16),
        wd1=ffn_down[l, 0].astype(BF16), n2=row(mix_norm[l]), wp=wp,
        bfp=jnp.pad(row(b_forget[l]), ((0, 0), (0, LANES - H_FOX))),
        conv_w=conv_w[l], conv_b=row(conv_b[l]), ln_g=row(conv_ln_g[l]), ln_b=row(conv_ln_b[l]),
        wo=w_out[l].astype(BF16), n3=row(ffn_norm[l, 1]), wg2=ffn_gate[l, 1].astype(BF16),
        wu2=ffn_up[l, 1].astype(BF16), wd2=ffn_down[l, 1].astype(BF16))


def kernel(x_prompt, x_sample, cache_fox_k, cache_fox_v, cache_fox_logf, cache_sb_k, cache_sb_v, state_conv,
           ffn_norm, ffn_gate, ffn_up, ffn_down, mix_norm, w_in, b_forget, conv_w, conv_b, conv_ln_g,
           conv_ln_b, w_out, final_norm):
    depth = w_in.shape[0]
    bp = x_prompt.shape[0]
    dt = x_prompt.dtype
    empty = (jnp.zeros((bp, 0, H_FOX, HEAD_DIM), dt), jnp.zeros((bp, 0, H_FOX, HEAD_DIM), dt),
             jnp.zeros((bp, 0, H_FOX), dt), jnp.zeros((bp, 0, H_SB, HEAD_DIM), dt),
             jnp.zeros((bp, 0, H_SB, HEAD_DIM), dt), jnp.zeros((bp, CONV_W - 1, C_CONV), dt))
    fn = final_norm.reshape(1, -1).astype(F32)
    xp, xs = x_prompt, x_sample
    stacked_p = stacked_s = None
    conv_p, conv_s = [], []
    for l in range(depth):
        w = _layer_weights(l, ffn_norm, ffn_gate, ffn_up, ffn_down, mix_norm, w_in, b_forget,
                           conv_w, conv_b, conv_ln_g, conv_ln_b, w_out)
        xp, stacked_p, cp = _layer(xp, empty, w, fn, l, depth, stacked_p)
        cache = (cache_fox_k[l], cache_fox_v[l], cache_fox_logf[l], cache_sb_k[l], cache_sb_v[l],
                 state_conv[l])
        xs, stacked_s, cs = _layer(xs, cache, w, fn, l, depth, stacked_s)
        conv_p.append(cp)
        conv_s.append(cs)

    def outputs(x, stacked, conv):
        b, t, _ = x.shape
        heads = lambda a: a.reshape(depth, b, t, -1, HEAD_DIM)
        kf, vf, ks, vs, lf = stacked
        return heads(kf), heads(vf), lf.reshape(depth, b, t, H_FOX), heads(ks), heads(vs), jnp.stack(conv, 0)

    return (xp, xs) + outputs(xp, stacked_p, conv_p) + outputs(xs, stacked_s, conv_s)
```

```python
import functools

import jax
import jax.numpy as jnp
from jax import lax
from jax.experimental import pallas as pl
from jax.experimental.pallas import tpu as pltpu

D_MODEL = 1024
HEAD_DIM = 64
D_FOX = 512
C_CONV = 256
D_SB = 256
H_FOX = 8
H_SB = 4
CONV_W = 31
D_FF = 2816
EPS = 1e-6

LANES = 128
SUBLANES = 8
HALO = 32
ONES_ROWS = 16
FOX_HEADS = 4
DENSE_ROWS = 512
FOX_BLOCK = 512
SB_BLOCK = 256
CONV_ROWS = 512
CONV_CHUNK = 64
N_SPLIT = 3
VMEM_LIMIT = 56 * 1024 * 1024
SKIP_LOG = 110.0
LOG2E = 1.4426950408889634
SKIP_LOG2 = SKIP_LOG * LOG2E
NORM_SLACK = 1.01

ZQ, ZK, ZV, ZF, ZA, ZG, ZQS, ZKS, ZVS, ZEND = 0, 512, 1024, 1536, 1664, 1920, 2176, 2432, 2688, 2944

F32 = jnp.float32
BF16 = jnp.bfloat16


def _dot(a, b):
    return jnp.dot(a, b, preferred_element_type=F32)


def _sigmoid(x):
    return 1.0 / (1.0 + jnp.exp(-x))


def _softplus(x):
    return jnp.maximum(x, 0.0) + jnp.log1p(jnp.exp(-jnp.abs(x)))


def _rms(x, g):
    return x * lax.rsqrt(jnp.mean(x * x, axis=-1, keepdims=True) + EPS) * g


def _ffn_half(x, g_ref, wg_ref, wu_ref, wd_ref):
    hn = _rms(x, g_ref[...]).astype(BF16)
    g = _dot(hn, wg_ref[...])
    u = _dot(hn, wu_ref[...])
    act = (g * _sigmoid(g) * u).astype(BF16)
    return x + 0.5 * _dot(act, wd_ref[...])


def _split3(x):
    a = x.astype(BF16)
    r = x - a.astype(F32)
    b = r.astype(BF16)
    c = (r - b.astype(F32)).astype(BF16)
    return a, b, c


def _store_heads(o_ref, x):
    heads = x.shape[1] // HEAD_DIM
    for h in range(heads):
        o_ref[pl.ds(h, x.shape[0], stride=heads), :] = x[:, h * HEAD_DIM:(h + 1) * HEAD_DIM]


def _dense_in_kernel(x_ref, n1_ref, wg_ref, wu_ref, wd_ref, n2_ref, wp_ref, bf_ref, *refs):
    (x1_ref, qf_ref, kf_ref, vf_ref, lfp_ref, u_ref, qs_ref, ks_ref, vs_ref,
     kfb_ref, vfb_ref, ksb_ref, vsb_ref, lf_ref) = refs[-14:]
    x1 = _ffn_half(x_ref[...], n1_ref, wg_ref, wu_ref, wd_ref)
    x1_ref[...] = x1
    h = _rms(x1, n2_ref[...]).astype(BF16)
    z = _dot(h, wp_ref[...])
    scale = HEAD_DIM ** -0.5 * LOG2E
    qf_ref[...] = (z[:, ZQ:ZK] * scale).astype(BF16)
    kf, vf = z[:, ZK:ZV], z[:, ZV:ZF]
    _store_heads(kf_ref, kf)
    _store_heads(vf_ref, vf)
    kfb_ref[...] = kf.astype(BF16)
    vfb_ref[...] = vf.astype(BF16)
    zf = z[:, ZF:ZA] + bf_ref[...]
    lane = lax.broadcasted_iota(jnp.int32, zf.shape, 1)
    lf = jnp.where(lane < H_FOX, -_softplus(-zf), 0.0)
    lfp_ref[...] = lf
    lf_ref[...] = lf[:, :H_FOX]
    u_ref[...] = z[:, ZA:ZG] * _sigmoid(z[:, ZG:ZQS])
    qs_ref[...] = (z[:, ZQS:ZKS] * scale).astype(BF16)
    ks, vs = z[:, ZKS:ZVS], z[:, ZVS:ZEND]
    _store_heads(ks_ref, ks)
    _store_heads(vs_ref, vs)
    ksb_ref[...] = ks.astype(BF16)
    vsb_ref[...] = vs.astype(BF16)


def _const_spec(shape):
    return pl.BlockSpec(shape, lambda *_: (0,) * len(shape), pipeline_mode=pl.Buffered(1))


STACKED = {2: (H_FOX, HEAD_DIM), 3: (H_FOX, HEAD_DIM), 7: (H_SB, HEAD_DIM), 8: (H_SB, HEAD_DIM), 13: (1, H_FOX)}


def _dense_in(x, n1, wg, wu, wd, n2, wp, bfp, tm, layer, depth, stacked):
    n = x.shape[0]
    nt = n // tm
    widths = (D_MODEL, D_FOX, D_FOX, D_FOX, LANES, C_CONV, D_SB, D_SB, D_SB, D_FOX, D_FOX, D_SB, D_SB, H_FOX)
    dtypes = (F32, BF16, F32, F32, F32, F32, BF16, F32, F32, BF16, BF16, BF16, BF16, F32)
    specs, shapes = [], []
    for k, (w, d) in enumerate(zip(widths, dtypes)):
        if k in STACKED:
            per_token, width = STACKED[k]
            specs.append(pl.BlockSpec((tm * per_token, width), lambda i: (i + layer * nt, 0)))
            shapes.append(jax.ShapeDtypeStruct((depth * n * per_token, width), d))
        else:
            specs.append(pl.BlockSpec((tm, w), lambda i: (i, 0)))
            shapes.append(jax.ShapeDtypeStruct((n, w), d))
    prev = () if stacked is None else tuple(stacked)
    n_fixed = 8
    return pl.pallas_call(
        _dense_in_kernel,
        grid=(nt,),
        in_specs=[pl.BlockSpec((tm, D_MODEL), lambda i: (i, 0)), _const_spec((1, D_MODEL)),
                  _const_spec((D_MODEL, D_FF)), _const_spec((D_MODEL, D_FF)), _const_spec((D_FF, D_MODEL)),
                  _const_spec((1, D_MODEL)), _const_spec((D_MODEL, ZEND)), _const_spec((1, LANES))]
                 + [pl.BlockSpec(memory_space=pl.ANY)] * len(prev),
        out_specs=specs,
        out_shape=shapes,
        input_output_aliases={n_fixed + a: k for a, k in enumerate(STACKED)} if prev else {},
        compiler_params=pltpu.CompilerParams(dimension_semantics=("arbitrary",),
                                             vmem_limit_bytes=VMEM_LIMIT),
        name="dense_in",
    )(x, n1, wg, wu, wd, n2, wp, bfp, *prev)


def _dense_out_kernel(x1_ref, yf_ref, yc_ref, ys_ref, wo_ref, n_ref, wg_ref, wu_ref, wd_ref, fn_ref,
                      o_ref, *, final):
    x2 = (x1_ref[...] + _dot(yf_ref[...], wo_ref[0:D_FOX, :])
          + _dot(yc_ref[...], wo_ref[D_FOX:D_FOX + C_CONV, :])
          + _dot(ys_ref[...], wo_ref[D_FOX + C_CONV:, :]))
    x3 = _ffn_half(x2, n_ref, wg_ref, wu_ref, wd_ref)
    o_ref[...] = _rms(x3, fn_ref[...]) if final else x3


def _dense_out(x1, yf, yc, ys, wo, n, wg, wu, wd, fn, tm, final):
    nrow = x1.shape[0]
    row = lambda w: pl.BlockSpec((tm, w), lambda i: (i, 0))
    return pl.pallas_call(
        functools.partial(_dense_out_kernel, final=final),
        grid=(nrow // tm,),
        in_specs=[row(D_MODEL), row(D_FOX), row(C_CONV), row(D_SB), _const_spec((D_MODEL, D_MODEL)),
                  _const_spec((1, D_MODEL)), _const_spec((D_MODEL, D_FF)), _const_spec((D_MODEL, D_FF)),
                  _const_spec((D_FF, D_MODEL)), _const_spec((1, D_MODEL))],
        out_specs=row(D_MODEL),
        out_shape=jax.ShapeDtypeStruct((nrow, D_MODEL), F32),
        compiler_params=pltpu.CompilerParams(dimension_semantics=("arbitrary",),
                                             vmem_limit_bytes=VMEM_LIMIT),
        name="dense_out",
    )(x1, yf, yc, ys, wo, n, wg, wu, wd, fn)


def _prep_kernel(kf_ref, vf_ref, vs_ref, lfp_ref, tri_ref, sel_ref, hsel_ref,
                 kt_ref, vft_ref, vst_ref, bias_max_ref, knorm_max_ref, carry_ref, bmax_ref, kmax_ref):
    i = pl.program_id(1)

    @pl.when(i == 0)
    def _():
        carry_ref[...] = jnp.zeros_like(carry_ref)
        bmax_ref[...] = jnp.full_like(bmax_ref, -jnp.inf)
        kmax_ref[...] = jnp.zeros_like(kmax_ref)

    tri = tri_ref[...]
    cum = carry_ref[...]
    for part in _split3(lfp_ref[0]):
        cum = cum + _dot(tri, part)
    carry_ref[...] = cum[cum.shape[0] - 1:, :]
    bias = -cum * LOG2E
    aug = None
    for s, part in enumerate(_split3(bias)):
        term = _dot(part, sel_ref[s])
        aug = term if aug is None else aug + term
    kb = kf_ref[0]
    for hp in range(H_FOX // 2):
        base = 2 * LANES * hp
        kt_ref[0, :, base:base + LANES] = kb[:, hp * LANES:(hp + 1) * LANES]
        kt_ref[0, :, base + LANES:base + 2 * LANES] = aug[:, hp * LANES:(hp + 1) * LANES].astype(BF16)
    vft_ref[0] = jnp.transpose(vf_ref[0].astype(F32)).astype(BF16)
    vst_ref[0] = jnp.transpose(vs_ref[0].astype(F32)).astype(BF16)

    ksq = kb.astype(F32)
    ksq = ksq * ksq
    hi = ksq.astype(BF16)
    lo = (ksq - hi.astype(F32)).astype(BF16)
    norm2 = _dot(hi, hsel_ref[...]) + _dot(lo, hsel_ref[...])
    kmax = jnp.maximum(kmax_ref[...], jnp.max(norm2, axis=0, keepdims=True))
    bmax = jnp.maximum(bmax_ref[...], jnp.max(bias, axis=0, keepdims=True))
    kmax_ref[...] = kmax
    bmax_ref[...] = bmax
    knorm_max_ref[0, pl.ds(i, 1), :] = jnp.sqrt(kmax) * NORM_SLACK
    bias_max_ref[0, pl.ds(i, 1), :] = bmax


def _prep(kf, vf, vs, lfp, tp):
    b, tk, _ = lfp.shape
    nblk = tk // tp
    r = jnp.arange(tp)
    tri = (r[None, :] <= r[:, None]).astype(BF16)
    h = jnp.arange(LANES)
    col = jnp.arange(D_FOX)
    sel = jnp.stack([(col[None, :] == ((h // 2) * LANES + (h % 2) * N_SPLIT + s)[:, None])
                     & (h[:, None] < H_FOX) for s in range(N_SPLIT)]).astype(BF16)
    hsel = (col[:, None] // HEAD_DIM == h[None, :]).astype(BF16)
    blk = lambda w: pl.BlockSpec((1, tp, w), lambda bi, i: (bi, i, 0))
    blk_t = lambda w: pl.BlockSpec((1, w, tp), lambda bi, i: (bi, 0, i))
    per_batch = pl.BlockSpec((1, nblk, LANES), lambda bi, i: (bi, 0, 0))
    return pl.pallas_call(
        _prep_kernel,
        grid=(b, nblk),
        in_specs=[blk(D_FOX), blk(D_FOX), blk(D_SB), blk(LANES), _const_spec((tp, tp)),
                  _const_spec((N_SPLIT, LANES, D_FOX)), _const_spec((D_FOX, LANES))],
        out_specs=[blk(2 * D_FOX), blk_t(D_FOX), blk_t(D_SB), per_batch, per_batch],
        out_shape=[jax.ShapeDtypeStruct((b, tk, 2 * D_FOX), BF16),
                   jax.ShapeDtypeStruct((b, D_FOX, tk), BF16),
                   jax.ShapeDtypeStruct((b, D_SB, tk), BF16),
                   jax.ShapeDtypeStruct((b, nblk, LANES), F32),
                   jax.ShapeDtypeStruct((b, nblk, LANES), F32)],
        scratch_shapes=[pltpu.VMEM((1, LANES), F32)] * 3,
        compiler_params=pltpu.CompilerParams(dimension_semantics=("arbitrary", "arbitrary"),
                                             vmem_limit_bytes=VMEM_LIMIT),
        name="prep",
    )(kf, vf, vs, lfp, tri, sel, hsel)


def _conv_kernel(u_ref, prev_ref, buf_ref, w_ref, b_ref, g_ref, beta_ref, y_ref, xw_ref, sh_ref, *, tt, rows):
    first = pl.program_id(1) == 0
    xw_ref[0:HALO, :] = jnp.where(first, buf_ref[0], prev_ref[0])
    xw_ref[HALO:HALO + tt, :] = u_ref[0]
    off = HALO - (CONV_W - 1)
    for s in range(1, SUBLANES):
        sh_ref[s - 1] = xw_ref[s:s + tt + HALO - SUBLANES, :]
    for r0 in range(0, tt, rows):
        acc = jnp.zeros((rows, C_CONV), F32)
        for j in range(CONV_W):
            s, base = (j + off) % SUBLANES, (j + off) // SUBLANES * SUBLANES
            src = xw_ref if s == 0 else sh_ref.at[s - 1]
            acc = acc + w_ref[j:j + 1, :] * src[base + r0:base + r0 + rows, :]
        y = acc + b_ref[...]
        mu = jnp.mean(y, axis=-1, keepdims=True)
        yc = y - mu
        var = jnp.mean(yc * yc, axis=-1, keepdims=True)
        y = yc * lax.rsqrt(var + EPS) * g_ref[...] + beta_ref[...]
        y_ref[0, r0:r0 + rows, :] = (y * _sigmoid(y)).astype(BF16)


def _conv(u, buf_pad, w, b, g, beta, tt):
    bsz, t, _ = u.shape
    rows = min(tt, CONV_CHUNK)
    per = tt // HALO
    return pl.pallas_call(
        functools.partial(_conv_kernel, tt=tt, rows=rows),
        grid=(bsz, t // tt),
        in_specs=[pl.BlockSpec((1, tt, C_CONV), lambda bi, i: (bi, i, 0)),
                  pl.BlockSpec((1, HALO, C_CONV), lambda bi, i: (bi, jnp.maximum(i * per - 1, 0), 0)),
                  pl.BlockSpec((1, HALO, C_CONV), lambda bi, i: (bi, 0, 0)),
                  _const_spec((CONV_W, C_CONV)), _const_spec((1, C_CONV)), _const_spec((1, C_CONV)),
                  _const_spec((1, C_CONV))],
        out_specs=pl.BlockSpec((1, tt, C_CONV), lambda bi, i: (bi, i, 0)),
        out_shape=jax.ShapeDtypeStruct((bsz, t, C_CONV), BF16),
        scratch_shapes=[pltpu.VMEM((HALO + tt, C_CONV), F32),
                        pltpu.VMEM((SUBLANES - 1, HALO + tt - SUBLANES, C_CONV), F32)],
        compiler_params=pltpu.CompilerParams(dimension_semantics=("arbitrary", "arbitrary")),
        name="conv",
    )(u, u, buf_pad, w, b, g, beta)


def _query_heads(q_ref):
    qt = jnp.transpose(q_ref[0].astype(F32))
    row = lax.broadcasted_iota(jnp.int32, qt.shape, 0)
    return row, jnp.where(row < HEAD_DIM, qt, 0.0), jnp.where(row >= HEAD_DIM, qt, 0.0)


def _diag_block(i, tq, tk, past):
    return (past + i * tq + tq - 1) // tk


def _positions(i, tq, tk, past):
    key_pos = lax.broadcasted_iota(jnp.int32, (tk, tq), 0)
    q_pos = past + i * tq + lax.broadcasted_iota(jnp.int32, (tk, tq), 1)
    return key_pos, q_pos


def _merge_heads(out_a, out_b):
    return jnp.transpose(jnp.concatenate([out_a, out_b], axis=0))


def _pipelined_descend(i, nq, diag, start, clear, first, second, finish, reaches, state_ref):
    has_new = i < nq
    jd = diag(jnp.minimum(i, nq - 1))

    @pl.when(i == 0)
    def _():
        start()
        clear()
        first(jd, True, 0)
        state_ref[0] = reaches(jd - 1).astype(jnp.int32)
        state_ref[2] = 1

    for slot in range(2):
        waited = jnp.logical_and(i > 0, state_ref[2] == slot)

        @pl.when(jnp.logical_and(waited, has_new))
        def _():
            start()
            first(jd, True, 1 - slot)
            state_ref[0] = reaches(jd - 1).astype(jnp.int32)
            second(state_ref[1], slot)
            finish()
            clear()

        @pl.when(jnp.logical_and(waited, jnp.logical_not(has_new)))
        def _():
            second(state_ref[1], slot)
            finish()

    base = 1 - state_ref[2]
    waiting = lambda j: (base + jd - j - 1) & 1

    def cond(c):
        j, go = c
        return jnp.logical_and(j >= 0, go)

    def body(c):
        j, _ = c
        for slot in range(2):
            @pl.when(waiting(j) == slot)
            def _():
                first(j, False, 1 - slot)
                state_ref[0] = reaches(j - 1).astype(jnp.int32)
                second(j + 1, slot)
        return j - 1, state_ref[0] != 0

    go0 = jnp.logical_and(has_new, state_ref[0] != 0)
    j_stop, _ = lax.while_loop(cond, body, (jd - 1, go0))
    state_ref[1] = j_stop + 1
    state_ref[2] = waiting(j_stop)


def _fox_kernel(bias_max_ref, knorm_max_ref, q_ref, k_ref, vt_ref, o_ref,
                m_ref, acc_ref, s_ref, mblk_ref, alpha_ref, state_ref, *, tq, tk, past, nblk, nq):
    bi = pl.program_id(0)
    hg = pl.program_id(1)
    step = pl.program_id(2)
    i = jnp.minimum(step, nq - 1)
    q_heads, q_norms = [], []
    for g in range(FOX_HEADS // 2):
        row, qa, qb = _query_heads(q_ref.at[:, :, g * LANES:(g + 1) * LANES])
        ones_a = jnp.where(row < N_SPLIT, 1.0, 0.0)
        ones_b = jnp.where(row < 2 * N_SPLIT, 1.0, 0.0) - ones_a
        q_heads += [jnp.concatenate([qa, ones_a], axis=0).astype(BF16),
                    jnp.concatenate([qb, ones_b], axis=0).astype(BF16)]
        q_norms += [jnp.sqrt(jnp.sum(qh * qh, axis=0, keepdims=True)) * NORM_SLACK for qh in (qa, qb)]
    key_pos, q_pos = _positions(i, tq, tk, past)

    def start():
        m_ref[...] = jnp.full_like(m_ref, -jnp.inf)

    def clear():
        acc_ref[...] = jnp.zeros_like(acc_ref)

    def score(j, masked, slot):
        start = pl.multiple_of(j * tk, tk)
        scores = [_dot(k_ref[0, pl.ds(start, tk), (h // 2) * 2 * LANES:(h // 2 + 1) * 2 * LANES], q_heads[h])
                  for h in range(FOX_HEADS)]
        for h in range(FOX_HEADS):
            s = scores[h]
            if masked:
                s = jnp.where(start + key_pos <= q_pos, s, -jnp.inf)
            m_prev = m_ref[h]
            m_new = jnp.maximum(m_prev, jnp.max(s, axis=0, keepdims=True))
            alpha_ref[slot, h] = jnp.exp2(m_prev - m_new)
            mblk_ref[slot, h] = m_new
            m_ref[h] = m_new
            s_ref[slot, h] = s

    def accumulate(j, slot):
        start = pl.multiple_of(j * tk, tk)
        ones = jnp.ones((ONES_ROWS, tk), BF16)
        for h in range(FOX_HEADS):
            p = jnp.exp2(s_ref[slot, h] - mblk_ref[slot, h])
            vt = vt_ref[0, h * HEAD_DIM:(h + 1) * HEAD_DIM, pl.ds(start, tk)]
            pv = _dot(jnp.concatenate([vt, ones], axis=0), p.astype(BF16))
            acc_ref[h] = alpha_ref[slot, h] * acc_ref[h] + pv

    def reaches(j):
        base = (bi * nblk + jnp.maximum(j, 0)) * H_FOX + FOX_HEADS * hg
        worst = None
        for h in range(FOX_HEADS):
            top = q_norms[h] * knorm_max_ref[base + h] + bias_max_ref[base + h] - m_ref[h]
            worst = top if worst is None else jnp.maximum(worst, top)
        return jnp.max(worst) >= -SKIP_LOG2

    def finish():
        for g in range(FOX_HEADS // 2):
            o_ref[0, :, g * LANES:(g + 1) * LANES] = _merge_heads(
                *[acc_ref[h, :HEAD_DIM] / acc_ref[h, HEAD_DIM:HEAD_DIM + 1]
                  for h in (2 * g, 2 * g + 1)]).astype(BF16)

    _pipelined_descend(step, nq, lambda qi: _diag_block(qi, tq, tk, past), start, clear,
                       score, accumulate, finish, reaches, state_ref)


def _fox(q, kt, vt, bias_max, knorm_max, past, tq, tk):
    b, t, _ = q.shape
    tkeys = kt.shape[1]
    nblk = tkeys // tk
    nq = t // tq
    assert (tk % tq == 0 and past % tk == 0) or tk == tkeys
    smem = pl.BlockSpec(memory_space=pltpu.SMEM)
    per_head = lambda a: a[:, :, :H_FOX].reshape(-1)
    width = FOX_HEADS * HEAD_DIM
    resident = lambda shape, imap: pl.BlockSpec(
        shape, imap, pipeline_mode=pl.Buffered(1 if nblk > 1 else 2))
    return pl.pallas_call(
        functools.partial(_fox_kernel, tq=tq, tk=tk, past=past, nblk=nblk, nq=nq),
        grid=(b, H_FOX // FOX_HEADS, nq + 1),
        in_specs=[smem, smem,
                  pl.BlockSpec((1, tq, width), lambda bi, hg, i: (bi, jnp.minimum(i, nq - 1), hg)),
                  resident((1, tkeys, 2 * width), lambda bi, hg, i: (bi, 0, hg)),
                  resident((1, width, tkeys), lambda bi, hg, i: (bi, hg, 0))],
        out_specs=pl.BlockSpec((1, tq, width), lambda bi, hg, i: (bi, jnp.maximum(i - 1, 0), hg)),
        out_shape=jax.ShapeDtypeStruct((b, t, D_FOX), BF16),
        scratch_shapes=[pltpu.VMEM((FOX_HEADS, 1, tq), F32),
                        pltpu.VMEM((FOX_HEADS, HEAD_DIM + ONES_ROWS, tq), F32),
                        pltpu.VMEM((2, FOX_HEADS, tk, tq), F32),
                        pltpu.VMEM((2, FOX_HEADS, 1, tq), F32), pltpu.VMEM((2, FOX_HEADS, 1, tq), F32),
                        pltpu.SMEM((3,), jnp.int32)],
        compiler_params=pltpu.CompilerParams(dimension_semantics=("arbitrary",) * 3,
                                             vmem_limit_bytes=VMEM_LIMIT),
        name="fox",
    )(per_head(bias_max), per_head(knorm_max), q, kt, vt)


def _sb_kernel(q_ref, k_ref, vt_ref, tri_ref, o_ref, carry_ref, acc_ref, t_ref, later_ref, state_ref,
               *, tq, tk, past, nq):
    step = pl.program_id(1)
    i = jnp.minimum(step, nq - 1)
    q_heads = []
    for g in range(H_SB // 2):
        _, qa, qb = _query_heads(q_ref.at[:, :, g * LANES:(g + 1) * LANES])
        q_heads += [qa.astype(BF16), qb.astype(BF16)]
    key_pos, q_pos = _positions(i, tq, tk, past)

    def start():
        carry_ref[...] = jnp.zeros_like(carry_ref)

    def clear():
        acc_ref[...] = jnp.zeros_like(acc_ref)

    def logits(j, masked, slot):
        start = pl.multiple_of(j * tk, tk)
        tri2 = tri_ref[...]
        scores = [_dot(k_ref[0, pl.ds(start, tk), (h // 2) * LANES:(h // 2 + 1) * LANES], q_heads[h])
                  for h in range(H_SB)]
        for h in range(H_SB):
            z = scores[h]
            nl = jnp.maximum(z, 0.0) + jnp.log(1.0 + jnp.exp2(-jnp.abs(z))) * LOG2E
            if masked:
                valid = start + key_pos < q_pos
                nl = jnp.where(valid, nl, 0.0)
            hi = nl.astype(BF16)
            lo = (nl - hi.astype(F32)).astype(BF16)
            suffix = _dot(tri2, jnp.concatenate([hi, lo], axis=0))
            t = z - nl - suffix
            t_ref[slot, h] = jnp.where(valid, t, -jnp.inf) if masked else t
            later = carry_ref[h]
            later_ref[slot, h] = later
            carry_ref[h] = later + jnp.sum(nl, axis=0, keepdims=True)

    def weigh(j, slot):
        start = pl.multiple_of(j * tk, tk)
        for h in range(H_SB):
            a = jnp.exp2(t_ref[slot, h] - later_ref[slot, h])
            vt = vt_ref[0, h * HEAD_DIM:(h + 1) * HEAD_DIM, pl.ds(start, tk)]
            acc_ref[h] = acc_ref[h] + _dot(vt, a.astype(BF16))

    def reaches(j):
        least = carry_ref[0]
        for h in range(1, H_SB):
            least = jnp.minimum(least, carry_ref[h])
        return jnp.min(least) <= SKIP_LOG2

    def finish():
        for g in range(H_SB // 2):
            o_ref[0, :, g * LANES:(g + 1) * LANES] = _merge_heads(
                acc_ref[2 * g], acc_ref[2 * g + 1]).astype(BF16)

    _pipelined_descend(step, nq, lambda qi: _diag_block(qi, tq, tk, past), start, clear,
                       logits, weigh, finish, reaches, state_ref)


def _sb(q, k, vt, past, tq, tk):
    b, t, _ = q.shape
    tkeys = k.shape[1]
    nq = t // tq
    assert (tk % tq == 0 and past % tk == 0) or tk == tkeys
    r = jnp.arange(tk)
    tri = (r[None, :] > r[:, None]).astype(BF16)
    tri2 = jnp.concatenate([tri, tri], axis=1)
    per_batch = lambda shape: pl.BlockSpec(shape, lambda bi, i: (bi, 0, 0), pipeline_mode=pl.Buffered(1))
    return pl.pallas_call(
        functools.partial(_sb_kernel, tq=tq, tk=tk, past=past, nq=nq),
        grid=(b, nq + 1),
        in_specs=[pl.BlockSpec((1, tq, D_SB), lambda bi, i: (bi, jnp.minimum(i, nq - 1), 0)),
                  per_batch((1, tkeys, D_SB)), per_batch((1, D_SB, tkeys)), _const_spec((tk, 2 * tk))],
        out_specs=pl.BlockSpec((1, tq, D_SB), lambda bi, i: (bi, jnp.maximum(i - 1, 0), 0)),
        out_shape=jax.ShapeDtypeStruct((b, t, D_SB), BF16),
        scratch_shapes=[pltpu.VMEM((H_SB, 1, tq), F32), pltpu.VMEM((H_SB, HEAD_DIM, tq), F32),
                        pltpu.VMEM((2, H_SB, tk, tq), F32), pltpu.VMEM((2, H_SB, 1, tq), F32),
                        pltpu.SMEM((3,), jnp.int32)],
        compiler_params=pltpu.CompilerParams(dimension_semantics=("arbitrary",) * 2,
                                             vmem_limit_bytes=VMEM_LIMIT),
        name="sb",
    )(q, k, vt, tri2)


def _pick_tile(n, pref):
    t = min(n, pref)
    while n % t:
        t //= 2
    return t


def _layer(x, past, w, final_norm, layer, depth, stacked):
    b, t, _ = x.shape
    past_k, past_v, past_lf, past_sk, past_sv, conv_buf = past
    p = past_k.shape[1]
    assert t >= CONV_W - 1 and t % HALO == 0
    n = b * t
    tm = _pick_tile(n, DENSE_ROWS)

    (x1, qf, kf, vf, lfp, u, qs, ks, vs, kfb, vfb, ksb, vsb, lf) = _dense_in(
        x.reshape(n, D_MODEL), w["n1"], w["wg1"], w["wu1"], w["wd1"], w["n2"], w["wp"], w["bfp"], tm,
        layer, depth, stacked)

    r3 = lambda a: a.reshape(-1, t, a.shape[-1])
    lfp3, u3 = r3(lfp), r3(u)
    if p:
        fill = -(p + t) % LANES
        cat = lambda old, new: jnp.concatenate(
            [old.reshape(b, p, -1).astype(new.dtype), new,
             jnp.zeros((b, fill, new.shape[-1]), new.dtype)], axis=1)
        kf_all, vf_all, vs_all = cat(past_k, r3(kfb)), cat(past_v, r3(vfb)), cat(past_sv, r3(vsb))
        lf_all = cat(jnp.pad(past_lf, ((0, 0), (0, 0), (0, LANES - H_FOX))), lfp3)
        ks_all = cat(past_sk, r3(ksb))
        tq_fox = tq_sb = t
        tk_fox = p + t + fill
        tk_sb = LANES if (LANES % t == 0 and p % LANES == 0) else tk_fox
    else:
        kf_all, vf_all, vs_all, lf_all, ks_all = r3(kfb), r3(vfb), r3(vsb), lfp3, r3(ksb)
        tq_sb = tk_sb = _pick_tile(t, SB_BLOCK)
        tq_fox = tk_fox = _pick_tile(t, FOX_BLOCK)

    kt, vft, vst, bias_max, knorm_max = _prep(kf_all, vf_all, vs_all, lf_all, tk_fox)
    yf = _fox(r3(qf), kt, vft, bias_max, knorm_max, p, tq_fox, tk_fox)
    ys = _sb(r3(qs), ks_all, vst, p, tq_sb, tk_sb)
    buf_pad = jnp.pad(conv_buf, ((0, 0), (HALO - (CONV_W - 1), 0), (0, 0)))
    yc = _conv(u3, buf_pad, w["conv_w"], w["conv_b"], w["ln_g"], w["ln_b"], _pick_tile(t, CONV_ROWS))

    flat = lambda a: a.reshape(n, a.shape[-1])
    xo = _dense_out(x1, flat(yf), flat(yc), flat(ys), w["wo"], w["n3"], w["wg2"], w["wu2"], w["wd2"],
                    final_norm, tm, layer == depth - 1)
    return xo.reshape(b, t, D_MODEL), (kf, vf, ks, vs, lf), u3[:, t - (CONV_W - 1):, :]


def _layer_weights(l, ffn_norm, ffn_gate, ffn_up, ffn_down, mix_norm, w_in, b_forget,
                   conv_w, conv_b, conv_ln_g, conv_ln_b, w_out):
    wi = w_in[l]
    off_f = 3 * D_FOX
    off_glu = off_f + H_FOX
    off_qc = off_glu + 2 * C_CONV
    wp = jnp.concatenate([wi[:, :off_f], jnp.pad(wi[:, off_f:off_glu], ((0, 0), (0, LANES - H_FOX))),
                          wi[:, off_glu:off_qc], wi[:, off_qc:]], axis=1).astype(BF16)
    row = lambda a: a.reshape(1, -1).astype(F32)
    return dict(
        n1=row(ffn_norm[l, 0]), wg1=ffn_gate[l, 0].astype(BF16), wu1=ffn_up[l, 0].astype(BF16),
        wd1=ffn_down[l, 0].astype(BF16), n2=row(mix_norm[l]), wp=wp,
        bfp=jnp.pad(row(b_forget[l]), ((0, 0), (0, LANES - H_FOX))),
        conv_w=conv_w[l], conv_b=row(conv_b[l]), ln_g=row(conv_ln_g[l]), ln_b=row(conv_ln_b[l]),
        wo=w_out[l].astype(BF16), n3=row(ffn_norm[l, 1]), wg2=ffn_gate[l, 1].astype(BF16),
        wu2=ffn_up[l, 1].astype(BF16), wd2=ffn_down[l, 1].astype(BF16))


def kernel(x_prompt, x_sample, cache_fox_k, cache_fox_v, cache_fox_logf, cache_sb_k, cache_sb_v, state_conv,
           ffn_norm, ffn_gate, ffn_up, ffn_down, mix_norm, w_in, b_forget, conv_w, conv_b, conv_ln_g,
           conv_ln_b, w_out, final_norm):
    depth = w_in.shape[0]
    bp = x_prompt.shape[0]
    dt = x_prompt.dtype
    empty = (jnp.zeros((bp, 0, H_FOX, HEAD_DIM), dt), jnp.zeros((bp, 0, H_FOX, HEAD_DIM), dt),
             jnp.zeros((bp, 0, H_FOX), dt), jnp.zeros((bp, 0, H_SB, HEAD_DIM), dt),
             jnp.zeros((bp, 0, H_SB, HEAD_DIM), dt), jnp.zeros((bp, CONV_W - 1, C_CONV), dt))
    fn = final_norm.reshape(1, -1).astype(F32)
    xp, xs = x_prompt, x_sample
    stacked_p = stacked_s = None
    conv_p, conv_s = [], []
    for l in range(depth):
        w = _layer_weights(l, ffn_norm, ffn_gate, ffn_up, ffn_down, mix_norm, w_in, b_forget,
                           conv_w, conv_b, conv_ln_g, conv_ln_b, w_out)
        xp, stacked_p, cp = _layer(xp, empty, w, fn, l, depth, stacked_p)
        cache = (cache_fox_k[l], cache_fox_v[l], cache_fox_logf[l], cache_sb_k[l], cache_sb_v[l],
                 state_conv[l])
        xs, stacked_s, cs = _layer(xs, cache, w, fn, l, depth, stacked_s)
        conv_p.append(cp)
        conv_s.append(cs)

    def outputs(x, stacked, conv):
        b, t, _ = x.shape
        heads = lambda a: a.reshape(depth, b, t, -1, HEAD_DIM)
        kf, vf, ks, vs, lf = stacked
        return heads(kf), heads(vf), lf.reshape(depth, b, t, H_FOX), heads(ks), heads(vs), jnp.stack(conv, 0)

    return (xp, xs) + outputs(xp, stacked_p, conv_p) + outputs(xs, stacked_s, conv_s)
```

```python
import functools

import jax
import jax.numpy as jnp
from jax import lax
from jax.experimental import pallas as pl
from jax.experimental.pallas import tpu as pltpu

D_MODEL = 1024
HEAD_DIM = 64
D_FOX = 512
C_CONV = 256
D_SB = 256
H_FOX = 8
H_SB = 4
CONV_W = 31
D_FF = 2816
EPS = 1e-6

LANES = 128
SUBLANES = 8
HALO = 32
ONES_ROWS = 16
FOX_HEADS = 4
DENSE_ROWS = 512
FOX_BLOCK = 512
SB_BLOCK = 256
CONV_ROWS = 512
CONV_CHUNK = 64
N_SPLIT = 3
VMEM_LIMIT = 56 * 1024 * 1024
SKIP_LOG = 110.0
LOG2E = 1.4426950408889634
SKIP_LOG2 = SKIP_LOG * LOG2E
NORM_SLACK = 1.01

ZQ, ZK, ZV, ZF, ZA, ZG, ZQS, ZKS, ZVS, ZEND = 0, 512, 1024, 1536, 1664, 1920, 2176, 2432, 2688, 2944

F32 = jnp.float32
BF16 = jnp.bfloat16


def _dot(a, b):
    return jnp.dot(a, b, preferred_element_type=F32)


def _sigmoid(x):
    return 1.0 / (1.0 + jnp.exp(-x))


def _softplus(x):
    return jnp.maximum(x, 0.0) + jnp.log1p(jnp.exp(-jnp.abs(x)))


def _rms(x, g):
    return x * lax.rsqrt(jnp.mean(x * x, axis=-1, keepdims=True) + EPS) * g


def _ffn_half(x, g_ref, wg_ref, wu_ref, wd_ref):
    hn = _rms(x, g_ref[...]).astype(BF16)
    g = _dot(hn, wg_ref[...])
    u = _dot(hn, wu_ref[...])
    act = (g * _sigmoid(g) * u).astype(BF16)
    return x + 0.5 * _dot(act, wd_ref[...])


def _split3(x):
    a = x.astype(BF16)
    r = x - a.astype(F32)
    b = r.astype(BF16)
    c = (r - b.astype(F32)).astype(BF16)
    return a, b, c


def _store_heads(o_ref, x):
    heads = x.shape[1] // HEAD_DIM
    for h in range(heads):
        o_ref[pl.ds(h, x.shape[0], stride=heads), :] = x[:, h * HEAD_DIM:(h + 1) * HEAD_DIM]


def _dense_in_kernel(x_ref, n1_ref, wg_ref, wu_ref, wd_ref, n2_ref, wp_ref, bf_ref, *refs):
    (x1_ref, qf_ref, kf_ref, vf_ref, lfp_ref, u_ref, qs_ref, ks_ref, vs_ref,
     kfb_ref, vfb_ref, ksb_ref, vsb_ref, lf_ref) = refs[-14:]
    x1 = _ffn_half(x_ref[...], n1_ref, wg_ref, wu_ref, wd_ref)
    x1_ref[...] = x1
    h = _rms(x1, n2_ref[...]).astype(BF16)
    z = _dot(h, wp_ref[...])
    scale = HEAD_DIM ** -0.5 * LOG2E
    qf_ref[...] = (z[:, ZQ:ZK] * scale).astype(BF16)
    kf, vf = z[:, ZK:ZV], z[:, ZV:ZF]
    _store_heads(kf_ref, kf)
    _store_heads(vf_ref, vf)
    kfb_ref[...] = kf.astype(BF16)
    vfb_ref[...] = vf.astype(BF16)
    zf = z[:, ZF:ZA] + bf_ref[...]
    lane = lax.broadcasted_iota(jnp.int32, zf.shape, 1)
    lf = jnp.where(lane < H_FOX, -_softplus(-zf), 0.0)
    lfp_ref[...] = lf
    lf_ref[...] = lf[:, :H_FOX]
    u_ref[...] = z[:, ZA:ZG] * _sigmoid(z[:, ZG:ZQS])
    qs_ref[...] = (z[:, ZQS:ZKS] * scale).astype(BF16)
    ks, vs = z[:, ZKS:ZVS], z[:, ZVS:ZEND]
    _store_heads(ks_ref, ks)
    _store_heads(vs_ref, vs)
    ksb_ref[...] = ks.astype(BF16)
    vsb_ref[...] = vs.astype(BF16)


def _const_spec(shape):
    return pl.BlockSpec(shape, lambda *_: (0,) * len(shape), pipeline_mode=pl.Buffered(1))


STACKED = {2: (H_FOX, HEAD_DIM), 3: (H_FOX, HEAD_DIM), 7: (H_SB, HEAD_DIM), 8: (H_SB, HEAD_DIM), 13: (1, H_FOX)}


def _dense_in(x, n1, wg, wu, wd, n2, wp, bfp, tm, layer, depth, stacked):
    n = x.shape[0]
    nt = n // tm
    widths = (D_MODEL, D_FOX, D_FOX, D_FOX, LANES, C_CONV, D_SB, D_SB, D_SB, D_FOX, D_FOX, D_SB, D_SB, H_FOX)
    dtypes = (F32, BF16, F32, F32, F32, F32, BF16, F32, F32, BF16, BF16, BF16, BF16, F32)
    specs, shapes = [], []
    for k, (w, d) in enumerate(zip(widths, dtypes)):
        if k in STACKED:
            per_token, width = STACKED[k]
            specs.append(pl.BlockSpec((tm * per_token, width), lambda i: (i + layer * nt, 0)))
            shapes.append(jax.ShapeDtypeStruct((depth * n * per_token, width), d))
        else:
            specs.append(pl.BlockSpec((tm, w), lambda i: (i, 0)))
            shapes.append(jax.ShapeDtypeStruct((n, w), d))
    prev = () if stacked is None else tuple(stacked)
    n_fixed = 8
    return pl.pallas_call(
        _dense_in_kernel,
        grid=(nt,),
        in_specs=[pl.BlockSpec((tm, D_MODEL), lambda i: (i, 0)), _const_spec((1, D_MODEL)),
                  _const_spec((D_MODEL, D_FF)), _const_spec((D_MODEL, D_FF)), _const_spec((D_FF, D_MODEL)),
                  _const_spec((1, D_MODEL)), _const_spec((D_MODEL, ZEND)), _const_spec((1, LANES))]
                 + [pl.BlockSpec(memory_space=pl.ANY)] * len(prev),
        out_specs=specs,
        out_shape=shapes,
        input_output_aliases={n_fixed + a: k for a, k in enumerate(STACKED)} if prev else {},
        compiler_params=pltpu.CompilerParams(dimension_semantics=("arbitrary",),
                                             vmem_limit_bytes=VMEM_LIMIT),
        name="dense_in",
    )(x, n1, wg, wu, wd, n2, wp, bfp, *prev)


def _dense_out_kernel(x1_ref, yf_ref, yc_ref, ys_ref, wo_ref, n_ref, wg_ref, wu_ref, wd_ref, fn_ref,
                      o_ref, *, final):
    x2 = (x1_ref[...] + _dot(yf_ref[...], wo_ref[0:D_FOX, :])
          + _dot(yc_ref[...], wo_ref[D_FOX:D_FOX + C_CONV, :])
          + _dot(ys_ref[...], wo_ref[D_FOX + C_CONV:, :]))
    x3 = _ffn_half(x2, n_ref, wg_ref, wu_ref, wd_ref)
    o_ref[...] = _rms(x3, fn_ref[...]) if final else x3


def _dense_out(x1, yf, yc, ys, wo, n, wg, wu, wd, fn, tm, final):
    nrow = x1.shape[0]
    row = lambda w: pl.BlockSpec((tm, w), lambda i: (i, 0))
    return pl.pallas_call(
        functools.partial(_dense_out_kernel, final=final),
        grid=(nrow // tm,),
        in_specs=[row(D_MODEL), row(D_FOX), row(C_CONV), row(D_SB), _const_spec((D_MODEL, D_MODEL)),
                  _const_spec((1, D_MODEL)), _const_spec((D_MODEL, D_FF)), _const_spec((D_MODEL, D_FF)),
                  _const_spec((D_FF, D_MODEL)), _const_spec((1, D_MODEL))],
        out_specs=row(D_MODEL),
        out_shape=jax.ShapeDtypeStruct((nrow, D_MODEL), F32),
        compiler_params=pltpu.CompilerParams(dimension_semantics=("arbitrary",),
                                             vmem_limit_bytes=VMEM_LIMIT),
        name="dense_out",
    )(x1, yf, yc, ys, wo, n, wg, wu, wd, fn)


def _prep_kernel(kf_ref, vf_ref, vs_ref, lfp_ref, tri_ref, sel_ref, hsel_ref,
                 kt_ref, vft_ref, vst_ref, bias_max_ref, knorm_max_ref, carry_ref, bmax_ref, kmax_ref):
    i = pl.program_id(1)

    @pl.when(i == 0)
    def _():
        carry_ref[...] = jnp.zeros_like(carry_ref)
        bmax_ref[...] = jnp.full_like(bmax_ref, -jnp.inf)
        kmax_ref[...] = jnp.zeros_like(kmax_ref)

    tri = tri_ref[...]
    cum = carry_ref[...]
    for part in _split3(lfp_ref[0]):
        cum = cum + _dot(tri, part)
    carry_ref[...] = cum[cum.shape[0] - 1:, :]
    bias = -cum * LOG2E
    aug = None
    for s, part in enumerate(_split3(bias)):
        term = _dot(part, sel_ref[s])
        aug = term if aug is None else aug + term
    kb = kf_ref[0]
    for hp in range(H_FOX // 2):
        base = 2 * LANES * hp
        kt_ref[0, :, base:base + LANES] = kb[:, hp * LANES:(hp + 1) * LANES]
        kt_ref[0, :, base + LANES:base + 2 * LANES] = aug[:, hp * LANES:(hp + 1) * LANES].astype(BF16)
    vft_ref[0] = jnp.transpose(vf_ref[0].astype(F32)).astype(BF16)
    vst_ref[0] = jnp.transpose(vs_ref[0].astype(F32)).astype(BF16)

    ksq = kb.astype(F32)
    ksq = ksq * ksq
    hi = ksq.astype(BF16)
    lo = (ksq - hi.astype(F32)).astype(BF16)
    norm2 = _dot(hi, hsel_ref[...]) + _dot(lo, hsel_ref[...])
    kmax = jnp.maximum(kmax_ref[...], jnp.max(norm2, axis=0, keepdims=True))
    bmax = jnp.maximum(bmax_ref[...], jnp.max(bias, axis=0, keepdims=True))
    kmax_ref[...] = kmax
    bmax_ref[...] = bmax
    knorm_max_ref[0, pl.ds(i, 1), :] = jnp.sqrt(kmax) * NORM_SLACK
    bias_max_ref[0, pl.ds(i, 1), :] = bmax


def _prep(kf, vf, vs, lfp, tp):
    b, tk, _ = lfp.shape
    nblk = tk // tp
    r = jnp.arange(tp)
    tri = (r[None, :] <= r[:, None]).astype(BF16)
    h = jnp.arange(LANES)
    col = jnp.arange(D_FOX)
    sel = jnp.stack([(col[None, :] == ((h // 2) * LANES + (h % 2) * N_SPLIT + s)[:, None])
                     & (h[:, None] < H_FOX) for s in range(N_SPLIT)]).astype(BF16)
    hsel = (col[:, None] // HEAD_DIM == h[None, :]).astype(BF16)
    blk = lambda w: pl.BlockSpec((1, tp, w), lambda bi, i: (bi, i, 0))
    blk_t = lambda w: pl.BlockSpec((1, w, tp), lambda bi, i: (bi, 0, i))
    per_batch = pl.BlockSpec((1, nblk, LANES), lambda bi, i: (bi, 0, 0))
    return pl.pallas_call(
        _prep_kernel,
        grid=(b, nblk),
        in_specs=[blk(D_FOX), blk(D_FOX), blk(D_SB), blk(LANES), _const_spec((tp, tp)),
                  _const_spec((N_SPLIT, LANES, D_FOX)), _const_spec((D_FOX, LANES))],
        out_specs=[blk(2 * D_FOX), blk_t(D_FOX), blk_t(D_SB), per_batch, per_batch],
        out_shape=[jax.ShapeDtypeStruct((b, tk, 2 * D_FOX), BF16),
                   jax.ShapeDtypeStruct((b, D_FOX, tk), BF16),
                   jax.ShapeDtypeStruct((b, D_SB, tk), BF16),
                   jax.ShapeDtypeStruct((b, nblk, LANES), F32),
                   jax.ShapeDtypeStruct((b, nblk, LANES), F32)],
        scratch_shapes=[pltpu.VMEM((1, LANES), F32)] * 3,
        compiler_params=pltpu.CompilerParams(dimension_semantics=("arbitrary", "arbitrary"),
                                             vmem_limit_bytes=VMEM_LIMIT),
        name="prep",
    )(kf, vf, vs, lfp, tri, sel, hsel)


def _conv_kernel(u_ref, prev_ref, buf_ref, w_ref, b_ref, g_ref, beta_ref, y_ref, xw_ref, sh_ref, *, tt, rows):
    first = pl.program_id(1) == 0
    xw_ref[0:HALO, :] = jnp.where(first, buf_ref[0], prev_ref[0])
    xw_ref[HALO:HALO + tt, :] = u_ref[0]
    off = HALO - (CONV_W - 1)
    for s in range(1, SUBLANES):
        sh_ref[s - 1] = xw_ref[s:s + tt + HALO - SUBLANES, :]
    for r0 in range(0, tt, rows):
        acc = jnp.zeros((rows, C_CONV), F32)
        for j in range(CONV_W):
            s, base = (j + off) % SUBLANES, (j + off) // SUBLANES * SUBLANES
            src = xw_ref if s == 0 else sh_ref.at[s - 1]
            acc = acc + w_ref[j:j + 1, :] * src[base + r0:base + r0 + rows, :]
        y = acc + b_ref[...]
        mu = jnp.mean(y, axis=-1, keepdims=True)
        yc = y - mu
        var = jnp.mean(yc * yc, axis=-1, keepdims=True)
        y = yc * lax.rsqrt(var + EPS) * g_ref[...] + beta_ref[...]
        y_ref[0, r0:r0 + rows, :] = (y * _sigmoid(y)).astype(BF16)


def _conv(u, buf_pad, w, b, g, beta, tt):
    bsz, t, _ = u.shape
    rows = min(tt, CONV_CHUNK)
    per = tt // HALO
    return pl.pallas_call(
        functools.partial(_conv_kernel, tt=tt, rows=rows),
        grid=(bsz, t // tt),
        in_specs=[pl.BlockSpec((1, tt, C_CONV), lambda bi, i: (bi, i, 0)),
                  pl.BlockSpec((1, HALO, C_CONV), lambda bi, i: (bi, jnp.maximum(i * per - 1, 0), 0)),
                  pl.BlockSpec((1, HALO, C_CONV), lambda bi, i: (bi, 0, 0)),
                  _const_spec((CONV_W, C_CONV)), _const_spec((1, C_CONV)), _const_spec((1, C_CONV)),
                  _const_spec((1, C_CONV))],
        out_specs=pl.BlockSpec((1, tt, C_CONV), lambda bi, i: (bi, i, 0)),
        out_shape=jax.ShapeDtypeStruct((bsz, t, C_CONV), BF16),
        scratch_shapes=[pltpu.VMEM((HALO + tt, C_CONV), F32),
                        pltpu.VMEM((SUBLANES - 1, HALO + tt - SUBLANES, C_CONV), F32)],
        compiler_params=pltpu.CompilerParams(dimension_semantics=("arbitrary", "arbitrary")),
        name="conv",
    )(u, u, buf_pad, w, b, g, beta)


def _query_heads(q_ref):
    qt = jnp.transpose(q_ref[0].astype(F32))
    row = lax.broadcasted_iota(jnp.int32, qt.shape, 0)
    return row, jnp.where(row < HEAD_DIM, qt, 0.0), jnp.where(row >= HEAD_DIM, qt, 0.0)


def _diag_block(i, tq, tk, past):
    return (past + i * tq + tq - 1) // tk


def _positions(i, tq, tk, past):
    key_pos = lax.broadcasted_iota(jnp.int32, (tk, tq), 0)
    q_pos = past + i * tq + lax.broadcasted_iota(jnp.int32, (tk, tq), 1)
    return key_pos, q_pos


def _merge_heads(out_a, out_b):
    return jnp.transpose(jnp.concatenate([out_a, out_b], axis=0))


def _grid_steps(nq):
    return nq + 1 if nq > 1 else 1


def _pipelined_descend(i, nq, diag, start, clear, first, second, finish, reaches, state_ref):
    has_new = i < nq
    jd = diag(jnp.minimum(i, nq - 1))

    @pl.when(i == 0)
    def _():
        start()
        clear()
        first(jd, True, 0)
        state_ref[0] = reaches(jd - 1).astype(jnp.int32)
        state_ref[2] = 1

    for slot in range(2 if nq > 1 else 0):
        waited = jnp.logical_and(i > 0, state_ref[2] == slot)

        @pl.when(jnp.logical_and(waited, has_new))
        def _():
            start()
            first(jd, True, 1 - slot)
            state_ref[0] = reaches(jd - 1).astype(jnp.int32)
            second(state_ref[1], slot)
            finish()
            clear()

        @pl.when(jnp.logical_and(waited, jnp.logical_not(has_new)))
        def _():
            second(state_ref[1], slot)
            finish()

    base = 1 - state_ref[2]
    waiting = lambda j: (base + jd - j - 1) & 1

    def cond(c):
        j, go = c
        return jnp.logical_and(j >= 0, go)

    def body(c):
        j, _ = c
        for slot in range(2):
            @pl.when(waiting(j) == slot)
            def _():
                first(j, False, 1 - slot)
                state_ref[0] = reaches(j - 1).astype(jnp.int32)
                second(j + 1, slot)
        return j - 1, state_ref[0] != 0

    go0 = jnp.logical_and(has_new, state_ref[0] != 0)
    j_stop, _ = lax.while_loop(cond, body, (jd - 1, go0))
    if nq > 1:
        state_ref[1] = j_stop + 1
        state_ref[2] = waiting(j_stop)
    else:
        for slot in range(2):
            @pl.when(waiting(j_stop) == slot)
            def _():
                second(j_stop + 1, slot)
                finish()


def _fox_kernel(bias_max_ref, knorm_max_ref, q_ref, k_ref, vt_ref, o_ref,
                m_ref, acc_ref, s_ref, mblk_ref, alpha_ref, state_ref, *, tq, tk, past, nblk, nq):
    bi = pl.program_id(0)
    hg = pl.program_id(1)
    step = pl.program_id(2)
    i = jnp.minimum(step, nq - 1)
    q_heads, q_norms = [], []
    for g in range(FOX_HEADS // 2):
        row, qa, qb = _query_heads(q_ref.at[:, :, g * LANES:(g + 1) * LANES])
        ones_a = jnp.where(row < N_SPLIT, 1.0, 0.0)
        ones_b = jnp.where(row < 2 * N_SPLIT, 1.0, 0.0) - ones_a
        q_heads += [jnp.concatenate([qa, ones_a], axis=0).astype(BF16),
                    jnp.concatenate([qb, ones_b], axis=0).astype(BF16)]
        q_norms += [jnp.sqrt(jnp.sum(qh * qh, axis=0, keepdims=True)) * NORM_SLACK for qh in (qa, qb)]
    key_pos, q_pos = _positions(i, tq, tk, past)

    def start():
        m_ref[...] = jnp.full_like(m_ref, -jnp.inf)

    def clear():
        acc_ref[...] = jnp.zeros_like(acc_ref)

    def score(j, masked, slot):
        start = pl.multiple_of(j * tk, tk)
        scores = [_dot(k_ref[0, pl.ds(start, tk), (h // 2) * 2 * LANES:(h // 2 + 1) * 2 * LANES], q_heads[h])
                  for h in range(FOX_HEADS)]
        for h in range(FOX_HEADS):
            s = scores[h]
            if masked:
                s = jnp.where(start + key_pos <= q_pos, s, -jnp.inf)
            m_prev = m_ref[h]
            m_new = jnp.maximum(m_prev, jnp.max(s, axis=0, keepdims=True))
            alpha_ref[slot, h] = jnp.exp2(m_prev - m_new)
            mblk_ref[slot, h] = m_new
            m_ref[h] = m_new
            s_ref[slot, h] = s

    def accumulate(j, slot):
        start = pl.multiple_of(j * tk, tk)
        ones = jnp.ones((ONES_ROWS, tk), BF16)
        for h in range(FOX_HEADS):
            p = jnp.exp2(s_ref[slot, h] - mblk_ref[slot, h])
            vt = vt_ref[0, h * HEAD_DIM:(h + 1) * HEAD_DIM, pl.ds(start, tk)]
            pv = _dot(jnp.concatenate([vt, ones], axis=0), p.astype(BF16))
            acc_ref[h] = alpha_ref[slot, h] * acc_ref[h] + pv

    def reaches(j):
        base = (bi * nblk + jnp.maximum(j, 0)) * H_FOX + FOX_HEADS * hg
        worst = None
        for h in range(FOX_HEADS):
            top = q_norms[h] * knorm_max_ref[base + h] + bias_max_ref[base + h] - m_ref[h]
            worst = top if worst is None else jnp.maximum(worst, top)
        return jnp.max(worst) >= -SKIP_LOG2

    def finish():
        for g in range(FOX_HEADS // 2):
            o_ref[0, :, g * LANES:(g + 1) * LANES] = _merge_heads(
                *[acc_ref[h, :HEAD_DIM] / acc_ref[h, HEAD_DIM:HEAD_DIM + 1]
                  for h in (2 * g, 2 * g + 1)]).astype(BF16)

    _pipelined_descend(step, nq, lambda qi: _diag_block(qi, tq, tk, past), start, clear,
                       score, accumulate, finish, reaches, state_ref)


def _fox(q, kt, vt, bias_max, knorm_max, past, tq, tk):
    b, t, _ = q.shape
    tkeys = kt.shape[1]
    nblk = tkeys // tk
    nq = t // tq
    assert (tk % tq == 0 and past % tk == 0) or tk == tkeys
    smem = pl.BlockSpec(memory_space=pltpu.SMEM)
    per_head = lambda a: a[:, :, :H_FOX].reshape(-1)
    width = FOX_HEADS * HEAD_DIM
    resident = lambda shape, imap: pl.BlockSpec(
        shape, imap, pipeline_mode=pl.Buffered(1 if nblk > 1 else 2))
    return pl.pallas_call(
        functools.partial(_fox_kernel, tq=tq, tk=tk, past=past, nblk=nblk, nq=nq),
        grid=(b, H_FOX // FOX_HEADS, _grid_steps(nq)),
        in_specs=[smem, smem,
                  pl.BlockSpec((1, tq, width), lambda bi, hg, i: (bi, jnp.minimum(i, nq - 1), hg)),
                  resident((1, tkeys, 2 * width), lambda bi, hg, i: (bi, 0, hg)),
                  resident((1, width, tkeys), lambda bi, hg, i: (bi, hg, 0))],
        out_specs=pl.BlockSpec((1, tq, width), lambda bi, hg, i: (bi, jnp.maximum(i - 1, 0), hg)),
        out_shape=jax.ShapeDtypeStruct((b, t, D_FOX), BF16),
        scratch_shapes=[pltpu.VMEM((FOX_HEADS, 1, tq), F32),
                        pltpu.VMEM((FOX_HEADS, HEAD_DIM + ONES_ROWS, tq), F32),
                        pltpu.VMEM((2, FOX_HEADS, tk, tq), F32),
                        pltpu.VMEM((2, FOX_HEADS, 1, tq), F32), pltpu.VMEM((2, FOX_HEADS, 1, tq), F32),
                        pltpu.SMEM((3,), jnp.int32)],
        compiler_params=pltpu.CompilerParams(dimension_semantics=("arbitrary",) * 3,
                                             vmem_limit_bytes=VMEM_LIMIT),
        name="fox",
    )(per_head(bias_max), per_head(knorm_max), q, kt, vt)


def _sb_kernel(q_ref, k_ref, vt_ref, tri_ref, o_ref, carry_ref, acc_ref, t_ref, later_ref, state_ref,
               *, tq, tk, past, nq):
    step = pl.program_id(1)
    i = jnp.minimum(step, nq - 1)
    q_heads = []
    for g in range(H_SB // 2):
        _, qa, qb = _query_heads(q_ref.at[:, :, g * LANES:(g + 1) * LANES])
        q_heads += [qa.astype(BF16), qb.astype(BF16)]
    key_pos, q_pos = _positions(i, tq, tk, past)

    def start():
        carry_ref[...] = jnp.zeros_like(carry_ref)

    def clear():
        acc_ref[...] = jnp.zeros_like(acc_ref)

    def logits(j, masked, slot):
        start = pl.multiple_of(j * tk, tk)
        tri2 = tri_ref[...]
        scores = [_dot(k_ref[0, pl.ds(start, tk), (h // 2) * LANES:(h // 2 + 1) * LANES], q_heads[h])
                  for h in range(H_SB)]
        for h in range(H_SB):
            z = scores[h]
            nl = jnp.maximum(z, 0.0) + jnp.log(1.0 + jnp.exp2(-jnp.abs(z))) * LOG2E
            if masked:
                valid = start + key_pos < q_pos
                nl = jnp.where(valid, nl, 0.0)
            hi = nl.astype(BF16)
            lo = (nl - hi.astype(F32)).astype(BF16)
            suffix = _dot(tri2, jnp.concatenate([hi, lo], axis=0))
            t = z - nl - suffix
            t_ref[slot, h] = jnp.where(valid, t, -jnp.inf) if masked else t
            later = carry_ref[h]
            later_ref[slot, h] = later
            carry_ref[h] = later + jnp.sum(nl, axis=0, keepdims=True)

    def weigh(j, slot):
        start = pl.multiple_of(j * tk, tk)
        for h in range(H_SB):
            a = jnp.exp2(t_ref[slot, h] - later_ref[slot, h])
            vt = vt_ref[0, h * HEAD_DIM:(h + 1) * HEAD_DIM, pl.ds(start, tk)]
            acc_ref[h] = acc_ref[h] + _dot(vt, a.astype(BF16))

    def reaches(j):
        least = carry_ref[0]
        for h in range(1, H_SB):
            least = jnp.minimum(least, carry_ref[h])
        return jnp.min(least) <= SKIP_LOG2

    def finish():
        for g in range(H_SB // 2):
            o_ref[0, :, g * LANES:(g + 1) * LANES] = _merge_heads(
                acc_ref[2 * g], acc_ref[2 * g + 1]).astype(BF16)

    _pipelined_descend(step, nq, lambda qi: _diag_block(qi, tq, tk, past), start, clear,
                       logits, weigh, finish, reaches, state_ref)


def _sb(q, k, vt, past, tq, tk):
    b, t, _ = q.shape
    tkeys = k.shape[1]
    nq = t // tq
    assert (tk % tq == 0 and past % tk == 0) or tk == tkeys
    r = jnp.arange(tk)
    tri = (r[None, :] > r[:, None]).astype(BF16)
    tri2 = jnp.concatenate([tri, tri], axis=1)
    per_batch = lambda shape: pl.BlockSpec(shape, lambda bi, i: (bi, 0, 0), pipeline_mode=pl.Buffered(1))
    return pl.pallas_call(
        functools.partial(_sb_kernel, tq=tq, tk=tk, past=past, nq=nq),
        grid=(b, _grid_steps(nq)),
        in_specs=[pl.BlockSpec((1, tq, D_SB), lambda bi, i: (bi, jnp.minimum(i, nq - 1), 0)),
                  per_batch((1, tkeys, D_SB)), per_batch((1, D_SB, tkeys)), _const_spec((tk, 2 * tk))],
        out_specs=pl.BlockSpec((1, tq, D_SB), lambda bi, i: (bi, jnp.maximum(i - 1, 0), 0)),
        out_shape=jax.ShapeDtypeStruct((b, t, D_SB), BF16),
        scratch_shapes=[pltpu.VMEM((H_SB, 1, tq), F32), pltpu.VMEM((H_SB, HEAD_DIM, tq), F32),
                        pltpu.VMEM((2, H_SB, tk, tq), F32), pltpu.VMEM((2, H_SB, 1, tq), F32),
                        pltpu.SMEM((3,), jnp.int32)],
        compiler_params=pltpu.CompilerParams(dimension_semantics=("arbitrary",) * 2,
                                             vmem_limit_bytes=VMEM_LIMIT),
        name="sb",
    )(q, k, vt, tri2)


def _pick_tile(n, pref):
    t = min(n, pref)
    while n % t:
        t //= 2
    return t


def _layer(x, past, w, final_norm, layer, depth, stacked):
    b, t, _ = x.shape
    past_k, past_v, past_lf, past_sk, past_sv, conv_buf = past
    p = past_k.shape[1]
    assert t >= CONV_W - 1 and t % HALO == 0
    n = b * t
    tm = _pick_tile(n, DENSE_ROWS)

    (x1, qf, kf, vf, lfp, u, qs, ks, vs, kfb, vfb, ksb, vsb, lf) = _dense_in(
        x.reshape(n, D_MODEL), w["n1"], w["wg1"], w["wu1"], w["wd1"], w["n2"], w["wp"], w["bfp"], tm,
        layer, depth, stacked)

    r3 = lambda a: a.reshape(-1, t, a.shape[-1])
    lfp3, u3 = r3(lfp), r3(u)
    if p:
        fill = -(p + t) % LANES
        cat = lambda old, new: jnp.concatenate(
            [old.reshape(b, p, -1).astype(new.dtype), new,
             jnp.zeros((b, fill, new.shape[-1]), new.dtype)], axis=1)
        kf_all, vf_all, vs_all = cat(past_k, r3(kfb)), cat(past_v, r3(vfb)), cat(past_sv, r3(vsb))
        lf_all = cat(jnp.pad(past_lf, ((0, 0), (0, 0), (0, LANES - H_FOX))), lfp3)
        ks_all = cat(past_sk, r3(ksb))
        tq_fox = tq_sb = t
        tk_fox = p + t + fill
        tk_sb = LANES if (LANES % t == 0 and p % LANES == 0) else tk_fox
    else:
        kf_all, vf_all, vs_all, lf_all, ks_all = r3(kfb), r3(vfb), r3(vsb), lfp3, r3(ksb)
        tq_sb = tk_sb = _pick_tile(t, SB_BLOCK)
        tq_fox = tk_fox = _pick_tile(t, FOX_BLOCK)

    kt, vft, vst, bias_max, knorm_max = _prep(kf_all, vf_all, vs_all, lf_all, tk_fox)
    yf = _fox(r3(qf), kt, vft, bias_max, knorm_max, p, tq_fox, tk_fox)
    ys = _sb(r3(qs), ks_all, vst, p, tq_sb, tk_sb)
    buf_pad = jnp.pad(conv_buf, ((0, 0), (HALO - (CONV_W - 1), 0), (0, 0)))
    yc = _conv(u3, buf_pad, w["conv_w"], w["conv_b"], w["ln_g"], w["ln_b"], _pick_tile(t, CONV_ROWS))

    flat = lambda a: a.reshape(n, a.shape[-1])
    xo = _dense_out(x1, flat(yf), flat(yc), flat(ys), w["wo"], w["n3"], w["wg2"], w["wu2"], w["wd2"],
                    final_norm, tm, layer == depth - 1)
    return xo.reshape(b, t, D_MODEL), (kf, vf, ks, vs, lf), u3[:, t - (CONV_W - 1):, :]


def _layer_weights(l, ffn_norm, ffn_gate, ffn_up, ffn_down, mix_norm, w_in, b_forget,
                   conv_w, conv_b, conv_ln_g, conv_ln_b, w_out):
    wi = w_in[l]
    off_f = 3 * D_FOX
    off_glu = off_f + H_FOX
    off_qc = off_glu + 2 * C_CONV
    wp = jnp.concatenate([wi[:, :off_f], jnp.pad(wi[:, off_f:off_glu], ((0, 0), (0, LANES - H_FOX))),
                          wi[:, off_glu:off_qc], wi[:, off_qc:]], axis=1).astype(BF16)
    row = lambda a: a.reshape(1, -1).astype(F32)
    return dict(
        n1=row(ffn_norm[l, 0]), wg1=ffn_gate[l, 0].astype(BF16), wu1=ffn_up[l, 0].astype(BF16),
        wd1=ffn_down[l, 0].astype(BF16), n2=row(mix_norm[l]), wp=wp,
        bfp=jnp.pad(row(b_forget[l]), ((0, 0), (0, LANES - H_FOX))),
        conv_w=conv_w[l], conv_b=row(conv_b[l]), ln_g=row(conv_ln_g[l]), ln_b=row(conv_ln_b[l]),
        wo=w_out[l].astype(BF16), n3=row(ffn_norm[l, 1]), wg2=ffn_gate[l, 1].astype(BF16),
        wu2=ffn_up[l, 1].astype(BF16), wd2=ffn_down[l, 1].astype(BF16))


def kernel(x_prompt, x_sample, cache_fox_k, cache_fox_v, cache_fox_logf, cache_sb_k, cache_sb_v, state_conv,
           ffn_norm, ffn_gate, ffn_up, ffn_down, mix_norm, w_in, b_forget, conv_w, conv_b, conv_ln_g,
           conv_ln_b, w_out, final_norm):
    depth = w_in.shape[0]
    bp = x_prompt.shape[0]
    dt = x_prompt.dtype
    empty = (jnp.zeros((bp, 0, H_FOX, HEAD_DIM), dt), jnp.zeros((bp, 0, H_FOX, HEAD_DIM), dt),
             jnp.zeros((bp, 0, H_FOX), dt), jnp.zeros((bp, 0, H_SB, HEAD_DIM), dt),
             jnp.zeros((bp, 0, H_SB, HEAD_DIM), dt), jnp.zeros((bp, CONV_W - 1, C_CONV), dt))
    fn = final_norm.reshape(1, -1).astype(F32)
    xp, xs = x_prompt, x_sample
    stacked_p = stacked_s = None
    conv_p, conv_s = [], []
    for l in range(depth):
        w = _layer_weights(l, ffn_norm, ffn_gate, ffn_up, ffn_down, mix_norm, w_in, b_forget,
                           conv_w, conv_b, conv_ln_g, conv_ln_b, w_out)
        xp, stacked_p, cp = _layer(xp, empty, w, fn, l, depth, stacked_p)
        cache = (cache_fox_k[l], cache_fox_v[l], cache_fox_logf[l], cache_sb_k[l], cache_sb_v[l],
                 state_conv[l])
        xs, stacked_s, cs = _layer(xs, cache, w, fn, l, depth, stacked_s)
        conv_p.append(cp)
        conv_s.append(cs)

    def outputs(x, stacked, conv):
        b, t, _ = x.shape
        heads = lambda a: a.reshape(depth, b, t, -1, HEAD_DIM)
        kf, vf, ks, vs, lf = stacked
        return heads(kf), heads(vf), lf.reshape(depth, b, t, H_FOX), heads(ks), heads(vs), jnp.stack(conv, 0)

    return (xp, xs) + outputs(xp, stacked_p, conv_p) + outputs(xs, stacked_s, conv_s)
```

```python
import functools

import jax
import jax.numpy as jnp
from jax import lax
from jax.experimental import pallas as pl
from jax.experimental.pallas import tpu as pltpu

D_MODEL = 1024
HEAD_DIM = 64
D_FOX = 512
C_CONV = 256
D_SB = 256
H_FOX = 8
H_SB = 4
CONV_W = 31
D_FF = 2816
EPS = 1e-6

LANES = 128
SUBLANES = 8
HALO = 32
ONES_ROWS = 16
FOX_HEADS = 4
DENSE_ROWS = 512
FOX_BLOCK = 512
SB_BLOCK = 256
CONV_ROWS = 512
CONV_CHUNK = 64
N_SPLIT = 3
VMEM_LIMIT = 56 * 1024 * 1024
SKIP_LOG = 105.0
LOG2E = 1.4426950408889634
SKIP_LOG2 = SKIP_LOG * LOG2E
NORM_SLACK = 1.01

ZQ, ZK, ZV, ZF, ZA, ZG, ZQS, ZKS, ZVS, ZEND = 0, 512, 1024, 1536, 1664, 1920, 2176, 2432, 2688, 2944

F32 = jnp.float32
BF16 = jnp.bfloat16


def _dot(a, b):
    return jnp.dot(a, b, preferred_element_type=F32)


def _sigmoid(x):
    return 1.0 / (1.0 + jnp.exp(-x))


def _softplus(x):
    return jnp.maximum(x, 0.0) + jnp.log1p(jnp.exp(-jnp.abs(x)))


def _rms(x, g):
    return x * lax.rsqrt(jnp.mean(x * x, axis=-1, keepdims=True) + EPS) * g


def _ffn_half(x, g_ref, wg_ref, wu_ref, wd_ref):
    hn = _rms(x, g_ref[...]).astype(BF16)
    g = _dot(hn, wg_ref[...])
    u = _dot(hn, wu_ref[...])
    act = (g * _sigmoid(g) * u).astype(BF16)
    return x + 0.5 * _dot(act, wd_ref[...])


def _split3(x):
    a = x.astype(BF16)
    r = x - a.astype(F32)
    b = r.astype(BF16)
    c = (r - b.astype(F32)).astype(BF16)
    return a, b, c


def _store_heads(o_ref, x):
    heads = x.shape[1] // HEAD_DIM
    for h in range(heads):
        o_ref[pl.ds(h, x.shape[0], stride=heads), :] = x[:, h * HEAD_DIM:(h + 1) * HEAD_DIM]


def _dense_in_kernel(x_ref, n1_ref, wg_ref, wu_ref, wd_ref, n2_ref, wp_ref, bf_ref, *refs):
    (x1_ref, qf_ref, kf_ref, vf_ref, lfp_ref, u_ref, qs_ref, ks_ref, vs_ref,
     kfb_ref, vfb_ref, ksb_ref, vsb_ref, lf_ref) = refs[-14:]
    x1 = _ffn_half(x_ref[...], n1_ref, wg_ref, wu_ref, wd_ref)
    x1_ref[...] = x1
    h = _rms(x1, n2_ref[...]).astype(BF16)
    z = _dot(h, wp_ref[...])
    scale = HEAD_DIM ** -0.5 * LOG2E
    qf_ref[...] = (z[:, ZQ:ZK] * scale).astype(BF16)
    kf, vf = z[:, ZK:ZV], z[:, ZV:ZF]
    _store_heads(kf_ref, kf)
    _store_heads(vf_ref, vf)
    kfb_ref[...] = kf.astype(BF16)
    vfb_ref[...] = vf.astype(BF16)
    zf = z[:, ZF:ZA] + bf_ref[...]
    lane = lax.broadcasted_iota(jnp.int32, zf.shape, 1)
    lf = jnp.where(lane < H_FOX, -_softplus(-zf), 0.0)
    lfp_ref[...] = lf
    lf_ref[...] = lf[:, :H_FOX]
    u_ref[...] = z[:, ZA:ZG] * _sigmoid(z[:, ZG:ZQS])
    qs_ref[...] = (z[:, ZQS:ZKS] * scale).astype(BF16)
    ks, vs = z[:, ZKS:ZVS], z[:, ZVS:ZEND]
    _store_heads(ks_ref, ks)
    _store_heads(vs_ref, vs)
    ksb_ref[...] = ks.astype(BF16)
    vsb_ref[...] = vs.astype(BF16)


def _const_spec(shape):
    return pl.BlockSpec(shape, lambda *_: (0,) * len(shape), pipeline_mode=pl.Buffered(1))


STACKED = {2: (H_FOX, HEAD_DIM), 3: (H_FOX, HEAD_DIM), 7: (H_SB, HEAD_DIM), 8: (H_SB, HEAD_DIM), 13: (1, H_FOX)}


def _dense_in(x, n1, wg, wu, wd, n2, wp, bfp, tm, layer, depth, stacked):
    n = x.shape[0]
    nt = n // tm
    widths = (D_MODEL, D_FOX, D_FOX, D_FOX, LANES, C_CONV, D_SB, D_SB, D_SB, D_FOX, D_FOX, D_SB, D_SB, H_FOX)
    dtypes = (F32, BF16, F32, F32, F32, F32, BF16, F32, F32, BF16, BF16, BF16, BF16, F32)
    specs, shapes = [], []
    for k, (w, d) in enumerate(zip(widths, dtypes)):
        if k in STACKED:
            per_token, width = STACKED[k]
            specs.append(pl.BlockSpec((tm * per_token, width), lambda i: (i + layer * nt, 0)))
            shapes.append(jax.ShapeDtypeStruct((depth * n * per_token, width), d))
        else:
            specs.append(pl.BlockSpec((tm, w), lambda i: (i, 0)))
            shapes.append(jax.ShapeDtypeStruct((n, w), d))
    prev = () if stacked is None else tuple(stacked)
    n_fixed = 8
    return pl.pallas_call(
        _dense_in_kernel,
        grid=(nt,),
        in_specs=[pl.BlockSpec((tm, D_MODEL), lambda i: (i, 0)), _const_spec((1, D_MODEL)),
                  _const_spec((D_MODEL, D_FF)), _const_spec((D_MODEL, D_FF)), _const_spec((D_FF, D_MODEL)),
                  _const_spec((1, D_MODEL)), _const_spec((D_MODEL, ZEND)), _const_spec((1, LANES))]
                 + [pl.BlockSpec(memory_space=pl.ANY)] * len(prev),
        out_specs=specs,
        out_shape=shapes,
        input_output_aliases={n_fixed + a: k for a, k in enumerate(STACKED)} if prev else {},
        compiler_params=pltpu.CompilerParams(dimension_semantics=("arbitrary",),
                                             vmem_limit_bytes=VMEM_LIMIT),
        name="dense_in",
    )(x, n1, wg, wu, wd, n2, wp, bfp, *prev)


def _dense_out_kernel(x1_ref, yf_ref, yc_ref, ys_ref, wo_ref, n_ref, wg_ref, wu_ref, wd_ref, fn_ref,
                      o_ref, *, final):
    x2 = (x1_ref[...] + _dot(yf_ref[...], wo_ref[0:D_FOX, :])
          + _dot(yc_ref[...], wo_ref[D_FOX:D_FOX + C_CONV, :])
          + _dot(ys_ref[...], wo_ref[D_FOX + C_CONV:, :]))
    x3 = _ffn_half(x2, n_ref, wg_ref, wu_ref, wd_ref)
    o_ref[...] = _rms(x3, fn_ref[...]) if final else x3


def _dense_out(x1, yf, yc, ys, wo, n, wg, wu, wd, fn, tm, final):
    nrow = x1.shape[0]
    row = lambda w: pl.BlockSpec((tm, w), lambda i: (i, 0))
    return pl.pallas_call(
        functools.partial(_dense_out_kernel, final=final),
        grid=(nrow // tm,),
        in_specs=[row(D_MODEL), row(D_FOX), row(C_CONV), row(D_SB), _const_spec((D_MODEL, D_MODEL)),
                  _const_spec((1, D_MODEL)), _const_spec((D_MODEL, D_FF)), _const_spec((D_MODEL, D_FF)),
                  _const_spec((D_FF, D_MODEL)), _const_spec((1, D_MODEL))],
        out_specs=row(D_MODEL),
        out_shape=jax.ShapeDtypeStruct((nrow, D_MODEL), F32),
        compiler_params=pltpu.CompilerParams(dimension_semantics=("arbitrary",),
                                             vmem_limit_bytes=VMEM_LIMIT),
        name="dense_out",
    )(x1, yf, yc, ys, wo, n, wg, wu, wd, fn)


def _prep_kernel(kf_ref, vf_ref, vs_ref, lfp_ref, tri_ref, sel_ref, hsel_ref,
                 kt_ref, vft_ref, vst_ref, bias_max_ref, knorm_max_ref, carry_ref, bmax_ref, kmax_ref):
    i = pl.program_id(1)

    @pl.when(i == 0)
    def _():
        carry_ref[...] = jnp.zeros_like(carry_ref)
        bmax_ref[...] = jnp.full_like(bmax_ref, -jnp.inf)
        kmax_ref[...] = jnp.zeros_like(kmax_ref)

    tri = tri_ref[...]
    cum = carry_ref[...]
    for part in _split3(lfp_ref[0]):
        cum = cum + _dot(tri, part)
    carry_ref[...] = cum[cum.shape[0] - 1:, :]
    bias = -cum * LOG2E
    aug = None
    for s, part in enumerate(_split3(bias)):
        term = _dot(part, sel_ref[s])
        aug = term if aug is None else aug + term
    kb = kf_ref[0]
    for hp in range(H_FOX // 2):
        base = 2 * LANES * hp
        kt_ref[0, :, base:base + LANES] = kb[:, hp * LANES:(hp + 1) * LANES]
        kt_ref[0, :, base + LANES:base + 2 * LANES] = aug[:, hp * LANES:(hp + 1) * LANES].astype(BF16)
    vft_ref[0] = jnp.transpose(vf_ref[0].astype(F32)).astype(BF16)
    vst_ref[0] = jnp.transpose(vs_ref[0].astype(F32)).astype(BF16)

    ksq = kb.astype(F32)
    ksq = ksq * ksq
    hi = ksq.astype(BF16)
    lo = (ksq - hi.astype(F32)).astype(BF16)
    norm2 = _dot(hi, hsel_ref[...]) + _dot(lo, hsel_ref[...])
    kmax = jnp.maximum(kmax_ref[...], jnp.max(norm2, axis=0, keepdims=True))
    bmax = jnp.maximum(bmax_ref[...], jnp.max(bias, axis=0, keepdims=True))
    kmax_ref[...] = kmax
    bmax_ref[...] = bmax
    knorm_max_ref[0, pl.ds(i, 1), :] = jnp.sqrt(kmax) * NORM_SLACK
    bias_max_ref[0, pl.ds(i, 1), :] = bmax


def _prep(kf, vf, vs, lfp, tp):
    b, tk, _ = lfp.shape
    nblk = tk // tp
    r = jnp.arange(tp)
    tri = (r[None, :] <= r[:, None]).astype(BF16)
    h = jnp.arange(LANES)
    col = jnp.arange(D_FOX)
    sel = jnp.stack([(col[None, :] == ((h // 2) * LANES + (h % 2) * N_SPLIT + s)[:, None])
                     & (h[:, None] < H_FOX) for s in range(N_SPLIT)]).astype(BF16)
    hsel = (col[:, None] // HEAD_DIM == h[None, :]).astype(BF16)
    blk = lambda w: pl.BlockSpec((1, tp, w), lambda bi, i: (bi, i, 0))
    blk_t = lambda w: pl.BlockSpec((1, w, tp), lambda bi, i: (bi, 0, i))
    per_batch = pl.BlockSpec((1, nblk, LANES), lambda bi, i: (bi, 0, 0))
    return pl.pallas_call(
        _prep_kernel,
        grid=(b, nblk),
        in_specs=[blk(D_FOX), blk(D_FOX), blk(D_SB), blk(LANES), _const_spec((tp, tp)),
                  _const_spec((N_SPLIT, LANES, D_FOX)), _const_spec((D_FOX, LANES))],
        out_specs=[blk(2 * D_FOX), blk_t(D_FOX), blk_t(D_SB), per_batch, per_batch],
        out_shape=[jax.ShapeDtypeStruct((b, tk, 2 * D_FOX), BF16),
                   jax.ShapeDtypeStruct((b, D_FOX, tk), BF16),
                   jax.ShapeDtypeStruct((b, D_SB, tk), BF16),
                   jax.ShapeDtypeStruct((b, nblk, LANES), F32),
                   jax.ShapeDtypeStruct((b, nblk, LANES), F32)],
        scratch_shapes=[pltpu.VMEM((1, LANES), F32)] * 3,
        compiler_params=pltpu.CompilerParams(dimension_semantics=("arbitrary", "arbitrary"),
                                             vmem_limit_bytes=VMEM_LIMIT),
        name="prep",
    )(kf, vf, vs, lfp, tri, sel, hsel)


def _conv_kernel(u_ref, prev_ref, buf_ref, w_ref, b_ref, g_ref, beta_ref, y_ref, xw_ref, sh_ref, *, tt, rows):
    first = pl.program_id(1) == 0
    xw_ref[0:HALO, :] = jnp.where(first, buf_ref[0], prev_ref[0])
    xw_ref[HALO:HALO + tt, :] = u_ref[0]
    off = HALO - (CONV_W - 1)
    for s in range(1, SUBLANES):
        sh_ref[s - 1] = xw_ref[s:s + tt + HALO - SUBLANES, :]
    for r0 in range(0, tt, rows):
        acc = jnp.zeros((rows, C_CONV), F32)
        for j in range(CONV_W):
            s, base = (j + off) % SUBLANES, (j + off) // SUBLANES * SUBLANES
            src = xw_ref if s == 0 else sh_ref.at[s - 1]
            acc = acc + w_ref[j:j + 1, :] * src[base + r0:base + r0 + rows, :]
        y = acc + b_ref[...]
        mu = jnp.mean(y, axis=-1, keepdims=True)
        yc = y - mu
        var = jnp.mean(yc * yc, axis=-1, keepdims=True)
        y = yc * lax.rsqrt(var + EPS) * g_ref[...] + beta_ref[...]
        y_ref[0, r0:r0 + rows, :] = (y * _sigmoid(y)).astype(BF16)


def _conv(u, buf_pad, w, b, g, beta, tt):
    bsz, t, _ = u.shape
    rows = min(tt, CONV_CHUNK)
    per = tt // HALO
    return pl.pallas_call(
        functools.partial(_conv_kernel, tt=tt, rows=rows),
        grid=(bsz, t // tt),
        in_specs=[pl.BlockSpec((1, tt, C_CONV), lambda bi, i: (bi, i, 0)),
                  pl.BlockSpec((1, HALO, C_CONV), lambda bi, i: (bi, jnp.maximum(i * per - 1, 0), 0)),
                  pl.BlockSpec((1, HALO, C_CONV), lambda bi, i: (bi, 0, 0)),
                  _const_spec((CONV_W, C_CONV)), _const_spec((1, C_CONV)), _const_spec((1, C_CONV)),
                  _const_spec((1, C_CONV))],
        out_specs=pl.BlockSpec((1, tt, C_CONV), lambda bi, i: (bi, i, 0)),
        out_shape=jax.ShapeDtypeStruct((bsz, t, C_CONV), BF16),
        scratch_shapes=[pltpu.VMEM((HALO + tt, C_CONV), F32),
                        pltpu.VMEM((SUBLANES - 1, HALO + tt - SUBLANES, C_CONV), F32)],
        compiler_params=pltpu.CompilerParams(dimension_semantics=("arbitrary", "arbitrary")),
        name="conv",
    )(u, u, buf_pad, w, b, g, beta)


def _query_heads(q_ref):
    qt = jnp.transpose(q_ref[0].astype(F32))
    row = lax.broadcasted_iota(jnp.int32, qt.shape, 0)
    return row, jnp.where(row < HEAD_DIM, qt, 0.0), jnp.where(row >= HEAD_DIM, qt, 0.0)


def _diag_block(i, tq, tk, past):
    return (past + i * tq + tq - 1) // tk


def _positions(i, tq, tk, past):
    key_pos = lax.broadcasted_iota(jnp.int32, (tk, tq), 0)
    q_pos = past + i * tq + lax.broadcasted_iota(jnp.int32, (tk, tq), 1)
    return key_pos, q_pos


def _merge_heads(out_a, out_b):
    return jnp.transpose(jnp.concatenate([out_a, out_b], axis=0))


def _grid_steps(nq):
    return nq + 1 if nq > 1 else 1


def _pipelined_descend(i, nq, diag, start, clear, first, second, finish, reaches, state_ref):
    has_new = i < nq
    jd = diag(jnp.minimum(i, nq - 1))

    @pl.when(i == 0)
    def _():
        start()
        clear()
        first(jd, True, 0)
        state_ref[0] = reaches(jd - 1).astype(jnp.int32)
        state_ref[2] = 1

    for slot in range(2 if nq > 1 else 0):
        waited = jnp.logical_and(i > 0, state_ref[2] == slot)

        @pl.when(jnp.logical_and(waited, has_new))
        def _():
            start()
            first(jd, True, 1 - slot)
            state_ref[0] = reaches(jd - 1).astype(jnp.int32)
            second(state_ref[1], slot)
            finish()
            clear()

        @pl.when(jnp.logical_and(waited, jnp.logical_not(has_new)))
        def _():
            second(state_ref[1], slot)
            finish()

    base = 1 - state_ref[2]
    waiting = lambda j: (base + jd - j - 1) & 1

    def cond(c):
        j, go = c
        return jnp.logical_and(j >= 0, go)

    def body(c):
        j, _ = c
        for slot in range(2):
            @pl.when(waiting(j) == slot)
            def _():
                first(j, False, 1 - slot)
                state_ref[0] = reaches(j - 1).astype(jnp.int32)
                second(j + 1, slot)
        return j - 1, state_ref[0] != 0

    go0 = jnp.logical_and(has_new, state_ref[0] != 0)
    j_stop, _ = lax.while_loop(cond, body, (jd - 1, go0))
    if nq > 1:
        state_ref[1] = j_stop + 1
        state_ref[2] = waiting(j_stop)
    else:
        for slot in range(2):
            @pl.when(waiting(j_stop) == slot)
            def _():
                second(j_stop + 1, slot)
                finish()


def _fox_kernel(bias_max_ref, knorm_max_ref, q_ref, k_ref, vt_ref, o_ref,
                m_ref, acc_ref, s_ref, mblk_ref, alpha_ref, state_ref, *, tq, tk, past, nblk, nq):
    bi = pl.program_id(0)
    hg = pl.program_id(1)
    step = pl.program_id(2)
    i = jnp.minimum(step, nq - 1)
    q_heads, q_norms = [], []
    for g in range(FOX_HEADS // 2):
        row, qa, qb = _query_heads(q_ref.at[:, :, g * LANES:(g + 1) * LANES])
        ones_a = jnp.where(row < N_SPLIT, 1.0, 0.0)
        ones_b = jnp.where(row < 2 * N_SPLIT, 1.0, 0.0) - ones_a
        q_heads += [jnp.concatenate([qa, ones_a], axis=0).astype(BF16),
                    jnp.concatenate([qb, ones_b], axis=0).astype(BF16)]
        q_norms += [jnp.sqrt(jnp.sum(qh * qh, axis=0, keepdims=True)) * NORM_SLACK for qh in (qa, qb)]
    key_pos, q_pos = _positions(i, tq, tk, past)

    def start():
        m_ref[...] = jnp.full_like(m_ref, -jnp.inf)

    def clear():
        acc_ref[...] = jnp.zeros_like(acc_ref)

    def score(j, masked, slot):
        start = pl.multiple_of(j * tk, tk)
        scores = [_dot(k_ref[0, pl.ds(start, tk), (h // 2) * 2 * LANES:(h // 2 + 1) * 2 * LANES], q_heads[h])
                  for h in range(FOX_HEADS)]
        for h in range(FOX_HEADS):
            s = scores[h]
            if masked:
                s = jnp.where(start + key_pos <= q_pos, s, -jnp.inf)
            m_prev = m_ref[h]
            m_new = jnp.maximum(m_prev, jnp.max(s, axis=0, keepdims=True))
            alpha_ref[slot, h] = jnp.exp2(m_prev - m_new)
            mblk_ref[slot, h] = m_new
            m_ref[h] = m_new
            s_ref[slot, h] = s

    def accumulate(j, slot):
        start = pl.multiple_of(j * tk, tk)
        ones = jnp.ones((ONES_ROWS, tk), BF16)
        for h in range(FOX_HEADS):
            p = jnp.exp2(s_ref[slot, h] - mblk_ref[slot, h])
            vt = vt_ref[0, h * HEAD_DIM:(h + 1) * HEAD_DIM, pl.ds(start, tk)]
            pv = _dot(jnp.concatenate([vt, ones], axis=0), p.astype(BF16))
            acc_ref[h] = alpha_ref[slot, h] * acc_ref[h] + pv

    def reaches(j):
        base = (bi * nblk + jnp.maximum(j, 0)) * H_FOX + FOX_HEADS * hg
        worst = None
        for h in range(FOX_HEADS):
            top = q_norms[h] * knorm_max_ref[base + h] + bias_max_ref[base + h] - m_ref[h]
            worst = top if worst is None else jnp.maximum(worst, top)
        return jnp.max(worst) >= -SKIP_LOG2

    def finish():
        for g in range(FOX_HEADS // 2):
            o_ref[0, :, g * LANES:(g + 1) * LANES] = _merge_heads(
                *[acc_ref[h, :HEAD_DIM] / acc_ref[h, HEAD_DIM:HEAD_DIM + 1]
                  for h in (2 * g, 2 * g + 1)]).astype(BF16)

    _pipelined_descend(step, nq, lambda qi: _diag_block(qi, tq, tk, past), start, clear,
                       score, accumulate, finish, reaches, state_ref)


def _fox(q, kt, vt, bias_max, knorm_max, past, tq, tk):
    b, t, _ = q.shape
    tkeys = kt.shape[1]
    nblk = tkeys // tk
    nq = t // tq
    assert (tk % tq == 0 and past % tk == 0) or tk == tkeys
    smem = pl.BlockSpec(memory_space=pltpu.SMEM)
    per_head = lambda a: a[:, :, :H_FOX].reshape(-1)
    width = FOX_HEADS * HEAD_DIM
    resident = lambda shape, imap: pl.BlockSpec(
        shape, imap, pipeline_mode=pl.Buffered(1 if nblk > 1 else 2))
    return pl.pallas_call(
        functools.partial(_fox_kernel, tq=tq, tk=tk, past=past, nblk=nblk, nq=nq),
        grid=(b, H_FOX // FOX_HEADS, _grid_steps(nq)),
        in_specs=[smem, smem,
                  pl.BlockSpec((1, tq, width), lambda bi, hg, i: (bi, jnp.minimum(i, nq - 1), hg)),
                  resident((1, tkeys, 2 * width), lambda bi, hg, i: (bi, 0, hg)),
                  resident((1, width, tkeys), lambda bi, hg, i: (bi, hg, 0))],
        out_specs=pl.BlockSpec((1, tq, width), lambda bi, hg, i: (bi, jnp.maximum(i - 1, 0), hg)),
        out_shape=jax.ShapeDtypeStruct((b, t, D_FOX), BF16),
        scratch_shapes=[pltpu.VMEM((FOX_HEADS, 1, tq), F32),
                        pltpu.VMEM((FOX_HEADS, HEAD_DIM + ONES_ROWS, tq), F32),
                        pltpu.VMEM((2, FOX_HEADS, tk, tq), F32),
                        pltpu.VMEM((2, FOX_HEADS, 1, tq), F32), pltpu.VMEM((2, FOX_HEADS, 1, tq), F32),
                        pltpu.SMEM((3,), jnp.int32)],
        compiler_params=pltpu.CompilerParams(dimension_semantics=("arbitrary",) * 3,
                                             vmem_limit_bytes=VMEM_LIMIT),
        name="fox",
    )(per_head(bias_max), per_head(knorm_max), q, kt, vt)


def _sb_kernel(q_ref, k_ref, vt_ref, tri_ref, o_ref, carry_ref, acc_ref, t_ref, later_ref, state_ref,
               *, tq, tk, past, nq):
    step = pl.program_id(1)
    i = jnp.minimum(step, nq - 1)
    q_heads = []
    for g in range(H_SB // 2):
        _, qa, qb = _query_heads(q_ref.at[:, :, g * LANES:(g + 1) * LANES])
        q_heads += [qa.astype(BF16), qb.astype(BF16)]
    key_pos, q_pos = _positions(i, tq, tk, past)

    def start():
        carry_ref[...] = jnp.zeros_like(carry_ref)

    def clear():
        acc_ref[...] = jnp.zeros_like(acc_ref)

    def logits(j, masked, slot):
        start = pl.multiple_of(j * tk, tk)
        tri2 = tri_ref[...]
        scores = [_dot(k_ref[0, pl.ds(start, tk), (h // 2) * LANES:(h // 2 + 1) * LANES], q_heads[h])
                  for h in range(H_SB)]
        for h in range(H_SB):
            z = scores[h]
            nl = jnp.maximum(z, 0.0) + jnp.log(1.0 + jnp.exp2(-jnp.abs(z))) * LOG2E
            if masked:
                valid = start + key_pos < q_pos
                nl = jnp.where(valid, nl, 0.0)
            hi = nl.astype(BF16)
            lo = (nl - hi.astype(F32)).astype(BF16)
            suffix = _dot(tri2, jnp.concatenate([hi, lo], axis=0))
            t = z - nl - suffix
            t_ref[slot, h] = jnp.where(valid, t, -jnp.inf) if masked else t
            later = carry_ref[h]
            later_ref[slot, h] = later
            carry_ref[h] = later + jnp.sum(nl, axis=0, keepdims=True)

    def weigh(j, slot):
        start = pl.multiple_of(j * tk, tk)
        for h in range(H_SB):
            a = jnp.exp2(t_ref[slot, h] - later_ref[slot, h])
            vt = vt_ref[0, h * HEAD_DIM:(h + 1) * HEAD_DIM, pl.ds(start, tk)]
            acc_ref[h] = acc_ref[h] + _dot(vt, a.astype(BF16))

    def reaches(j):
        least = carry_ref[0]
        for h in range(1, H_SB):
            least = jnp.minimum(least, carry_ref[h])
        return jnp.min(least) <= SKIP_LOG2

    def finish():
        for g in range(H_SB // 2):
            o_ref[0, :, g * LANES:(g + 1) * LANES] = _merge_heads(
                acc_ref[2 * g], acc_ref[2 * g + 1]).astype(BF16)

    _pipelined_descend(step, nq, lambda qi: _diag_block(qi, tq, tk, past), start, clear,
                       logits, weigh, finish, reaches, state_ref)


def _sb(q, k, vt, past, tq, tk):
    b, t, _ = q.shape
    tkeys = k.shape[1]
    nq = t // tq
    assert (tk % tq == 0 and past % tk == 0) or tk == tkeys
    r = jnp.arange(tk)
    tri = (r[None, :] > r[:, None]).astype(BF16)
    tri2 = jnp.concatenate([tri, tri], axis=1)
    per_batch = lambda shape: pl.BlockSpec(shape, lambda bi, i: (bi, 0, 0), pipeline_mode=pl.Buffered(1))
    return pl.pallas_call(
        functools.partial(_sb_kernel, tq=tq, tk=tk, past=past, nq=nq),
        grid=(b, _grid_steps(nq)),
        in_specs=[pl.BlockSpec((1, tq, D_SB), lambda bi, i: (bi, jnp.minimum(i, nq - 1), 0)),
                  per_batch((1, tkeys, D_SB)), per_batch((1, D_SB, tkeys)), _const_spec((tk, 2 * tk))],
        out_specs=pl.BlockSpec((1, tq, D_SB), lambda bi, i: (bi, jnp.maximum(i - 1, 0), 0)),
        out_shape=jax.ShapeDtypeStruct((b, t, D_SB), BF16),
        scratch_shapes=[pltpu.VMEM((H_SB, 1, tq), F32), pltpu.VMEM((H_SB, HEAD_DIM, tq), F32),
                        pltpu.VMEM((2, H_SB, tk, tq), F32), pltpu.VMEM((2, H_SB, 1, tq), F32),
                        pltpu.SMEM((3,), jnp.int32)],
        compiler_params=pltpu.CompilerParams(dimension_semantics=("arbitrary",) * 2,
                                             vmem_limit_bytes=VMEM_LIMIT),
        name="sb",
    )(q, k, vt, tri2)


def _pick_tile(n, pref):
    t = min(n, pref)
    while n % t:
        t //= 2
    return t


def _layer(x, past, w, final_norm, layer, depth, stacked):
    b, t, _ = x.shape
    past_k, past_v, past_lf, past_sk, past_sv, conv_buf = past
    p = past_k.shape[1]
    assert t >= CONV_W - 1 and t % HALO == 0
    n = b * t
    tm = _pick_tile(n, DENSE_ROWS)

    (x1, qf, kf, vf, lfp, u, qs, ks, vs, kfb, vfb, ksb, vsb, lf) = _dense_in(
        x.reshape(n, D_MODEL), w["n1"], w["wg1"], w["wu1"], w["wd1"], w["n2"], w["wp"], w["bfp"], tm,
        layer, depth, stacked)

    r3 = lambda a: a.reshape(-1, t, a.shape[-1])
    lfp3, u3 = r3(lfp), r3(u)
    if p:
        fill = -(p + t) % LANES
        cat = lambda old, new: jnp.concatenate(
            [old.reshape(b, p, -1).astype(new.dtype), new,
             jnp.zeros((b, fill, new.shape[-1]), new.dtype)], axis=1)
        kf_all, vf_all, vs_all = cat(past_k, r3(kfb)), cat(past_v, r3(vfb)), cat(past_sv, r3(vsb))
        lf_all = cat(jnp.pad(past_lf, ((0, 0), (0, 0), (0, LANES - H_FOX))), lfp3)
        ks_all = cat(past_sk, r3(ksb))
        tq_fox = tq_sb = t
        tk_fox = p + t + fill
        tk_sb = LANES if (LANES % t == 0 and p % LANES == 0) else tk_fox
    else:
        kf_all, vf_all, vs_all, lf_all, ks_all = r3(kfb), r3(vfb), r3(vsb), lfp3, r3(ksb)
        tq_sb = tk_sb = _pick_tile(t, SB_BLOCK)
        tq_fox = tk_fox = _pick_tile(t, FOX_BLOCK)

    kt, vft, vst, bias_max, knorm_max = _prep(kf_all, vf_all, vs_all, lf_all, tk_fox)
    yf = _fox(r3(qf), kt, vft, bias_max, knorm_max, p, tq_fox, tk_fox)
    ys = _sb(r3(qs), ks_all, vst, p, tq_sb, tk_sb)
    buf_pad = jnp.pad(conv_buf, ((0, 0), (HALO - (CONV_W - 1), 0), (0, 0)))
    yc = _conv(u3, buf_pad, w["conv_w"], w["conv_b"], w["ln_g"], w["ln_b"], _pick_tile(t, CONV_ROWS))

    flat = lambda a: a.reshape(n, a.shape[-1])
    xo = _dense_out(x1, flat(yf), flat(yc), flat(ys), w["wo"], w["n3"], w["wg2"], w["wu2"], w["wd2"],
                    final_norm, tm, layer == depth - 1)
    return xo.reshape(b, t, D_MODEL), (kf, vf, ks, vs, lf), u3[:, t - (CONV_W - 1):, :]


def _layer_weights(l, ffn_norm, ffn_gate, ffn_up, ffn_down, mix_norm, w_in, b_forget,
                   conv_w, conv_b, conv_ln_g, conv_ln_b, w_out):
    wi = w_in[l]
    off_f = 3 * D_FOX
    off_glu = off_f + H_FOX
    off_qc = off_glu + 2 * C_CONV
    wp = jnp.concatenate([wi[:, :off_f], jnp.pad(wi[:, off_f:off_glu], ((0, 0), (0, LANES - H_FOX))),
                          wi[:, off_glu:off_qc], wi[:, off_qc:]], axis=1).astype(BF16)
    row = lambda a: a.reshape(1, -1).astype(F32)
    return dict(
        n1=row(ffn_norm[l, 0]), wg1=ffn_gate[l, 0].astype(BF16), wu1=ffn_up[l, 0].astype(BF16),
        wd1=ffn_down[l, 0].astype(BF16), n2=row(mix_norm[l]), wp=wp,
        bfp=jnp.pad(row(b_forget[l]), ((0, 0), (0, LANES - H_FOX))),
        conv_w=conv_w[l], conv_b=row(conv_b[l]), ln_g=row(conv_ln_g[l]), ln_b=row(conv_ln_b[l]),
        wo=w_out[l].astype(BF16), n3=row(ffn_norm[l, 1]), wg2=ffn_gate[l, 1].astype(BF16),
        wu2=ffn_up[l, 1].astype(BF16), wd2=ffn_down[l, 1].astype(BF16))


def kernel(x_prompt, x_sample, cache_fox_k, cache_fox_v, cache_fox_logf, cache_sb_k, cache_sb_v, state_conv,
           ffn_norm, ffn_gate, ffn_up, ffn_down, mix_norm, w_in, b_forget, conv_w, conv_b, conv_ln_g,
           conv_ln_b, w_out, final_norm):
    depth = w_in.shape[0]
    bp = x_prompt.shape[0]
    dt = x_prompt.dtype
    empty = (jnp.zeros((bp, 0, H_FOX, HEAD_DIM), dt), jnp.zeros((bp, 0, H_FOX, HEAD_DIM), dt),
             jnp.zeros((bp, 0, H_FOX), dt), jnp.zeros((bp, 0, H_SB, HEAD_DIM), dt),
             jnp.zeros((bp, 0, H_SB, HEAD_DIM), dt), jnp.zeros((bp, CONV_W - 1, C_CONV), dt))
    fn = final_norm.reshape(1, -1).astype(F32)
    xp, xs = x_prompt, x_sample
    stacked_p = stacked_s = None
    conv_p, conv_s = [], []
    for l in range(depth):
        w = _layer_weights(l, ffn_norm, ffn_gate, ffn_up, ffn_down, mix_norm, w_in, b_forget,
                           conv_w, conv_b, conv_ln_g, conv_ln_b, w_out)
        xp, stacked_p, cp = _layer(xp, empty, w, fn, l, depth, stacked_p)
        cache = (cache_fox_k[l], cache_fox_v[l], cache_fox_logf[l], cache_sb_k[l], cache_sb_v[l],
                 state_conv[l])
        xs, stacked_s, cs = _layer(xs, cache, w, fn, l, depth, stacked_s)
        conv_p.append(cp)
        conv_s.append(cs)

    def outputs(x, stacked, conv):
        b, t, _ = x.shape
        heads = lambda a: a.reshape(depth, b, t, -1, HEAD_DIM)
        kf, vf, ks, vs, lf = stacked
        return heads(kf), heads(vf), lf.reshape(depth, b, t, H_FOX), heads(ks), heads(vs), jnp.stack(conv, 0)

    return (xp, xs) + outputs(xp, stacked_p, conv_p) + outputs(xs, stacked_s, conv_s)
```

```python
import functools

import jax
import jax.numpy as jnp
from jax import lax
from jax.experimental import pallas as pl
from jax.experimental.pallas import tpu as pltpu

D_MODEL = 1024
HEAD_DIM = 64
D_FOX = 512
C_CONV = 256
D_SB = 256
H_FOX = 8
H_SB = 4
CONV_W = 31
D_FF = 2816
EPS = 1e-6

LANES = 128
SUBLANES = 8
HALO = 32
ONES_ROWS = 16
FOX_HEADS = 4
DENSE_ROWS = 512
FOX_BLOCK = 512
SB_BLOCK = 256
CONV_ROWS = 1024
CONV_CHUNK = 64
N_SPLIT = 3
VMEM_LIMIT = 56 * 1024 * 1024
SKIP_LOG = 105.0
LOG2E = 1.4426950408889634
SKIP_LOG2 = SKIP_LOG * LOG2E
NORM_SLACK = 1.01

ZQ, ZK, ZV, ZF, ZA, ZG, ZQS, ZKS, ZVS, ZEND = 0, 512, 1024, 1536, 1664, 1920, 2176, 2432, 2688, 2944

F32 = jnp.float32
BF16 = jnp.bfloat16


def _dot(a, b):
    return jnp.dot(a, b, preferred_element_type=F32)


def _sigmoid(x):
    return 1.0 / (1.0 + jnp.exp(-x))


def _softplus(x):
    return jnp.maximum(x, 0.0) + jnp.log1p(jnp.exp(-jnp.abs(x)))


def _rms(x, g):
    return x * lax.rsqrt(jnp.mean(x * x, axis=-1, keepdims=True) + EPS) * g


def _ffn_half(x, g_ref, wg_ref, wu_ref, wd_ref):
    hn = _rms(x, g_ref[...]).astype(BF16)
    g = _dot(hn, wg_ref[...])
    u = _dot(hn, wu_ref[...])
    act = (g * _sigmoid(g) * u).astype(BF16)
    return x + 0.5 * _dot(act, wd_ref[...])


def _split3(x):
    a = x.astype(BF16)
    r = x - a.astype(F32)
    b = r.astype(BF16)
    c = (r - b.astype(F32)).astype(BF16)
    return a, b, c


def _store_heads(o_ref, x):
    heads = x.shape[1] // HEAD_DIM
    for h in range(heads):
        o_ref[pl.ds(h, x.shape[0], stride=heads), :] = x[:, h * HEAD_DIM:(h + 1) * HEAD_DIM]


def _dense_in_kernel(x_ref, n1_ref, wg_ref, wu_ref, wd_ref, n2_ref, wp_ref, bf_ref, *refs):
    (x1_ref, qf_ref, kf_ref, vf_ref, lfp_ref, u_ref, qs_ref, ks_ref, vs_ref,
     kfb_ref, vfb_ref, ksb_ref, vsb_ref, lf_ref) = refs[-14:]
    x1 = _ffn_half(x_ref[...], n1_ref, wg_ref, wu_ref, wd_ref)
    x1_ref[...] = x1
    h = _rms(x1, n2_ref[...]).astype(BF16)
    z = _dot(h, wp_ref[...])
    scale = HEAD_DIM ** -0.5 * LOG2E
    qf_ref[...] = (z[:, ZQ:ZK] * scale).astype(BF16)
    kf, vf = z[:, ZK:ZV], z[:, ZV:ZF]
    _store_heads(kf_ref, kf)
    _store_heads(vf_ref, vf)
    kfb_ref[...] = kf.astype(BF16)
    vfb_ref[...] = vf.astype(BF16)
    zf = z[:, ZF:ZA] + bf_ref[...]
    lane = lax.broadcasted_iota(jnp.int32, zf.shape, 1)
    lf = jnp.where(lane < H_FOX, -_softplus(-zf), 0.0)
    lfp_ref[...] = lf
    lf_ref[...] = lf[:, :H_FOX]
    u_ref[...] = z[:, ZA:ZG] * _sigmoid(z[:, ZG:ZQS])
    qs_ref[...] = (z[:, ZQS:ZKS] * scale).astype(BF16)
    ks, vs = z[:, ZKS:ZVS], z[:, ZVS:ZEND]
    _store_heads(ks_ref, ks)
    _store_heads(vs_ref, vs)
    ksb_ref[...] = ks.astype(BF16)
    vsb_ref[...] = vs.astype(BF16)


def _const_spec(shape):
    return pl.BlockSpec(shape, lambda *_: (0,) * len(shape), pipeline_mode=pl.Buffered(1))


STACKED = {2: (H_FOX, HEAD_DIM), 3: (H_FOX, HEAD_DIM), 7: (H_SB, HEAD_DIM), 8: (H_SB, HEAD_DIM), 13: (1, H_FOX)}


def _dense_in(x, n1, wg, wu, wd, n2, wp, bfp, tm, layer, depth, stacked):
    n = x.shape[0]
    nt = n // tm
    widths = (D_MODEL, D_FOX, D_FOX, D_FOX, LANES, C_CONV, D_SB, D_SB, D_SB, D_FOX, D_FOX, D_SB, D_SB, H_FOX)
    dtypes = (F32, BF16, F32, F32, F32, F32, BF16, F32, F32, BF16, BF16, BF16, BF16, F32)
    specs, shapes = [], []
    for k, (w, d) in enumerate(zip(widths, dtypes)):
        if k in STACKED:
            per_token, width = STACKED[k]
            specs.append(pl.BlockSpec((tm * per_token, width), lambda i: (i + layer * nt, 0)))
            shapes.append(jax.ShapeDtypeStruct((depth * n * per_token, width), d))
        else:
            specs.append(pl.BlockSpec((tm, w), lambda i: (i, 0)))
            shapes.append(jax.ShapeDtypeStruct((n, w), d))
    prev = () if stacked is None else tuple(stacked)
    n_fixed = 8
    return pl.pallas_call(
        _dense_in_kernel,
        grid=(nt,),
        in_specs=[pl.BlockSpec((tm, D_MODEL), lambda i: (i, 0)), _const_spec((1, D_MODEL)),
                  _const_spec((D_MODEL, D_FF)), _const_spec((D_MODEL, D_FF)), _const_spec((D_FF, D_MODEL)),
                  _const_spec((1, D_MODEL)), _const_spec((D_MODEL, ZEND)), _const_spec((1, LANES))]
                 + [pl.BlockSpec(memory_space=pl.ANY)] * len(prev),
        out_specs=specs,
        out_shape=shapes,
        input_output_aliases={n_fixed + a: k for a, k in enumerate(STACKED)} if prev else {},
        compiler_params=pltpu.CompilerParams(dimension_semantics=("arbitrary",),
                                             vmem_limit_bytes=VMEM_LIMIT),
        name="dense_in",
    )(x, n1, wg, wu, wd, n2, wp, bfp, *prev)


def _dense_out_kernel(x1_ref, yf_ref, yc_ref, ys_ref, wo_ref, n_ref, wg_ref, wu_ref, wd_ref, fn_ref,
                      o_ref, *, final):
    x2 = (x1_ref[...] + _dot(yf_ref[...], wo_ref[0:D_FOX, :])
          + _dot(yc_ref[...], wo_ref[D_FOX:D_FOX + C_CONV, :])
          + _dot(ys_ref[...], wo_ref[D_FOX + C_CONV:, :]))
    x3 = _ffn_half(x2, n_ref, wg_ref, wu_ref, wd_ref)
    o_ref[...] = _rms(x3, fn_ref[...]) if final else x3


def _dense_out(x1, yf, yc, ys, wo, n, wg, wu, wd, fn, tm, final):
    nrow = x1.shape[0]
    row = lambda w: pl.BlockSpec((tm, w), lambda i: (i, 0))
    return pl.pallas_call(
        functools.partial(_dense_out_kernel, final=final),
        grid=(nrow // tm,),
        in_specs=[row(D_MODEL), row(D_FOX), row(C_CONV), row(D_SB), _const_spec((D_MODEL, D_MODEL)),
                  _const_spec((1, D_MODEL)), _const_spec((D_MODEL, D_FF)), _const_spec((D_MODEL, D_FF)),
                  _const_spec((D_FF, D_MODEL)), _const_spec((1, D_MODEL))],
        out_specs=row(D_MODEL),
        out_shape=jax.ShapeDtypeStruct((nrow, D_MODEL), F32),
        compiler_params=pltpu.CompilerParams(dimension_semantics=("arbitrary",),
                                             vmem_limit_bytes=VMEM_LIMIT),
        name="dense_out",
    )(x1, yf, yc, ys, wo, n, wg, wu, wd, fn)


def _prep_kernel(kf_ref, vf_ref, vs_ref, lfp_ref, tri_ref, sel_ref, hsel_ref,
                 kt_ref, vft_ref, vst_ref, bias_max_ref, knorm_max_ref, carry_ref, bmax_ref, kmax_ref):
    i = pl.program_id(1)

    @pl.when(i == 0)
    def _():
        carry_ref[...] = jnp.zeros_like(carry_ref)
        bmax_ref[...] = jnp.full_like(bmax_ref, -jnp.inf)
        kmax_ref[...] = jnp.zeros_like(kmax_ref)

    tri = tri_ref[...]
    cum = carry_ref[...]
    for part in _split3(lfp_ref[0]):
        cum = cum + _dot(tri, part)
    carry_ref[...] = cum[cum.shape[0] - 1:, :]
    bias = -cum * LOG2E
    aug = None
    for s, part in enumerate(_split3(bias)):
        term = _dot(part, sel_ref[s])
        aug = term if aug is None else aug + term
    kb = kf_ref[0]
    for hp in range(H_FOX // 2):
        base = 2 * LANES * hp
        kt_ref[0, :, base:base + LANES] = kb[:, hp * LANES:(hp + 1) * LANES]
        kt_ref[0, :, base + LANES:base + 2 * LANES] = aug[:, hp * LANES:(hp + 1) * LANES].astype(BF16)
    vft_ref[0] = jnp.transpose(vf_ref[0].astype(F32)).astype(BF16)
    vst_ref[0] = jnp.transpose(vs_ref[0].astype(F32)).astype(BF16)

    ksq = kb.astype(F32)
    ksq = ksq * ksq
    norm2 = _dot(ksq.astype(BF16), hsel_ref[...])
    kmax = jnp.maximum(kmax_ref[...], jnp.max(norm2, axis=0, keepdims=True))
    bmax = jnp.maximum(bmax_ref[...], jnp.max(bias, axis=0, keepdims=True))
    kmax_ref[...] = kmax
    bmax_ref[...] = bmax
    knorm_max_ref[0, pl.ds(i, 1), :] = jnp.sqrt(kmax) * NORM_SLACK
    bias_max_ref[0, pl.ds(i, 1), :] = bmax


def _prep(kf, vf, vs, lfp, tp):
    b, tk, _ = lfp.shape
    nblk = tk // tp
    r = jnp.arange(tp)
    tri = (r[None, :] <= r[:, None]).astype(BF16)
    h = jnp.arange(LANES)
    col = jnp.arange(D_FOX)
    sel = jnp.stack([(col[None, :] == ((h // 2) * LANES + (h % 2) * N_SPLIT + s)[:, None])
                     & (h[:, None] < H_FOX) for s in range(N_SPLIT)]).astype(BF16)
    hsel = (col[:, None] // HEAD_DIM == h[None, :]).astype(BF16)
    blk = lambda w: pl.BlockSpec((1, tp, w), lambda bi, i: (bi, i, 0))
    blk_t = lambda w: pl.BlockSpec((1, w, tp), lambda bi, i: (bi, 0, i))
    per_batch = pl.BlockSpec((1, nblk, LANES), lambda bi, i: (bi, 0, 0))
    return pl.pallas_call(
        _prep_kernel,
        grid=(b, nblk),
        in_specs=[blk(D_FOX), blk(D_FOX), blk(D_SB), blk(LANES), _const_spec((tp, tp)),
                  _const_spec((N_SPLIT, LANES, D_FOX)), _const_spec((D_FOX, LANES))],
        out_specs=[blk(2 * D_FOX), blk_t(D_FOX), blk_t(D_SB), per_batch, per_batch],
        out_shape=[jax.ShapeDtypeStruct((b, tk, 2 * D_FOX), BF16),
                   jax.ShapeDtypeStruct((b, D_FOX, tk), BF16),
                   jax.ShapeDtypeStruct((b, D_SB, tk), BF16),
                   jax.ShapeDtypeStruct((b, nblk, LANES), F32),
                   jax.ShapeDtypeStruct((b, nblk, LANES), F32)],
        scratch_shapes=[pltpu.VMEM((1, LANES), F32)] * 3,
        compiler_params=pltpu.CompilerParams(dimension_semantics=("arbitrary", "arbitrary"),
                                             vmem_limit_bytes=VMEM_LIMIT),
        name="prep",
    )(kf, vf, vs, lfp, tri, sel, hsel)


def _conv_kernel(u_ref, prev_ref, buf_ref, w_ref, b_ref, g_ref, beta_ref, y_ref, xw_ref, sh_ref, *, tt, rows):
    first = pl.program_id(1) == 0
    xw_ref[0:HALO, :] = jnp.where(first, buf_ref[0], prev_ref[0])
    xw_ref[HALO:HALO + tt, :] = u_ref[0]
    off = HALO - (CONV_W - 1)
    for s in range(1, SUBLANES):
        sh_ref[s - 1] = xw_ref[s:s + tt + HALO - SUBLANES, :]
    for r0 in range(0, tt, rows):
        acc = jnp.zeros((rows, C_CONV), F32)
        for j in range(CONV_W):
            s, base = (j + off) % SUBLANES, (j + off) // SUBLANES * SUBLANES
            src = xw_ref if s == 0 else sh_ref.at[s - 1]
            acc = acc + w_ref[j:j + 1, :] * src[base + r0:base + r0 + rows, :]
        y = acc + b_ref[...]
        mu = jnp.mean(y, axis=-1, keepdims=True)
        yc = y - mu
        var = jnp.mean(yc * yc, axis=-1, keepdims=True)
        y = yc * lax.rsqrt(var + EPS) * g_ref[...] + beta_ref[...]
        y_ref[0, r0:r0 + rows, :] = (y * _sigmoid(y)).astype(BF16)


def _conv(u, buf_pad, w, b, g, beta, tt):
    bsz, t, _ = u.shape
    rows = min(tt, CONV_CHUNK)
    per = tt // HALO
    return pl.pallas_call(
        functools.partial(_conv_kernel, tt=tt, rows=rows),
        grid=(bsz, t // tt),
        in_specs=[pl.BlockSpec((1, tt, C_CONV), lambda bi, i: (bi, i, 0)),
                  pl.BlockSpec((1, HALO, C_CONV), lambda bi, i: (bi, jnp.maximum(i * per - 1, 0), 0)),
                  pl.BlockSpec((1, HALO, C_CONV), lambda bi, i: (bi, 0, 0)),
                  _const_spec((CONV_W, C_CONV)), _const_spec((1, C_CONV)), _const_spec((1, C_CONV)),
                  _const_spec((1, C_CONV))],
        out_specs=pl.BlockSpec((1, tt, C_CONV), lambda bi, i: (bi, i, 0)),
        out_shape=jax.ShapeDtypeStruct((bsz, t, C_CONV), BF16),
        scratch_shapes=[pltpu.VMEM((HALO + tt, C_CONV), F32),
                        pltpu.VMEM((SUBLANES - 1, HALO + tt - SUBLANES, C_CONV), F32)],
        compiler_params=pltpu.CompilerParams(dimension_semantics=("arbitrary", "arbitrary")),
        name="conv",
    )(u, u, buf_pad, w, b, g, beta)


def _query_heads(q_ref):
    qt = jnp.transpose(q_ref[0].astype(F32))
    row = lax.broadcasted_iota(jnp.int32, qt.shape, 0)
    return row, jnp.where(row < HEAD_DIM, qt, 0.0), jnp.where(row >= HEAD_DIM, qt, 0.0)


def _diag_block(i, tq, tk, past):
    return (past + i * tq + tq - 1) // tk


def _positions(i, tq, tk, past):
    key_pos = lax.broadcasted_iota(jnp.int32, (tk, tq), 0)
    q_pos = past + i * tq + lax.broadcasted_iota(jnp.int32, (tk, tq), 1)
    return key_pos, q_pos


def _merge_heads(out_a, out_b):
    return jnp.transpose(jnp.concatenate([out_a, out_b], axis=0))


def _grid_steps(nq):
    return nq + 1 if nq > 1 else 1


def _pipelined_descend(i, nq, diag, start, clear, first, second, finish, reaches, state_ref):
    has_new = i < nq
    jd = diag(jnp.minimum(i, nq - 1))

    @pl.when(i == 0)
    def _():
        start()
        clear()
        first(jd, True, 0)
        state_ref[0] = reaches(jd - 1).astype(jnp.int32)
        state_ref[2] = 1

    for slot in range(2 if nq > 1 else 0):
        waited = jnp.logical_and(i > 0, state_ref[2] == slot)

        @pl.when(jnp.logical_and(waited, has_new))
        def _():
            start()
            first(jd, True, 1 - slot)
            state_ref[0] = reaches(jd - 1).astype(jnp.int32)
            second(state_ref[1], slot)
            finish()
            clear()

        @pl.when(jnp.logical_and(waited, jnp.logical_not(has_new)))
        def _():
            second(state_ref[1], slot)
            finish()

    base = 1 - state_ref[2]
    waiting = lambda j: (base + jd - j - 1) & 1

    def cond(c):
        j, go = c
        return jnp.logical_and(j >= 0, go)

    def body(c):
        j, _ = c
        for slot in range(2):
            @pl.when(waiting(j) == slot)
            def _():
                first(j, False, 1 - slot)
                state_ref[0] = reaches(j - 1).astype(jnp.int32)
                second(j + 1, slot)
        return j - 1, state_ref[0] != 0

    go0 = jnp.logical_and(has_new, state_ref[0] != 0)
    j_stop, _ = lax.while_loop(cond, body, (jd - 1, go0))
    if nq > 1:
        state_ref[1] = j_stop + 1
        state_ref[2] = waiting(j_stop)
    else:
        for slot in range(2):
            @pl.when(waiting(j_stop) == slot)
            def _():
                second(j_stop + 1, slot)
                finish()


def _fox_kernel(bias_max_ref, knorm_max_ref, q_ref, k_ref, vt_ref, o_ref,
                m_ref, acc_ref, s_ref, mblk_ref, alpha_ref, state_ref, *, tq, tk, past, nblk, nq):
    bi = pl.program_id(0)
    hg = pl.program_id(1)
    step = pl.program_id(2)
    i = jnp.minimum(step, nq - 1)
    q_heads, q_norms = [], []
    for g in range(FOX_HEADS // 2):
        row, qa, qb = _query_heads(q_ref.at[:, :, g * LANES:(g + 1) * LANES])
        ones_a = jnp.where(row < N_SPLIT, 1.0, 0.0)
        ones_b = jnp.where(row < 2 * N_SPLIT, 1.0, 0.0) - ones_a
        q_heads += [jnp.concatenate([qa, ones_a], axis=0).astype(BF16),
                    jnp.concatenate([qb, ones_b], axis=0).astype(BF16)]
        q_norms += [jnp.sqrt(jnp.sum(qh * qh, axis=0, keepdims=True)) * NORM_SLACK for qh in (qa, qb)]
    key_pos, q_pos = _positions(i, tq, tk, past)

    def start():
        m_ref[...] = jnp.full_like(m_ref, -jnp.inf)

    def clear():
        acc_ref[...] = jnp.zeros_like(acc_ref)

    def score(j, masked, slot):
        start = pl.multiple_of(j * tk, tk)
        scores = [_dot(k_ref[0, pl.ds(start, tk), (h // 2) * 2 * LANES:(h // 2 + 1) * 2 * LANES], q_heads[h])
                  for h in range(FOX_HEADS)]
        for h in range(FOX_HEADS):
            s = scores[h]
            if masked:
                s = jnp.where(start + key_pos <= q_pos, s, -jnp.inf)
            m_prev = m_ref[h]
            m_new = jnp.maximum(m_prev, jnp.max(s, axis=0, keepdims=True))
            alpha_ref[slot, h] = jnp.exp2(m_prev - m_new)
            mblk_ref[slot, h] = m_new
            m_ref[h] = m_new
            s_ref[slot, h] = s

    def accumulate(j, slot):
        start = pl.multiple_of(j * tk, tk)
        ones = jnp.ones((ONES_ROWS, tk), BF16)
        for h in range(FOX_HEADS):
            p = jnp.exp2(s_ref[slot, h] - mblk_ref[slot, h])
            vt = vt_ref[0, h * HEAD_DIM:(h + 1) * HEAD_DIM, pl.ds(start, tk)]
            pv = _dot(jnp.concatenate([vt, ones], axis=0), p.astype(BF16))
            acc_ref[h] = alpha_ref[slot, h] * acc_ref[h] + pv

    def reaches(j):
        base = (bi * nblk + jnp.maximum(j, 0)) * H_FOX + FOX_HEADS * hg
        worst = None
        for h in range(FOX_HEADS):
            top = q_norms[h] * knorm_max_ref[base + h] + bias_max_ref[base + h] - m_ref[h]
            worst = top if worst is None else jnp.maximum(worst, top)
        return jnp.max(worst) >= -SKIP_LOG2

    def finish():
        for g in range(FOX_HEADS // 2):
            o_ref[0, :, g * LANES:(g + 1) * LANES] = _merge_heads(
                *[acc_ref[h, :HEAD_DIM] / acc_ref[h, HEAD_DIM:HEAD_DIM + 1]
                  for h in (2 * g, 2 * g + 1)]).astype(BF16)

    _pipelined_descend(step, nq, lambda qi: _diag_block(qi, tq, tk, past), start, clear,
                       score, accumulate, finish, reaches, state_ref)


def _fox(q, kt, vt, bias_max, knorm_max, past, tq, tk):
    b, t, _ = q.shape
    tkeys = kt.shape[1]
    nblk = tkeys // tk
    nq = t // tq
    assert (tk % tq == 0 and past % tk == 0) or tk == tkeys
    smem = pl.BlockSpec(memory_space=pltpu.SMEM)
    per_head = lambda a: a[:, :, :H_FOX].reshape(-1)
    width = FOX_HEADS * HEAD_DIM
    resident = lambda shape, imap: pl.BlockSpec(
        shape, imap, pipeline_mode=pl.Buffered(1 if nblk > 1 else 2))
    return pl.pallas_call(
        functools.partial(_fox_kernel, tq=tq, tk=tk, past=past, nblk=nblk, nq=nq),
        grid=(b, H_FOX // FOX_HEADS, _grid_steps(nq)),
        in_specs=[smem, smem,
                  pl.BlockSpec((1, tq, width), lambda bi, hg, i: (bi, jnp.minimum(i, nq - 1), hg)),
                  resident((1, tkeys, 2 * width), lambda bi, hg, i: (bi, 0, hg)),
                  resident((1, width, tkeys), lambda bi, hg, i: (bi, hg, 0))],
        out_specs=pl.BlockSpec((1, tq, width), lambda bi, hg, i: (bi, jnp.maximum(i - 1, 0), hg)),
        out_shape=jax.ShapeDtypeStruct((b, t, D_FOX), BF16),
        scratch_shapes=[pltpu.VMEM((FOX_HEADS, 1, tq), F32),
                        pltpu.VMEM((FOX_HEADS, HEAD_DIM + ONES_ROWS, tq), F32),
                        pltpu.VMEM((2, FOX_HEADS, tk, tq), F32),
                        pltpu.VMEM((2, FOX_HEADS, 1, tq), F32), pltpu.VMEM((2, FOX_HEADS, 1, tq), F32),
                        pltpu.SMEM((3,), jnp.int32)],
        compiler_params=pltpu.CompilerParams(dimension_semantics=("arbitrary",) * 3,
                                             vmem_limit_bytes=VMEM_LIMIT),
        name="fox",
    )(per_head(bias_max), per_head(knorm_max), q, kt, vt)


def _sb_kernel(q_ref, k_ref, vt_ref, tri_ref, o_ref, carry_ref, acc_ref, t_ref, later_ref, state_ref,
               *, tq, tk, past, nq):
    step = pl.program_id(1)
    i = jnp.minimum(step, nq - 1)
    q_heads = []
    for g in range(H_SB // 2):
        _, qa, qb = _query_heads(q_ref.at[:, :, g * LANES:(g + 1) * LANES])
        q_heads += [qa.astype(BF16), qb.astype(BF16)]
    key_pos, q_pos = _positions(i, tq, tk, past)

    def start():
        carry_ref[...] = jnp.zeros_like(carry_ref)

    def clear():
        acc_ref[...] = jnp.zeros_like(acc_ref)

    def logits(j, masked, slot):
        start = pl.multiple_of(j * tk, tk)
        tri2 = tri_ref[...]
        scores = [_dot(k_ref[0, pl.ds(start, tk), (h // 2) * LANES:(h // 2 + 1) * LANES], q_heads[h])
                  for h in range(H_SB)]
        for h in range(H_SB):
            z = scores[h]
            nl = jnp.maximum(z, 0.0) + jnp.log(1.0 + jnp.exp2(-jnp.abs(z))) * LOG2E
            if masked:
                valid = start + key_pos < q_pos
                nl = jnp.where(valid, nl, 0.0)
            hi = nl.astype(BF16)
            lo = (nl - hi.astype(F32)).astype(BF16)
            suffix = _dot(tri2, jnp.concatenate([hi, lo], axis=0))
            t = z - nl - suffix
            t_ref[slot, h] = jnp.where(valid, t, -jnp.inf) if masked else t
            later = carry_ref[h]
            later_ref[slot, h] = later
            carry_ref[h] = later + jnp.sum(nl, axis=0, keepdims=True)

    def weigh(j, slot):
        start = pl.multiple_of(j * tk, tk)
        for h in range(H_SB):
            a = jnp.exp2(t_ref[slot, h] - later_ref[slot, h])
            vt = vt_ref[0, h * HEAD_DIM:(h + 1) * HEAD_DIM, pl.ds(start, tk)]
            acc_ref[h] = acc_ref[h] + _dot(vt, a.astype(BF16))

    def reaches(j):
        least = carry_ref[0]
        for h in range(1, H_SB):
            least = jnp.minimum(least, carry_ref[h])
        return jnp.min(least) <= SKIP_LOG2

    def finish():
        for g in range(H_SB // 2):
            o_ref[0, :, g * LANES:(g + 1) * LANES] = _merge_heads(
                acc_ref[2 * g], acc_ref[2 * g + 1]).astype(BF16)

    _pipelined_descend(step, nq, lambda qi: _diag_block(qi, tq, tk, past), start, clear,
                       logits, weigh, finish, reaches, state_ref)


def _sb(q, k, vt, past, tq, tk):
    b, t, _ = q.shape
    tkeys = k.shape[1]
    nq = t // tq
    assert (tk % tq == 0 and past % tk == 0) or tk == tkeys
    r = jnp.arange(tk)
    tri = (r[None, :] > r[:, None]).astype(BF16)
    tri2 = jnp.concatenate([tri, tri], axis=1)
    per_batch = lambda shape: pl.BlockSpec(shape, lambda bi, i: (bi, 0, 0), pipeline_mode=pl.Buffered(1))
    return pl.pallas_call(
        functools.partial(_sb_kernel, tq=tq, tk=tk, past=past, nq=nq),
        grid=(b, _grid_steps(nq)),
        in_specs=[pl.BlockSpec((1, tq, D_SB), lambda bi, i: (bi, jnp.minimum(i, nq - 1), 0)),
                  per_batch((1, tkeys, D_SB)), per_batch((1, D_SB, tkeys)), _const_spec((tk, 2 * tk))],
        out_specs=pl.BlockSpec((1, tq, D_SB), lambda bi, i: (bi, jnp.maximum(i - 1, 0), 0)),
        out_shape=jax.ShapeDtypeStruct((b, t, D_SB), BF16),
        scratch_shapes=[pltpu.VMEM((H_SB, 1, tq), F32), pltpu.VMEM((H_SB, HEAD_DIM, tq), F32),
                        pltpu.VMEM((2, H_SB, tk, tq), F32), pltpu.VMEM((2, H_SB, 1, tq), F32),
                        pltpu.SMEM((3,), jnp.int32)],
        compiler_params=pltpu.CompilerParams(dimension_semantics=("arbitrary",) * 2,
                                             vmem_limit_bytes=VMEM_LIMIT),
        name="sb",
    )(q, k, vt, tri2)


def _pick_tile(n, pref):
    t = min(n, pref)
    while n % t:
        t //= 2
    return t


def _layer(x, past, w, final_norm, layer, depth, stacked):
    b, t, _ = x.shape
    past_k, past_v, past_lf, past_sk, past_sv, conv_buf = past
    p = past_k.shape[1]
    assert t >= CONV_W - 1 and t % HALO == 0
    n = b * t
    tm = _pick_tile(n, DENSE_ROWS)

    (x1, qf, kf, vf, lfp, u, qs, ks, vs, kfb, vfb, ksb, vsb, lf) = _dense_in(
        x.reshape(n, D_MODEL), w["n1"], w["wg1"], w["wu1"], w["wd1"], w["n2"], w["wp"], w["bfp"], tm,
        layer, depth, stacked)

    r3 = lambda a: a.reshape(-1, t, a.shape[-1])
    lfp3, u3 = r3(lfp), r3(u)
    if p:
        fill = -(p + t) % LANES
        cat = lambda old, new: jnp.concatenate(
            [old.reshape(b, p, -1).astype(new.dtype), new,
             jnp.zeros((b, fill, new.shape[-1]), new.dtype)], axis=1)
        kf_all, vf_all, vs_all = cat(past_k, r3(kfb)), cat(past_v, r3(vfb)), cat(past_sv, r3(vsb))
        lf_all = cat(jnp.pad(past_lf, ((0, 0), (0, 0), (0, LANES - H_FOX))), lfp3)
        ks_all = cat(past_sk, r3(ksb))
        tq_fox = tq_sb = t
        tk_fox = p + t + fill
        tk_sb = LANES if (LANES % t == 0 and p % LANES == 0) else tk_fox
    else:
        kf_all, vf_all, vs_all, lf_all, ks_all = r3(kfb), r3(vfb), r3(vsb), lfp3, r3(ksb)
        tq_sb = tk_sb = _pick_tile(t, SB_BLOCK)
        tq_fox = tk_fox = _pick_tile(t, FOX_BLOCK)

    kt, vft, vst, bias_max, knorm_max = _prep(kf_all, vf_all, vs_all, lf_all, tk_fox)
    yf = _fox(r3(qf), kt, vft, bias_max, knorm_max, p, tq_fox, tk_fox)
    ys = _sb(r3(qs), ks_all, vst, p, tq_sb, tk_sb)
    buf_pad = jnp.pad(conv_buf, ((0, 0), (HALO - (CONV_W - 1), 0), (0, 0)))
    yc = _conv(u3, buf_pad, w["conv_w"], w["conv_b"], w["ln_g"], w["ln_b"], _pick_tile(t, CONV_ROWS))

    flat = lambda a: a.reshape(n, a.shape[-1])
    xo = _dense_out(x1, flat(yf), flat(yc), flat(ys), w["wo"], w["n3"], w["wg2"], w["wu2"], w["wd2"],
                    final_norm, tm, layer == depth - 1)
    return xo.reshape(b, t, D_MODEL), (kf, vf, ks, vs, lf), u3[:, t - (CONV_W - 1):, :]


def _layer_weights(l, ffn_norm, ffn_gate, ffn_up, ffn_down, mix_norm, w_in, b_forget,
                   conv_w, conv_b, conv_ln_g, conv_ln_b, w_out):
    wi = w_in[l]
    off_f = 3 * D_FOX
    off_glu = off_f + H_FOX
    off_qc = off_glu + 2 * C_CONV
    wp = jnp.concatenate([wi[:, :off_f], jnp.pad(wi[:, off_f:off_glu], ((0, 0), (0, LANES - H_FOX))),
                          wi[:, off_glu:off_qc], wi[:, off_qc:]], axis=1).astype(BF16)
    row = lambda a: a.reshape(1, -1).astype(F32)
    return dict(
        n1=row(ffn_norm[l, 0]), wg1=ffn_gate[l, 0].astype(BF16), wu1=ffn_up[l, 0].astype(BF16),
        wd1=ffn_down[l, 0].astype(BF16), n2=row(mix_norm[l]), wp=wp,
        bfp=jnp.pad(row(b_forget[l]), ((0, 0), (0, LANES - H_FOX))),
        conv_w=conv_w[l], conv_b=row(conv_b[l]), ln_g=row(conv_ln_g[l]), ln_b=row(conv_ln_b[l]),
        wo=w_out[l].astype(BF16), n3=row(ffn_norm[l, 1]), wg2=ffn_gate[l, 1].astype(BF16),
        wu2=ffn_up[l, 1].astype(BF16), wd2=ffn_down[l, 1].astype(BF16))


def kernel(x_prompt, x_sample, cache_fox_k, cache_fox_v, cache_fox_logf, cache_sb_k, cache_sb_v, state_conv,
           ffn_norm, ffn_gate, ffn_up, ffn_down, mix_norm, w_in, b_forget, conv_w, conv_b, conv_ln_g,
           conv_ln_b, w_out, final_norm):
    depth = w_in.shape[0]
    bp = x_prompt.shape[0]
    dt = x_prompt.dtype
    empty = (jnp.zeros((bp, 0, H_FOX, HEAD_DIM), dt), jnp.zeros((bp, 0, H_FOX, HEAD_DIM), dt),
             jnp.zeros((bp, 0, H_FOX), dt), jnp.zeros((bp, 0, H_SB, HEAD_DIM), dt),
             jnp.zeros((bp, 0, H_SB, HEAD_DIM), dt), jnp.zeros((bp, CONV_W - 1, C_CONV), dt))
    fn = final_norm.reshape(1, -1).astype(F32)
    xp, xs = x_prompt, x_sample
    stacked_p = stacked_s = None
    conv_p, conv_s = [], []
    for l in range(depth):
        w = _layer_weights(l, ffn_norm, ffn_gate, ffn_up, ffn_down, mix_norm, w_in, b_forget,
                           conv_w, conv_b, conv_ln_g, conv_ln_b, w_out)
        xp, stacked_p, cp = _layer(xp, empty, w, fn, l, depth, stacked_p)
        cache = (cache_fox_k[l], cache_fox_v[l], cache_fox_logf[l], cache_sb_k[l], cache_sb_v[l],
                 state_conv[l])
        xs, stacked_s, cs = _layer(xs, cache, w, fn, l, depth, stacked_s)
        conv_p.append(cp)
        conv_s.append(cs)

    def outputs(x, stacked, conv):
        b, t, _ = x.shape
        heads = lambda a: a.reshape(depth, b, t, -1, HEAD_DIM)
        kf, vf, ks, vs, lf = stacked
        return heads(kf), heads(vf), lf.reshape(depth, b, t, H_FOX), heads(ks), heads(vs), jnp.stack(conv, 0)

    return (xp, xs) + outputs(xp, stacked_p, conv_p) + outputs(xs, stacked_s, conv_s)
```

```python
import functools

import jax
import jax.numpy as jnp
from jax import lax
from jax.experimental import pallas as pl
from jax.experimental.pallas import tpu as pltpu

D_MODEL = 1024
HEAD_DIM = 64
D_FOX = 512
C_CONV = 256
D_SB = 256
H_FOX = 8
H_SB = 4
CONV_W = 31
D_FF = 2816
EPS = 1e-6

LANES = 128
SUBLANES = 8
HALO = 32
ONES_ROWS = 16
FOX_HEADS = 4
DENSE_ROWS = 512
FOX_BLOCK = 512
SB_BLOCK = 256
CONV_ROWS = 1024
CONV_CHUNK = 64
N_SPLIT = 3
VMEM_LIMIT = 56 * 1024 * 1024
SKIP_LOG = 105.0
LOG2E = 1.4426950408889634
SKIP_LOG2 = SKIP_LOG * LOG2E
NORM_SLACK = 1.01

ZQ, ZK, ZV, ZF, ZA, ZG, ZQS, ZKS, ZVS, ZEND = 0, 512, 1024, 1536, 1664, 1920, 2176, 2432, 2688, 2944

F32 = jnp.float32
BF16 = jnp.bfloat16


def _dot(a, b):
    return jnp.dot(a, b, preferred_element_type=F32)


def _sigmoid(x):
    return 1.0 / (1.0 + jnp.exp(-x))


def _softplus(x):
    return jnp.maximum(x, 0.0) + jnp.log1p(jnp.exp(-jnp.abs(x)))


def _rms(x, g):
    return x * lax.rsqrt(jnp.mean(x * x, axis=-1, keepdims=True) + EPS) * g


def _ffn_half(x, g_ref, wg_ref, wu_ref, wd_ref):
    hn = _rms(x, g_ref[...]).astype(BF16)
    g = _dot(hn, wg_ref[...])
    u = _dot(hn, wu_ref[...])
    act = (g * _sigmoid(g) * u).astype(BF16)
    return x + 0.5 * _dot(act, wd_ref[...])


def _split3(x):
    a = x.astype(BF16)
    r = x - a.astype(F32)
    b = r.astype(BF16)
    c = (r - b.astype(F32)).astype(BF16)
    return a, b, c


def _store_heads(o_ref, x):
    heads = x.shape[1] // HEAD_DIM
    for h in range(heads):
        o_ref[pl.ds(h, x.shape[0], stride=heads), :] = x[:, h * HEAD_DIM:(h + 1) * HEAD_DIM]


def _dense_in_kernel(x_ref, n1_ref, wg_ref, wu_ref, wd_ref, n2_ref, wp_ref, bf_ref, *refs):
    (x1_ref, qf_ref, kf_ref, vf_ref, lfp_ref, u_ref, qs_ref, ks_ref, vs_ref,
     kfb_ref, vfb_ref, ksb_ref, vsb_ref, lf_ref) = refs[-14:]
    x1 = _ffn_half(x_ref[...], n1_ref, wg_ref, wu_ref, wd_ref)
    x1_ref[...] = x1
    h = _rms(x1, n2_ref[...]).astype(BF16)
    z = _dot(h, wp_ref[...])
    scale = HEAD_DIM ** -0.5 * LOG2E
    qf_ref[...] = (z[:, ZQ:ZK] * scale).astype(BF16)
    kf, vf = z[:, ZK:ZV], z[:, ZV:ZF]
    _store_heads(kf_ref, kf)
    _store_heads(vf_ref, vf)
    kfb_ref[...] = kf.astype(BF16)
    vfb_ref[...] = vf.astype(BF16)
    zf = z[:, ZF:ZA] + bf_ref[...]
    lane = lax.broadcasted_iota(jnp.int32, zf.shape, 1)
    lf = jnp.where(lane < H_FOX, -_softplus(-zf), 0.0)
    lfp_ref[...] = lf
    lf_ref[...] = lf[:, :H_FOX]
    u_ref[...] = z[:, ZA:ZG] * _sigmoid(z[:, ZG:ZQS])
    qs_ref[...] = (z[:, ZQS:ZKS] * scale).astype(BF16)
    ks, vs = z[:, ZKS:ZVS], z[:, ZVS:ZEND]
    _store_heads(ks_ref, ks)
    _store_heads(vs_ref, vs)
    ksb_ref[...] = ks.astype(BF16)
    vsb_ref[...] = vs.astype(BF16)


def _const_spec(shape):
    return pl.BlockSpec(shape, lambda *_: (0,) * len(shape), pipeline_mode=pl.Buffered(1))


STACKED = {2: (H_FOX, HEAD_DIM), 3: (H_FOX, HEAD_DIM), 7: (H_SB, HEAD_DIM), 8: (H_SB, HEAD_DIM), 13: (1, H_FOX)}


def _dense_in(x, n1, wg, wu, wd, n2, wp, bfp, tm, layer, depth, stacked):
    n = x.shape[0]
    nt = n // tm
    widths = (D_MODEL, D_FOX, D_FOX, D_FOX, LANES, C_CONV, D_SB, D_SB, D_SB, D_FOX, D_FOX, D_SB, D_SB, H_FOX)
    dtypes = (F32, BF16, F32, F32, F32, F32, BF16, F32, F32, BF16, BF16, BF16, BF16, F32)
    specs, shapes = [], []
    for k, (w, d) in enumerate(zip(widths, dtypes)):
        if k in STACKED:
            per_token, width = STACKED[k]
            specs.append(pl.BlockSpec((tm * per_token, width), lambda i: (i + layer * nt, 0)))
            shapes.append(jax.ShapeDtypeStruct((depth * n * per_token, width), d))
        else:
            specs.append(pl.BlockSpec((tm, w), lambda i: (i, 0)))
            shapes.append(jax.ShapeDtypeStruct((n, w), d))
    prev = () if stacked is None else tuple(stacked)
    n_fixed = 8
    return pl.pallas_call(
        _dense_in_kernel,
        grid=(nt,),
        in_specs=[pl.BlockSpec((tm, D_MODEL), lambda i: (i, 0)), _const_spec((1, D_MODEL)),
                  _const_spec((D_MODEL, D_FF)), _const_spec((D_MODEL, D_FF)), _const_spec((D_FF, D_MODEL)),
                  _const_spec((1, D_MODEL)), _const_spec((D_MODEL, ZEND)), _const_spec((1, LANES))]
                 + [pl.BlockSpec(memory_space=pl.ANY)] * len(prev),
        out_specs=specs,
        out_shape=shapes,
        input_output_aliases={n_fixed + a: k for a, k in enumerate(STACKED)} if prev else {},
        compiler_params=pltpu.CompilerParams(dimension_semantics=("arbitrary",),
                                             vmem_limit_bytes=VMEM_LIMIT),
        name="dense_in",
    )(x, n1, wg, wu, wd, n2, wp, bfp, *prev)


def _dense_out_kernel(x1_ref, yf_ref, yc_ref, ys_ref, wo_ref, n_ref, wg_ref, wu_ref, wd_ref, fn_ref,
                      o_ref, *, final):
    x2 = (x1_ref[...] + _dot(yf_ref[...], wo_ref[0:D_FOX, :])
          + _dot(yc_ref[...], wo_ref[D_FOX:D_FOX + C_CONV, :])
          + _dot(ys_ref[...], wo_ref[D_FOX + C_CONV:, :]))
    x3 = _ffn_half(x2, n_ref, wg_ref, wu_ref, wd_ref)
    o_ref[...] = _rms(x3, fn_ref[...]) if final else x3


def _dense_out(x1, yf, yc, ys, wo, n, wg, wu, wd, fn, tm, final):
    nrow = x1.shape[0]
    row = lambda w: pl.BlockSpec((tm, w), lambda i: (i, 0))
    return pl.pallas_call(
        functools.partial(_dense_out_kernel, final=final),
        grid=(nrow // tm,),
        in_specs=[row(D_MODEL), row(D_FOX), row(C_CONV), row(D_SB), _const_spec((D_MODEL, D_MODEL)),
                  _const_spec((1, D_MODEL)), _const_spec((D_MODEL, D_FF)), _const_spec((D_MODEL, D_FF)),
                  _const_spec((D_FF, D_MODEL)), _const_spec((1, D_MODEL))],
        out_specs=row(D_MODEL),
        out_shape=jax.ShapeDtypeStruct((nrow, D_MODEL), F32),
        compiler_params=pltpu.CompilerParams(dimension_semantics=("arbitrary",),
                                             vmem_limit_bytes=VMEM_LIMIT),
        name="dense_out",
    )(x1, yf, yc, ys, wo, n, wg, wu, wd, fn)


def _prep_kernel(kf_ref, vf_ref, vs_ref, lfp_ref, tri_ref, sel_ref, hsel_ref,
                 kt_ref, vft_ref, vst_ref, bias_max_ref, knorm_max_ref, carry_ref, bmax_ref, kmax_ref):
    i = pl.program_id(1)

    @pl.when(i == 0)
    def _():
        carry_ref[...] = jnp.zeros_like(carry_ref)
        bmax_ref[...] = jnp.full_like(bmax_ref, -jnp.inf)
        kmax_ref[...] = jnp.zeros_like(kmax_ref)

    tri = tri_ref[...]
    cum = carry_ref[...]
    for part in _split3(lfp_ref[0]):
        cum = cum + _dot(tri, part)
    carry_ref[...] = cum[cum.shape[0] - 1:, :]
    bias = -cum * LOG2E
    aug = None
    for s, part in enumerate(_split3(bias)):
        term = _dot(part, sel_ref[s])
        aug = term if aug is None else aug + term
    kb = kf_ref[0]
    for hp in range(H_FOX // 2):
        base = 2 * LANES * hp
        kt_ref[0, :, base:base + LANES] = kb[:, hp * LANES:(hp + 1) * LANES]
        kt_ref[0, :, base + LANES:base + 2 * LANES] = aug[:, hp * LANES:(hp + 1) * LANES].astype(BF16)
    vft_ref[0] = jnp.transpose(vf_ref[0].astype(F32)).astype(BF16)
    vst_ref[0] = jnp.transpose(vs_ref[0].astype(F32)).astype(BF16)

    ksq = kb.astype(F32)
    ksq = ksq * ksq
    norm2 = _dot(ksq.astype(BF16), hsel_ref[...])
    kmax = jnp.maximum(kmax_ref[...], jnp.max(norm2, axis=0, keepdims=True))
    bmax = jnp.maximum(bmax_ref[...], jnp.max(bias, axis=0, keepdims=True))
    kmax_ref[...] = kmax
    bmax_ref[...] = bmax
    knorm_max_ref[0, pl.ds(i, 1), :] = jnp.sqrt(kmax) * NORM_SLACK
    bias_max_ref[0, pl.ds(i, 1), :] = bmax


def _prep(kf, vf, vs, lfp, tp):
    b, tk, _ = lfp.shape
    nblk = tk // tp
    r = jnp.arange(tp)
    tri = (r[None, :] <= r[:, None]).astype(BF16)
    h = jnp.arange(LANES)
    col = jnp.arange(D_FOX)
    sel = jnp.stack([(col[None, :] == ((h // 2) * LANES + (h % 2) * N_SPLIT + s)[:, None])
                     & (h[:, None] < H_FOX) for s in range(N_SPLIT)]).astype(BF16)
    hsel = (col[:, None] // HEAD_DIM == h[None, :]).astype(BF16)
    blk = lambda w: pl.BlockSpec((1, tp, w), lambda bi, i: (bi, i, 0))
    blk_t = lambda w: pl.BlockSpec((1, w, tp), lambda bi, i: (bi, 0, i))
    per_batch = pl.BlockSpec((1, nblk, LANES), lambda bi, i: (bi, 0, 0))
    return pl.pallas_call(
        _prep_kernel,
        grid=(b, nblk),
        in_specs=[blk(D_FOX), blk(D_FOX), blk(D_SB), blk(LANES), _const_spec((tp, tp)),
                  _const_spec((N_SPLIT, LANES, D_FOX)), _const_spec((D_FOX, LANES))],
        out_specs=[blk(2 * D_FOX), blk_t(D_FOX), blk_t(D_SB), per_batch, per_batch],
        out_shape=[jax.ShapeDtypeStruct((b, tk, 2 * D_FOX), BF16),
                   jax.ShapeDtypeStruct((b, D_FOX, tk), BF16),
                   jax.ShapeDtypeStruct((b, D_SB, tk), BF16),
                   jax.ShapeDtypeStruct((b, nblk, LANES), F32),
                   jax.ShapeDtypeStruct((b, nblk, LANES), F32)],
        scratch_shapes=[pltpu.VMEM((1, LANES), F32)] * 3,
        compiler_params=pltpu.CompilerParams(dimension_semantics=("arbitrary", "arbitrary"),
                                             vmem_limit_bytes=VMEM_LIMIT),
        name="prep",
    )(kf, vf, vs, lfp, tri, sel, hsel)


def _conv_kernel(u_ref, prev_ref, buf_ref, w_ref, b_ref, g_ref, beta_ref, y_ref, xw_ref, sh_ref, *, tt, rows):
    first = pl.program_id(1) == 0
    xw_ref[0:HALO, :] = jnp.where(first, buf_ref[0], prev_ref[0])
    xw_ref[HALO:HALO + tt, :] = u_ref[0]
    off = HALO - (CONV_W - 1)
    for s in range(1, SUBLANES):
        sh_ref[s - 1] = xw_ref[s:s + tt + HALO - SUBLANES, :]
    for r0 in range(0, tt, rows):
        acc = jnp.zeros((rows, C_CONV), F32)
        for j in range(CONV_W):
            s, base = (j + off) % SUBLANES, (j + off) // SUBLANES * SUBLANES
            src = xw_ref if s == 0 else sh_ref.at[s - 1]
            acc = acc + w_ref[j:j + 1, :] * src[base + r0:base + r0 + rows, :]
        y = acc + b_ref[...]
        mu = jnp.mean(y, axis=-1, keepdims=True)
        yc = y - mu
        var = jnp.mean(yc * yc, axis=-1, keepdims=True)
        y = yc * lax.rsqrt(var + EPS) * g_ref[...] + beta_ref[...]
        y_ref[0, r0:r0 + rows, :] = (y * _sigmoid(y)).astype(BF16)


def _conv(u, buf_pad, w, b, g, beta, tt):
    bsz, t, _ = u.shape
    rows = min(tt, CONV_CHUNK)
    per = tt // HALO
    return pl.pallas_call(
        functools.partial(_conv_kernel, tt=tt, rows=rows),
        grid=(bsz, t // tt),
        in_specs=[pl.BlockSpec((1, tt, C_CONV), lambda bi, i: (bi, i, 0)),
                  pl.BlockSpec((1, HALO, C_CONV), lambda bi, i: (bi, jnp.maximum(i * per - 1, 0), 0)),
                  pl.BlockSpec((1, HALO, C_CONV), lambda bi, i: (bi, 0, 0)),
                  _const_spec((CONV_W, C_CONV)), _const_spec((1, C_CONV)), _const_spec((1, C_CONV)),
                  _const_spec((1, C_CONV))],
        out_specs=pl.BlockSpec((1, tt, C_CONV), lambda bi, i: (bi, i, 0)),
        out_shape=jax.ShapeDtypeStruct((bsz, t, C_CONV), BF16),
        scratch_shapes=[pltpu.VMEM((HALO + tt, C_CONV), F32),
                        pltpu.VMEM((SUBLANES - 1, HALO + tt - SUBLANES, C_CONV), F32)],
        compiler_params=pltpu.CompilerParams(dimension_semantics=("arbitrary", "arbitrary")),
        name="conv",
    )(u, u, buf_pad, w, b, g, beta)


def _query_heads(q_ref):
    qt = jnp.transpose(q_ref[0].astype(F32))
    row = lax.broadcasted_iota(jnp.int32, qt.shape, 0)
    return row, jnp.where(row < HEAD_DIM, qt, 0.0), jnp.where(row >= HEAD_DIM, qt, 0.0)


def _diag_block(i, tq, tk, past):
    return (past + i * tq + tq - 1) // tk


def _positions(i, tq, tk, past):
    key_pos = lax.broadcasted_iota(jnp.int32, (tk, tq), 0)
    q_pos = past + i * tq + lax.broadcasted_iota(jnp.int32, (tk, tq), 1)
    return key_pos, q_pos


def _merge_heads(out_a, out_b):
    return jnp.transpose(jnp.concatenate([out_a, out_b], axis=0))


def _grid_steps(nq):
    return nq + 1 if nq > 1 else 1


def _pipelined_descend(i, nq, diag, start, clear, first, second, finish, reaches, state_ref):
    has_new = i < nq
    jd = diag(jnp.minimum(i, nq - 1))

    @pl.when(i == 0)
    def _():
        start()
        clear()
        first(jd, True, 0)
        state_ref[0] = reaches(jd - 1).astype(jnp.int32)
        state_ref[2] = 1

    for slot in range(2 if nq > 1 else 0):
        waited = jnp.logical_and(i > 0, state_ref[2] == slot)

        @pl.when(jnp.logical_and(waited, has_new))
        def _():
            start()
            first(jd, True, 1 - slot)
            state_ref[0] = reaches(jd - 1).astype(jnp.int32)
            second(state_ref[1], slot)
            finish()
            clear()

        @pl.when(jnp.logical_and(waited, jnp.logical_not(has_new)))
        def _():
            second(state_ref[1], slot)
            finish()

    base = 1 - state_ref[2]
    waiting = lambda j: (base + jd - j - 1) & 1

    def cond(c):
        j, go = c
        return jnp.logical_and(j >= 0, go)

    def body(c):
        j, _ = c
        for slot in range(2):
            @pl.when(waiting(j) == slot)
            def _():
                first(j, False, 1 - slot)
                state_ref[0] = reaches(j - 1).astype(jnp.int32)
                second(j + 1, slot)
        return j - 1, state_ref[0] != 0

    go0 = jnp.logical_and(has_new, state_ref[0] != 0)
    j_stop, _ = lax.while_loop(cond, body, (jd - 1, go0))
    if nq > 1:
        state_ref[1] = j_stop + 1
        state_ref[2] = waiting(j_stop)
    else:
        for slot in range(2):
            @pl.when(waiting(j_stop) == slot)
            def _():
                second(j_stop + 1, slot)
                finish()


def _fox_kernel(bias_max_ref, knorm_max_ref, q_ref, k_ref, vt_ref, o_ref,
                m_ref, acc_ref, s_ref, mblk_ref, alpha_ref, state_ref, *, tq, tk, past, nblk, nq):
    bi = pl.program_id(0)
    hg = pl.program_id(1)
    step = pl.program_id(2)
    i = jnp.minimum(step, nq - 1)
    q_heads, q_norms = [], []
    for g in range(FOX_HEADS // 2):
        row, qa, qb = _query_heads(q_ref.at[:, :, g * LANES:(g + 1) * LANES])
        ones_a = jnp.where(row < N_SPLIT, 1.0, 0.0)
        ones_b = jnp.where(row < 2 * N_SPLIT, 1.0, 0.0) - ones_a
        q_heads += [jnp.concatenate([qa, ones_a], axis=0).astype(BF16),
                    jnp.concatenate([qb, ones_b], axis=0).astype(BF16)]
        q_norms += [jnp.sqrt(jnp.sum(qh * qh, axis=0, keepdims=True)) * NORM_SLACK for qh in (qa, qb)]
    key_pos, q_pos = _positions(i, tq, tk, past)

    def start():
        m_ref[...] = jnp.full_like(m_ref, -jnp.inf)

    def clear():
        acc_ref[...] = jnp.zeros_like(acc_ref)

    def score(j, masked, slot):
        start = pl.multiple_of(j * tk, tk)
        scores = [_dot(k_ref[0, pl.ds(start, tk), (h // 2) * 2 * LANES:(h // 2 + 1) * 2 * LANES], q_heads[h])
                  for h in range(FOX_HEADS)]
        for h in range(FOX_HEADS):
            s = scores[h]
            if masked:
                s = jnp.where(start + key_pos <= q_pos, s, -jnp.inf)
            m_prev = m_ref[h]
            m_new = jnp.maximum(m_prev, jnp.max(s, axis=0, keepdims=True))
            alpha_ref[slot, h] = jnp.exp2(m_prev - m_new)
            mblk_ref[slot, h] = m_new
            m_ref[h] = m_new
            s_ref[slot, h] = s

    def accumulate(j, slot):
        start = pl.multiple_of(j * tk, tk)
        ones = jnp.ones((ONES_ROWS, tk), BF16)
        for h in range(FOX_HEADS):
            p = jnp.exp2(s_ref[slot, h] - mblk_ref[slot, h])
            vt = vt_ref[0, h * HEAD_DIM:(h + 1) * HEAD_DIM, pl.ds(start, tk)]
            pv = _dot(jnp.concatenate([vt, ones], axis=0), p.astype(BF16))
            acc_ref[h] = alpha_ref[slot, h] * acc_ref[h] + pv

    def reaches(j):
        base = (bi * nblk + jnp.maximum(j, 0)) * H_FOX + FOX_HEADS * hg
        worst = None
        for h in range(FOX_HEADS):
            top = q_norms[h] * knorm_max_ref[base + h] + bias_max_ref[base + h] - m_ref[h]
            worst = top if worst is None else jnp.maximum(worst, top)
        return jnp.max(worst) >= -SKIP_LOG2

    def finish():
        for g in range(FOX_HEADS // 2):
            o_ref[0, :, g * LANES:(g + 1) * LANES] = _merge_heads(
                *[acc_ref[h, :HEAD_DIM] / acc_ref[h, HEAD_DIM:HEAD_DIM + 1]
                  for h in (2 * g, 2 * g + 1)]).astype(BF16)

    _pipelined_descend(step, nq, lambda qi: _diag_block(qi, tq, tk, past), start, clear,
                       score, accumulate, finish, reaches, state_ref)


def _fox(q, kt, vt, bias_max, knorm_max, past, tq, tk):
    b, t, _ = q.shape
    tkeys = kt.shape[1]
    nblk = tkeys // tk
    nq = t // tq
    assert (tk % tq == 0 and past % tk == 0) or tk == tkeys
    smem = pl.BlockSpec(memory_space=pltpu.SMEM)
    per_head = lambda a: a[:, :, :H_FOX].reshape(-1)
    width = FOX_HEADS * HEAD_DIM
    resident = lambda shape, imap: pl.BlockSpec(
        shape, imap, pipeline_mode=pl.Buffered(1 if nblk > 1 else 2))
    return pl.pallas_call(
        functools.partial(_fox_kernel, tq=tq, tk=tk, past=past, nblk=nblk, nq=nq),
        grid=(b, H_FOX // FOX_HEADS, _grid_steps(nq)),
        in_specs=[smem, smem,
                  pl.BlockSpec((1, tq, width), lambda bi, hg, i: (bi, jnp.minimum(i, nq - 1), hg)),
                  resident((1, tkeys, 2 * width), lambda bi, hg, i: (bi, 0, hg)),
                  resident((1, width, tkeys), lambda bi, hg, i: (bi, hg, 0))],
        out_specs=pl.BlockSpec((1, tq, width), lambda bi, hg, i: (bi, jnp.maximum(i - 1, 0), hg)),
        out_shape=jax.ShapeDtypeStruct((b, t, D_FOX), BF16),
        scratch_shapes=[pltpu.VMEM((FOX_HEADS, 1, tq), F32),
                        pltpu.VMEM((FOX_HEADS, HEAD_DIM + ONES_ROWS, tq), F32),
                        pltpu.VMEM((2, FOX_HEADS, tk, tq), F32),
                        pltpu.VMEM((2, FOX_HEADS, 1, tq), F32), pltpu.VMEM((2, FOX_HEADS, 1, tq), F32),
                        pltpu.SMEM((3,), jnp.int32)],
        compiler_params=pltpu.CompilerParams(dimension_semantics=("arbitrary",) * 3,
                                             vmem_limit_bytes=VMEM_LIMIT),
        name="fox",
    )(per_head(bias_max), per_head(knorm_max), q, kt, vt)


def _sb_kernel(q_ref, k_ref, vt_ref, tri_ref, o_ref, carry_ref, acc_ref, t_ref, later_ref, state_ref,
               *, tq, tk, past, nq):
    step = pl.program_id(1)
    i = jnp.minimum(step, nq - 1)
    q_heads = []
    for g in range(H_SB // 2):
        _, qa, qb = _query_heads(q_ref.at[:, :, g * LANES:(g + 1) * LANES])
        q_heads += [qa.astype(BF16), qb.astype(BF16)]
    key_pos, q_pos = _positions(i, tq, tk, past)

    def start():
        carry_ref[...] = jnp.zeros_like(carry_ref)

    def clear():
        acc_ref[...] = jnp.zeros_like(acc_ref)

    def logits(j, masked, slot):
        start = pl.multiple_of(j * tk, tk)
        tri2 = tri_ref[...]
        scores = [_dot(k_ref[0, pl.ds(start, tk), (h // 2) * LANES:(h // 2 + 1) * LANES], q_heads[h])
                  for h in range(H_SB)]
        for h in range(H_SB):
            z = scores[h]
            nl = jnp.maximum(z, 0.0) + jnp.log(1.0 + jnp.exp2(-jnp.abs(z))) * LOG2E
            if masked:
                valid = start + key_pos < q_pos
                nl = jnp.where(valid, nl, 0.0)
            hi = nl.astype(BF16)
            lo = (nl - hi.astype(F32)).astype(BF16)
            suffix = _dot(tri2, jnp.concatenate([hi, lo], axis=0))
            t = z - nl - suffix
            t_ref[slot, h] = jnp.where(valid, t, -jnp.inf) if masked else t
            later = carry_ref[h]
            later_ref[slot, h] = later
            carry_ref[h] = later + jnp.sum(nl, axis=0, keepdims=True)

    def weigh(j, slot):
        start = pl.multiple_of(j * tk, tk)
        for h in range(H_SB):
            a = jnp.exp2(t_ref[slot, h] - later_ref[slot, h])
            vt = vt_ref[0, h * HEAD_DIM:(h + 1) * HEAD_DIM, pl.ds(start, tk)]
            acc_ref[h] = acc_ref[h] + _dot(vt, a.astype(BF16))

    def reaches(j):
        least = carry_ref[0]
        for h in range(1, H_SB):
            least = jnp.minimum(least, carry_ref[h])
        return jnp.min(least) <= SKIP_LOG2

    def finish():
        for g in range(H_SB // 2):
            o_ref[0, :, g * LANES:(g + 1) * LANES] = _merge_heads(
                acc_ref[2 * g], acc_ref[2 * g + 1]).astype(BF16)

    _pipelined_descend(step, nq, lambda qi: _diag_block(qi, tq, tk, past), start, clear,
                       logits, weigh, finish, reaches, state_ref)


def _sb(q, k, vt, past, tq, tk):
    b, t, _ = q.shape
    tkeys = k.shape[1]
    nq = t // tq
    assert (tk % tq == 0 and past % tk == 0) or tk == tkeys
    r = jnp.arange(tk)
    tri = (r[None, :] > r[:, None]).astype(BF16)
    tri2 = jnp.concatenate([tri, tri], axis=1)
    per_batch = lambda shape: pl.BlockSpec(shape, lambda bi, i: (bi, 0, 0),
                                           pipeline_mode=pl.Buffered(1 if nq > 1 else 2))
    return pl.pallas_call(
        functools.partial(_sb_kernel, tq=tq, tk=tk, past=past, nq=nq),
        grid=(b, _grid_steps(nq)),
        in_specs=[pl.BlockSpec((1, tq, D_SB), lambda bi, i: (bi, jnp.minimum(i, nq - 1), 0)),
                  per_batch((1, tkeys, D_SB)), per_batch((1, D_SB, tkeys)), _const_spec((tk, 2 * tk))],
        out_specs=pl.BlockSpec((1, tq, D_SB), lambda bi, i: (bi, jnp.maximum(i - 1, 0), 0)),
        out_shape=jax.ShapeDtypeStruct((b, t, D_SB), BF16),
        scratch_shapes=[pltpu.VMEM((H_SB, 1, tq), F32), pltpu.VMEM((H_SB, HEAD_DIM, tq), F32),
                        pltpu.VMEM((2, H_SB, tk, tq), F32), pltpu.VMEM((2, H_SB, 1, tq), F32),
                        pltpu.SMEM((3,), jnp.int32)],
        compiler_params=pltpu.CompilerParams(dimension_semantics=("arbitrary",) * 2,
                                             vmem_limit_bytes=VMEM_LIMIT),
        name="sb",
    )(q, k, vt, tri2)


def _pick_tile(n, pref):
    t = min(n, pref)
    while n % t:
        t //= 2
    return t


def _layer(x, past, w, final_norm, layer, depth, stacked):
    b, t, _ = x.shape
    past_k, past_v, past_lf, past_sk, past_sv, conv_buf = past
    p = past_k.shape[1]
    assert t >= CONV_W - 1 and t % HALO == 0
    n = b * t
    tm = _pick_tile(n, DENSE_ROWS)

    (x1, qf, kf, vf, lfp, u, qs, ks, vs, kfb, vfb, ksb, vsb, lf) = _dense_in(
        x.reshape(n, D_MODEL), w["n1"], w["wg1"], w["wu1"], w["wd1"], w["n2"], w["wp"], w["bfp"], tm,
        layer, depth, stacked)

    r3 = lambda a: a.reshape(-1, t, a.shape[-1])
    lfp3, u3 = r3(lfp), r3(u)
    if p:
        fill = -(p + t) % LANES
        cat = lambda old, new: jnp.concatenate(
            [old.reshape(b, p, -1).astype(new.dtype), new,
             jnp.zeros((b, fill, new.shape[-1]), new.dtype)], axis=1)
        kf_all, vf_all, vs_all = cat(past_k, r3(kfb)), cat(past_v, r3(vfb)), cat(past_sv, r3(vsb))
        lf_all = cat(jnp.pad(past_lf, ((0, 0), (0, 0), (0, LANES - H_FOX))), lfp3)
        ks_all = cat(past_sk, r3(ksb))
        tq_fox = tq_sb = t
        tk_fox = p + t + fill
        tk_sb = LANES if (LANES % t == 0 and p % LANES == 0) else tk_fox
    else:
        kf_all, vf_all, vs_all, lf_all, ks_all = r3(kfb), r3(vfb), r3(vsb), lfp3, r3(ksb)
        tq_sb = tk_sb = _pick_tile(t, SB_BLOCK)
        tq_fox = tk_fox = _pick_tile(t, FOX_BLOCK)

    kt, vft, vst, bias_max, knorm_max = _prep(kf_all, vf_all, vs_all, lf_all, tk_fox)
    yf = _fox(r3(qf), kt, vft, bias_max, knorm_max, p, tq_fox, tk_fox)
    ys = _sb(r3(qs), ks_all, vst, p, tq_sb, tk_sb)
    buf_pad = jnp.pad(conv_buf, ((0, 0), (HALO - (CONV_W - 1), 0), (0, 0)))
    yc = _conv(u3, buf_pad, w["conv_w"], w["conv_b"], w["ln_g"], w["ln_b"], _pick_tile(t, CONV_ROWS))

    flat = lambda a: a.reshape(n, a.shape[-1])
    xo = _dense_out(x1, flat(yf), flat(yc), flat(ys), w["wo"], w["n3"], w["wg2"], w["wu2"], w["wd2"],
                    final_norm, tm, layer == depth - 1)
    return xo.reshape(b, t, D_MODEL), (kf, vf, ks, vs, lf), u3[:, t - (CONV_W - 1):, :]


def _layer_weights(l, ffn_norm, ffn_gate, ffn_up, ffn_down, mix_norm, w_in, b_forget,
                   conv_w, conv_b, conv_ln_g, conv_ln_b, w_out):
    wi = w_in[l]
    off_f = 3 * D_FOX
    off_glu = off_f + H_FOX
    off_qc = off_glu + 2 * C_CONV
    wp = jnp.concatenate([wi[:, :off_f], jnp.pad(wi[:, off_f:off_glu], ((0, 0), (0, LANES - H_FOX))),
                          wi[:, off_glu:off_qc], wi[:, off_qc:]], axis=1).astype(BF16)
    row = lambda a: a.reshape(1, -1).astype(F32)
    return dict(
        n1=row(ffn_norm[l, 0]), wg1=ffn_gate[l, 0].astype(BF16), wu1=ffn_up[l, 0].astype(BF16),
        wd1=ffn_down[l, 0].astype(BF16), n2=row(mix_norm[l]), wp=wp,
        bfp=jnp.pad(row(b_forget[l]), ((0, 0), (0, LANES - H_FOX))),
        conv_w=conv_w[l], conv_b=row(conv_b[l]), ln_g=row(conv_ln_g[l]), ln_b=row(conv_ln_b[l]),
        wo=w_out[l].astype(BF16), n3=row(ffn_norm[l, 1]), wg2=ffn_gate[l, 1].astype(BF16),
        wu2=ffn_up[l, 1].astype(BF16), wd2=ffn_down[l, 1].astype(BF16))


def kernel(x_prompt, x_sample, cache_fox_k, cache_fox_v, cache_fox_logf, cache_sb_k, cache_sb_v, state_conv,
           ffn_norm, ffn_gate, ffn_up, ffn_down, mix_norm, w_in, b_forget, conv_w, conv_b, conv_ln_g,
           conv_ln_b, w_out, final_norm):
    depth = w_in.shape[0]
    bp = x_prompt.shape[0]
    dt = x_prompt.dtype
    empty = (jnp.zeros((bp, 0, H_FOX, HEAD_DIM), dt), jnp.zeros((bp, 0, H_FOX, HEAD_DIM), dt),
             jnp.zeros((bp, 0, H_FOX), dt), jnp.zeros((bp, 0, H_SB, HEAD_DIM), dt),
             jnp.zeros((bp, 0, H_SB, HEAD_DIM), dt), jnp.zeros((bp, CONV_W - 1, C_CONV), dt))
    fn = final_norm.reshape(1, -1).astype(F32)
    xp, xs = x_prompt, x_sample
    stacked_p = stacked_s = None
    conv_p, conv_s = [], []
    for l in range(depth):
        w = _layer_weights(l, ffn_norm, ffn_gate, ffn_up, ffn_down, mix_norm, w_in, b_forget,
                           conv_w, conv_b, conv_ln_g, conv_ln_b, w_out)
        xp, stacked_p, cp = _layer(xp, empty, w, fn, l, depth, stacked_p)
        cache = (cache_fox_k[l], cache_fox_v[l], cache_fox_logf[l], cache_sb_k[l], cache_sb_v[l],
                 state_conv[l])
        xs, stacked_s, cs = _layer(xs, cache, w, fn, l, depth, stacked_s)
        conv_p.append(cp)
        conv_s.append(cs)

    def outputs(x, stacked, conv):
        b, t, _ = x.shape
        heads = lambda a: a.reshape(depth, b, t, -1, HEAD_DIM)
        kf, vf, ks, vs, lf = stacked
        return heads(kf), heads(vf), lf.reshape(depth, b, t, H_FOX), heads(ks), heads(vs), jnp.stack(conv, 0)

    return (xp, xs) + outputs(xp, stacked_p, conv_p) + outputs(xs, stacked_s, conv_s)
```
